```python
import math
import jax
import jax.numpy as jnp
from jax import lax
import numpy as np

D_MODEL = 1024
BATCH = 4
SEQ = 8192
DEPTH = 2

GRID_W = 64
CTX_LEN = 256
N_MOD = 9
FFN_HIDDEN = 2816
NORM_EPS = 1e-6
ROPE_BASE = 10000.0

MLA_HEADS = 8
QK_NOPE = 64
QK_ROPE = 32
V_DIM = 64
Q_RANK = 384
KV_RANK = 256
Q_BLOCK = 128

S5_CH = 512
S5_GROUP = 16
S5_GROUPS = S5_CH // S5_GROUP
S5_STATE = 64
S5_MIN_STEP = 1e-3
S5_MAX_STEP = 1e-1

HY_CH = 512
HY_BANDS = 16
HY_EMB = 1 + 2 * HY_BANDS
HY_FILT_HIDDEN = 64
HY_SHORT = 3
HY_DECAY_TARGET = 1e-2
HY_MIN_DECAY = math.log(HY_DECAY_TARGET) / 1.5
HY_MAX_DECAY = math.log(HY_DECAY_TARGET) / 0.3

LRU_CH = 512
LRU_BLOCKS = 8
LRU_BLOCK = LRU_CH // LRU_BLOCKS
LRU_CONV = 4
LRU_C = 8.0

EVEN_IN = Q_RANK + KV_RANK + QK_ROPE + S5_CH
EVEN_MIX = MLA_HEADS * V_DIM + S5_CH
ODD_IN = 3 * HY_CH + 2 * LRU_CH
ODD_MIX = HY_CH + LRU_CH
N_EVEN = (DEPTH + 1) // 2
N_ODD = DEPTH // 2

kernel_name = 'hybrid_mla_s5_hyena_rglru_diffusion_trunk'


def rmsnorm(x, g):
    xf = x.astype(jnp.float32)
    y = xf * lax.rsqrt(jnp.mean(xf * xf, axis=-1, keepdims=True) + NORM_EPS)
    return (y * g.astype(jnp.float32)).astype(x.dtype)


def modulate(h, shift, scale):
    return h * (1 + scale) + shift


def swiglu(h, w_gate, w_up, w_down):
    return (jax.nn.silu(h @ w_gate) * (h @ w_up)) @ w_down


def adaln_params(cond, w, b):
    m = jax.nn.silu(cond) @ w + b
    return m.reshape(cond.shape[0], 1, N_MOD, D_MODEL)


def ffn_half_step(h, m, k, g, w_gate, w_up, w_down):
    shift, scale, gate = m[:, :, 3 * k], m[:, :, 3 * k + 1], m[:, :, 3 * k + 2]
    hn = modulate(rmsnorm(h, g), shift, scale)
    return h + 0.5 * gate * swiglu(hn, w_gate, w_up, w_down)


def grid_rope_tables(n_tok):
    rows = n_tok // GRID_W
    row = jnp.repeat(jnp.arange(rows, dtype=jnp.float32), GRID_W)
    col = jnp.tile(jnp.arange(GRID_W, dtype=jnp.float32), rows)
    n_freq = QK_ROPE // 4
    inv_freq = ROPE_BASE ** (-jnp.arange(n_freq, dtype=jnp.float32) / n_freq)
    ang = jnp.concatenate([row[:, None] * inv_freq, col[:, None] * inv_freq], axis=-1)
    return jnp.cos(ang), jnp.sin(ang)


def apply_rope(x, cos, sin):
    x1, x2 = jnp.split(x.astype(jnp.float32), 2, axis=-1)
    return jnp.concatenate([x1 * cos - x2 * sin, x2 * cos + x1 * sin], axis=-1).astype(x.dtype)


def depthwise_conv(x, w, b, pad):
    y = lax.conv_general_dilated(x, w[:, None, :].astype(x.dtype), window_strides=(1,), padding=[pad],
                                 dimension_numbers=('NWC', 'WIO', 'NWC'), feature_group_count=x.shape[-1])
    return y + b.astype(x.dtype)


def _affine_combine(e1, e2):
    a1, b1 = e1
    a2, b2 = e2
    return a1 * a2, a2 * b1 + b2


def linear_scan(a, b, h0, reverse):
    if h0 is not None:
        idx = -1 if reverse else 0
        b = b.at[:, idx].add(a[:, idx] * h0)
    _, h = lax.associative_scan(_affine_combine, (a, b), reverse=reverse, axis=1)
    return h


def mla_queries(cq, q_norm, w_uq, cos, sin):
    B, n, _ = cq.shape
    q = (rmsnorm(cq, q_norm) @ w_uq).reshape(B, n, MLA_HEADS, QK_NOPE + QK_ROPE)
    if cos is None:
        return q
    q_rope = apply_rope(q[..., QK_NOPE:], cos[None, :, None], sin[None, :, None])
    return jnp.concatenate([q[..., :QK_NOPE], q_rope], axis=-1)


def mla_keys_values(ckv, kr, kv_norm, w_ukv, cos, sin):
    B, n, _ = ckv.shape
    kv = (rmsnorm(ckv, kv_norm) @ w_ukv).reshape(B, n, MLA_HEADS, QK_NOPE + V_DIM)
    if cos is not None:
        kr = apply_rope(kr, cos[None], sin[None])
    k = jnp.concatenate([kv[..., :QK_NOPE], jnp.broadcast_to(kr[:, :, None, :], (B, n, MLA_HEADS, QK_ROPE))], axis=-1)
    return k, kv[..., QK_NOPE:]


def context_attention(q, k, v):
    scale = (QK_NOPE + QK_ROPE) ** -0.5
    s = jnp.einsum('bqhd,bkhd->bhqk', q, k).astype(jnp.float32) * scale
    p = jax.nn.softmax(s, axis=-1).astype(v.dtype)
    return jnp.einsum('bhqk,bkhd->bqhd', p, v)


def latent_attention(q, k_lat, v_lat, k_ctx, v_ctx):
    B, n, H, dk = q.shape
    n_ctx = k_ctx.shape[1]
    scale = dk ** -0.5
    qb = jnp.moveaxis(q.reshape(B, n // Q_BLOCK, Q_BLOCK, H, dk), 1, 0)

    def one_block(qi):
        s = jnp.concatenate([jnp.einsum('bqhd,bkhd->bhqk', qi, k_ctx),
                             jnp.einsum('bqhd,bkhd->bhqk', qi, k_lat)], axis=-1)
        p = jax.nn.softmax(s.astype(jnp.float32) * scale, axis=-1).astype(v_lat.dtype)
        return (jnp.einsum('bhqk,bkhd->bqhd', p[..., :n_ctx], v_ctx)
                + jnp.einsum('bhqk,bkhd->bqhd', p[..., n_ctx:], v_lat))

    o = lax.map(one_block, qb)
    return jnp.moveaxis(o, 0, 1).reshape(B, n, H, v_lat.shape[-1])


def s5_discretise(lam_re, lam_im, log_step, b_re, b_im):
    f32 = jnp.float32
    lam = lax.complex(lam_re.astype(f32), lam_im.astype(f32))
    step = jnp.exp(log_step.astype(f32))[:, None]
    a_bar = jnp.exp(lam * step)
    b = lax.complex(b_re.astype(f32), b_im.astype(f32))
    b_bar = ((a_bar - 1.0) / lam)[..., None] * b
    return a_bar, b_bar


def s5_stream(u, lam_re, lam_im, log_step, b_re, b_im, c_re, c_im, d_skip, w_glu, b_glu, h0, need_out):
    f32 = jnp.float32
    B, n, _ = u.shape
    uf = u.astype(f32)
    ug = uf.reshape(B, n, S5_GROUPS, S5_GROUP).astype(jnp.complex64)
    y = None
    finals = []
    for dr in range(2):
        a_bar, b_bar = s5_discretise(lam_re[dr], lam_im[dr], log_step[dr], b_re[dr], b_im[dr])
        bu = jnp.einsum('gpc,bngc->bngp', b_bar, ug)
        a = jnp.broadcast_to(a_bar[None, None], (1, n, S5_GROUPS, S5_STATE))
        h = linear_scan(a, bu, None if h0 is None else h0[dr], reverse=dr == 1)
        finals.append(h[:, 0] if dr == 1 else h[:, -1])
        if need_out:
            c_mat = lax.complex(c_re[dr].astype(f32), c_im[dr].astype(f32))
            yd = jnp.real(jnp.einsum('gcp,bngp->bngc', c_mat, h))
            y = yd if y is None else y + yd
    if not need_out:
        return None, finals
    y = y.reshape(B, n, S5_CH) + d_skip.astype(f32) * uf
    y = jax.nn.gelu(y)
    y = y * jax.nn.sigmoid(y @ w_glu.astype(f32) + b_glu.astype(f32))
    return y.astype(u.dtype), finals


def hyena_filters(n, w1, b1, w2, b2, w3, sin_freq):
    f32 = jnp.float32
    t = jnp.linspace(0.0, 1.0, n, dtype=f32)[:, None]
    w = 2.0 * math.pi * jnp.arange(n, dtype=f32)[:, None] / n
    bands = jnp.linspace(1e-4, HY_BANDS - 1, HY_BANDS, dtype=f32)[None, :]
    z = jnp.concatenate([t, jnp.cos(bands * w), -jnp.sin(bands * w)], axis=-1)
    h = jnp.sin(sin_freq[0].astype(f32) * (z @ w1.astype(f32) + b1.astype(f32)))
    h = jnp.sin(sin_freq[1].astype(f32) * (h @ w2.astype(f32) + b2.astype(f32)))
    h = (h @ w3.astype(f32)).reshape(n, 2, HY_CH)
    deltas = jnp.abs(jnp.linspace(HY_MIN_DECAY, HY_MAX_DECAY, HY_CH, dtype=f32))
    return h * jnp.exp(-t * deltas)[:, None, :]


def bidir_fftconv(u, h_fwd, h_bwd, bias):
    n = u.shape[1]
    filt_circ = jnp.concatenate([h_fwd, jnp.zeros_like(h_fwd[:1]), h_bwd[:0:-1]], axis=0)
    uf = u.astype(jnp.float32)
    spec = jnp.fft.rfft(uf, n=2 * n, axis=1) * jnp.fft.rfft(filt_circ, axis=0)[None]
    y = jnp.fft.irfft(spec, n=2 * n, axis=1)[:, :n]
    return (y + uf * bias.astype(jnp.float32)).astype(u.dtype)


def hyena_stream(u, conv_w, conv_b, w1, b1, w2, b2, w3, sin_freq, bias):
    n = u.shape[1]
    u = depthwise_conv(u, conv_w, conv_b, (1, 1))
    x0, x1, v = jnp.split(u, 3, axis=-1)
    filt = hyena_filters(n, w1, b1, w2, b2, w3, sin_freq)
    return x0 * bidir_fftconv(v * x1, filt[:, 0], filt[:, 1], bias)


def rglru_stream(u_x, u_gate, conv_w, conv_b, w_a, b_a, w_x, b_x, lam, h0, need_out):
    f32 = jnp.float32
    B, n, _ = u_x.shape
    xc = depthwise_conv(u_x, conv_w, conv_b, (2, 1)).astype(f32)
    xb = xc.reshape(B, n, LRU_BLOCKS, LRU_BLOCK)
    y = None
    finals = []
    for dr in range(2):
        r = jax.nn.sigmoid(jnp.einsum('bnhi,hij->bnhj', xb, w_a[dr].astype(f32)).reshape(B, n, LRU_CH) + b_a[dr].astype(f32))
        ig = jax.nn.sigmoid(jnp.einsum('bnhi,hij->bnhj', xb, w_x[dr].astype(f32)).reshape(B, n, LRU_CH) + b_x[dr].astype(f32))
        log_a = -LRU_C * r * jax.nn.softplus(-lam[dr].astype(f32))
        a = jnp.exp(log_a)
        b = jnp.sqrt(-jnp.expm1(2.0 * log_a)) * (ig * xc)
        h = linear_scan(a, b, None if h0 is None else h0[dr], reverse=dr == 1)
        finals.append(h[:, 0] if dr == 1 else h[:, -1])
        if need_out:
            y = h if y is None else y + h
    if not need_out:
        return None, finals
    return (y * jax.nn.gelu(u_gate.astype(f32))).astype(u_x.dtype), finals


def even_mixer(h_ctx, h_lat, cos, sin, w_in, q_norm, w_uq, kv_norm, w_ukv,
               s5_lambda_re, s5_lambda_im, s5_log_step, s5_b_re, s5_b_im, s5_c_re, s5_c_im,
               s5_d, s5_w_glu, s5_b_glu, w_out, need_ctx_out):
    cut = [Q_RANK, Q_RANK + KV_RANK, Q_RANK + KV_RANK + QK_ROPE]
    cq_c, ckv_c, kr_c, s_c = jnp.split(h_ctx @ w_in, cut, axis=-1)
    cq_l, ckv_l, kr_l, s_l = jnp.split(h_lat @ w_in, cut, axis=-1)
    B, n, _ = h_lat.shape
    k_ctx, v_ctx = mla_keys_values(ckv_c, kr_c, kv_norm, w_ukv, None, None)
    k_lat, v_lat = mla_keys_values(ckv_l, kr_l, kv_norm, w_ukv, cos, sin)
    q_lat = mla_queries(cq_l, q_norm, w_uq, cos, sin)
    o_lat = latent_attention(q_lat, k_lat, v_lat, k_ctx, v_ctx).reshape(B, n, MLA_HEADS * V_DIM)
    s5p = (s5_lambda_re, s5_lambda_im, s5_log_step, s5_b_re, s5_b_im, s5_c_re, s5_c_im, s5_d, s5_w_glu, s5_b_glu)
    s_ctx_out, ctx_finals = s5_stream(s_c, *s5p, None, need_ctx_out)
    s_lat_out, _ = s5_stream(s_l, *s5p, ctx_finals, True)
    y_lat = jnp.concatenate([o_lat, s_lat_out], axis=-1) @ w_out
    y_ctx = None
    if need_ctx_out:
        n_ctx = h_ctx.shape[1]
        q_ctx = mla_queries(cq_c, q_norm, w_uq, None, None)
        o_ctx = context_attention(q_ctx, k_ctx, v_ctx).reshape(h_ctx.shape[0], n_ctx, MLA_HEADS * V_DIM)
        y_ctx = jnp.concatenate([o_ctx, s_ctx_out], axis=-1) @ w_out
    return y_ctx, y_lat


def odd_mixer(h_ctx, h_lat, w_in, hy_conv_w, hy_conv_b, hy_filt_w1, hy_filt_b1, hy_filt_w2, hy_filt_b2,
              hy_filt_w3, hy_sin_freq, hy_bias, lru_conv_w, lru_conv_b, lru_w_a, lru_b_a, lru_w_x, lru_b_x,
              lru_lambda, w_out, need_ctx_out):
    cut = [3 * HY_CH, 3 * HY_CH + LRU_CH]
    hy_c, lx_c, lg_c = jnp.split(h_ctx @ w_in, cut, axis=-1)
    hy_l, lx_l, lg_l = jnp.split(h_lat @ w_in, cut, axis=-1)
    hyp = (hy_conv_w, hy_conv_b, hy_filt_w1, hy_filt_b1, hy_filt_w2, hy_filt_b2, hy_filt_w3, hy_sin_freq, hy_bias)
    lrup = (lru_conv_w, lru_conv_b, lru_w_a, lru_b_a, lru_w_x, lru_b_x, lru_lambda)
    r_ctx, ctx_finals = rglru_stream(lx_c, lg_c, *lrup, None, need_ctx_out)
    r_lat, _ = rglru_stream(lx_l, lg_l, *lrup, ctx_finals, True)
    y_lat = jnp.concatenate([hyena_stream(hy_l, *hyp), r_lat], axis=-1) @ w_out
    y_ctx = None
    if need_ctx_out:
        y_ctx = jnp.concatenate([hyena_stream(hy_c, *hyp), r_ctx], axis=-1) @ w_out
    return y_ctx, y_lat


def setup_inputs(seed: int = 0) -> dict:
    key = jax.random.key(seed)
    keys = iter(jax.random.split(key, 64))
    f32 = jnp.float32

    def nrm(shape, scale):
        return scale * jax.random.normal(next(keys), shape, f32)

    def gain(shape):
        return 1.0 + nrm(shape, 0.02)

    def unif(shape, lo, hi):
        return jax.random.uniform(next(keys), shape, f32, lo, hi)

    D, F, NE, NO = D_MODEL, FFN_HIDDEN, N_EVEN, N_ODD
    lru_a0 = unif((NO, 2, LRU_CH), 0.9, 0.999) ** (1.0 / LRU_C)
    return {
        'x': nrm((BATCH, SEQ, D), 1.0),
        'c': nrm((BATCH, D), 1.0),
        'ctx': nrm((BATCH, CTX_LEN, D), 1.0),
        'c_ctx': nrm((D,), 1.0),
        'mod_w': nrm((DEPTH, D, N_MOD * D), 0.5 * D ** -0.5),
        'mod_b': nrm((DEPTH, N_MOD * D), 0.02),
        'norm_ffn1': gain((DEPTH, D)),
        'norm_mix': gain((DEPTH, D)),
        'norm_ffn2': gain((DEPTH, D)),
        'ffn1_w_gate': nrm((DEPTH, D, F), D ** -0.5),
        'ffn1_w_up': nrm((DEPTH, D, F), D ** -0.5),
        'ffn1_w_down': nrm((DEPTH, F, D), F ** -0.5),
        'ffn2_w_gate': nrm((DEPTH, D, F), D ** -0.5),
        'ffn2_w_up': nrm((DEPTH, D, F), D ** -0.5),
        'ffn2_w_down': nrm((DEPTH, F, D), F ** -0.5),
        'ev_w_in': nrm((NE, D, EVEN_IN), D ** -0.5),
        'mla_q_norm': gain((NE, Q_RANK)),
        'mla_w_uq': nrm((NE, Q_RANK, MLA_HEADS * (QK_NOPE + QK_ROPE)), Q_RANK ** -0.5),
        'mla_kv_norm': gain((NE, KV_RANK)),
        'mla_w_ukv': nrm((NE, KV_RANK, MLA_HEADS * (QK_NOPE + V_DIM)), KV_RANK ** -0.5),
        's5_lambda_re': -0.5 + nrm((NE, 2, S5_GROUPS, S5_STATE), 0.01),
        's5_lambda_im': math.pi * jnp.arange(S5_STATE, dtype=f32) + nrm((NE, 2, S5_GROUPS, S5_STATE), 0.01),
        's5_log_step': unif((NE, 2, S5_GROUPS), math.log(S5_MIN_STEP), math.log(S5_MAX_STEP)),
        's5_b_re': nrm((NE, 2, S5_GROUPS, S5_STATE, S5_GROUP), (2 * S5_GROUP) ** -0.5),
        's5_b_im': nrm((NE, 2, S5_GROUPS, S5_STATE, S5_GROUP), (2 * S5_GROUP) ** -0.5),
        's5_c_re': nrm((NE, 2, S5_GROUPS, S5_GROUP, S5_STATE), S5_STATE ** -0.5),
        's5_c_im': nrm((NE, 2, S5_GROUPS, S5_GROUP, S5_STATE), S5_STATE ** -0.5),
        's5_d': nrm((NE, S5_CH), 0.5),
        's5_w_glu': nrm((NE, S5_CH, S5_CH), S5_CH ** -0.5),
        's5_b_glu': nrm((NE, S5_CH), 0.02),
        'ev_w_out': nrm((NE, EVEN_MIX, D), EVEN_MIX ** -0.5),
        'od_w_in': nrm((NO, D, ODD_IN), D ** -0.5),
        'hy_conv_w': nrm((NO, HY_SHORT, 3 * HY_CH), HY_SHORT ** -0.5),
        'hy_conv_b': nrm((NO, 3 * HY_CH), 0.02),
        'hy_filt_w1': nrm((NO, HY_EMB, HY_FILT_HIDDEN), HY_EMB ** -0.5),
        'hy_filt_b1': nrm((NO, HY_FILT_HIDDEN), 0.02),
        'hy_filt_w2': nrm((NO, HY_FILT_HIDDEN, HY_FILT_HIDDEN), HY_FILT_HIDDEN ** -0.5),
        'hy_filt_b2': nrm((NO, HY_FILT_HIDDEN), 0.02),
        'hy_filt_w3': nrm((NO, HY_FILT_HIDDEN, 2 * HY_CH), 0.02 * HY_FILT_HIDDEN ** -0.5),
        'hy_sin_freq': gain((NO, 2, HY_FILT_HIDDEN)),
        'hy_bias': nrm((NO, HY_CH), 0.5),
        'lru_conv_w': nrm((NO, LRU_CONV, LRU_CH), LRU_CONV ** -0.5),
        'lru_conv_b': nrm((NO, LRU_CH), 0.02),
        'lru_w_a': nrm((NO, 2, LRU_BLOCKS, LRU_BLOCK, LRU_BLOCK), LRU_BLOCK ** -0.5),
        'lru_b_a': nrm((NO, 2, LRU_CH), 0.02),
        'lru_w_x': nrm((NO, 2, LRU_BLOCKS, LRU_BLOCK, LRU_BLOCK), LRU_BLOCK ** -0.5),
        'lru_b_x': nrm((NO, 2, LRU_CH), 0.02),
        'lru_lambda': jnp.log(lru_a0) - jnp.log1p(-lru_a0),
        'od_w_out': nrm((NO, ODD_MIX, D), ODD_MIX ** -0.5),
        'final_norm': gain((D,)),
    }


def reference(x, c, ctx, c_ctx, mod_w, mod_b, norm_ffn1, norm_mix, norm_ffn2,
              ffn1_w_gate, ffn1_w_up, ffn1_w_down, ffn2_w_gate, ffn2_w_up, ffn2_w_down,
              ev_w_in, mla_q_norm, mla_w_uq, mla_kv_norm, mla_w_ukv,
              s5_lambda_re, s5_lambda_im, s5_log_step, s5_b_re, s5_b_im, s5_c_re, s5_c_im,
              s5_d, s5_w_glu, s5_b_glu, ev_w_out,
              od_w_in, hy_conv_w, hy_conv_b, hy_filt_w1, hy_filt_b1, hy_filt_w2, hy_filt_b2,
              hy_filt_w3, hy_sin_freq, hy_bias,
              lru_conv_w, lru_conv_b, lru_w_a, lru_b_a, lru_w_x, lru_b_x, lru_lambda, od_w_out,
              final_norm):
    n_lat = x.shape[1]
    cos, sin = grid_rope_tables(n_lat)
    lat, cx = x, ctx
    for i in range(DEPTH):
        last = i == DEPTH - 1
        m_lat = adaln_params(c, mod_w[i], mod_b[i])
        m_ctx = adaln_params(c_ctx[None, :], mod_w[i], mod_b[i])
        ffn1 = (ffn1_w_gate[i], ffn1_w_up[i], ffn1_w_down[i])
        ffn2 = (ffn2_w_gate[i], ffn2_w_up[i], ffn2_w_down[i])
        lat = ffn_half_step(lat, m_lat, 0, norm_ffn1[i], *ffn1)
        cx = ffn_half_step(cx, m_ctx, 0, norm_ffn1[i], *ffn1)
        h_lat = modulate(rmsnorm(lat, norm_mix[i]), m_lat[:, :, 3], m_lat[:, :, 4])
        h_ctx = modulate(rmsnorm(cx, norm_mix[i]), m_ctx[:, :, 3], m_ctx[:, :, 4])
        j = i // 2
        if i % 2 == 0:
            y_ctx, y_lat = even_mixer(h_ctx, h_lat, cos, sin, ev_w_in[j], mla_q_norm[j], mla_w_uq[j],
                                      mla_kv_norm[j], mla_w_ukv[j], s5_lambda_re[j], s5_lambda_im[j],
                                      s5_log_step[j], s5_b_re[j], s5_b_im[j], s5_c_re[j], s5_c_im[j],
                                      s5_d[j], s5_w_glu[j], s5_b_glu[j], ev_w_out[j], not last)
        else:
            y_ctx, y_lat = odd_mixer(h_ctx, h_lat, od_w_in[j], hy_conv_w[j], hy_conv_b[j], hy_filt_w1[j],
                                     hy_filt_b1[j], hy_filt_w2[j], hy_filt_b2[j], hy_filt_w3[j],
                                     hy_sin_freq[j], hy_bias[j], lru_conv_w[j], lru_conv_b[j], lru_w_a[j],
                                     lru_b_a[j], lru_w_x[j], lru_b_x[j], lru_lambda[j], od_w_out[j], not last)
        lat = lat + m_lat[:, :, 5] * y_lat
        lat = ffn_half_step(lat, m_lat, 2, norm_ffn2[i], *ffn2)
        if not last:
            cx = cx + m_ctx[:, :, 5] * y_ctx
            cx = ffn_half_step(cx, m_ctx, 2, norm_ffn2[i], *ffn2)
    return rmsnorm(lat, final_norm)
```

```python
import functools
import math

import numpy as np
import jax
import jax.numpy as jnp
from jax import lax
from jax.experimental import pallas as pl
from jax.experimental.pallas import tpu as pltpu

F32 = jnp.float32
BF16 = jnp.bfloat16

GRID_W = 64
N_MOD = 9
NORM_EPS = 1e-6
ROPE_BASE = 10000.0
MLA_HEADS = 8
QK_NOPE = 64
QK_ROPE = 32
V_DIM = 64
Q_RANK = 384
KV_RANK = 256
S5_GROUP = 16
S5_STATE = 64
HY_BANDS = 16
HY_DECAY_TARGET = 1e-2
HY_MIN_DECAY = math.log(HY_DECAY_TARGET) / 1.5
HY_MAX_DECAY = math.log(HY_DECAY_TARGET) / 0.3
LRU_BLOCKS = 8
LRU_C = 8.0

LANES = 128
SUBLANES = 8
VMEM_BYTES_V7X = 64 * 1024 * 1024
VMEM_LIMIT = VMEM_BYTES_V7X - 8 * 1024 * 1024

HEAD_LANES = LANES
ONES_LANE = V_DIM
S5_CHUNK = 16
FFT_N2 = 128
NEG_BIG = -1e30


def _cparams(sem):
    return pltpu.CompilerParams(dimension_semantics=sem, vmem_limit_bytes=VMEM_LIMIT)


def _dot(a, b):
    return jnp.dot(a, b, preferred_element_type=F32)


def _split(x):
    hi = x.astype(BF16)
    lo = (x - hi.astype(F32)).astype(BF16)
    return hi, lo


def _dot3(a_hi, a_lo, b_hi, b_lo):
    return _dot(a_hi, b_hi) + _dot(a_lo, b_hi) + _dot(a_hi, b_lo)


def _sigmoid(x):
    return 1.0 / (1.0 + jnp.exp(-x))


def _gelu_tanh(x):
    return 0.5 * x * (1.0 + jnp.tanh(math.sqrt(2.0 / math.pi) * (x + 0.044715 * (x * x * x))))


def _rms(x, g):
    return x * lax.rsqrt(jnp.mean(x * x, axis=-1, keepdims=True) + NORM_EPS) * g


def _row_tile(*lengths):
    for t in (512, 256, 128, 64, 32, 16, 8):
        if all(l % t == 0 for l in lengths):
            return t
    raise ValueError(f"no row tile divides {lengths}")


def _adaln_kernel(c_ref, w_ref, b_ref, o_ref):
    c = c_ref[...]
    s_hi, s_lo = _split(c * _sigmoid(c))
    w_hi, w_lo = _split(w_ref[...])
    o_ref[...] = _dot3(s_hi, s_lo, w_hi, w_lo) + b_ref[...]


def _adaln(cond8, mod_w, mod_b):
    depth, d, nd = mod_w.shape
    tn = d
    out = pl.pallas_call(
        _adaln_kernel,
        grid=(depth, nd // tn),
        in_specs=[
            pl.BlockSpec((SUBLANES, d), lambda l, j: (0, 0)),
            pl.BlockSpec((None, d, tn), lambda l, j: (l, 0, j)),
            pl.BlockSpec((None, 1, tn), lambda l, j: (l, 0, j)),
        ],
        out_specs=pl.BlockSpec((None, SUBLANES, tn), lambda l, j: (l, 0, j)),
        out_shape=jax.ShapeDtypeStruct((depth, SUBLANES, nd), F32),
        compiler_params=_cparams(("arbitrary", "arbitrary")),
        name="adaln",
    )(cond8, mod_w, mod_b.reshape(depth, 1, nd))
    return out.reshape(depth, SUBLANES, N_MOD, d)


def _ffn_kernel(x_ref, m_ref, g_ref, wg_ref, wu_ref, wd_ref, *rest, k, tf, final):
    if final:
        fn_ref, o_ref = rest
    else:
        (o_ref,) = rest
    x = x_ref[...]
    m = m_ref[...]
    shift, scale, gate = m[3 * k:3 * k + 1], m[3 * k + 1:3 * k + 2], m[3 * k + 2:3 * k + 3]
    hb = (_rms(x, g_ref[...]) * (1.0 + scale) + shift).astype(BF16)
    acc = jnp.zeros(x.shape, F32)
    for f in range(wg_ref.shape[1] // tf):
        sl = slice(f * tf, (f + 1) * tf)
        gt = _dot(hb, wg_ref[:, sl])
        up = _dot(hb, wu_ref[:, sl])
        acc = acc + _dot((gt * _sigmoid(gt) * up).astype(BF16), wd_ref[sl, :])
    y = x + (0.5 * gate) * acc
    if final:
        y = _rms(y, fn_ref[...])
    o_ref[...] = y


def _ffn(tok, m, k, g, wg, wu, wd, *, n_rows, tm, mod_row, final_g=None):
    d = tok.shape[1]
    f = wg.shape[1]
    tf = 256 if f % 256 == 0 else LANES
    final = final_g is not None
    full = lambda shape: pl.BlockSpec(shape, lambda i: (0,) * len(shape), pipeline_mode=pl.Buffered(1))
    in_specs = [
        pl.BlockSpec((tm, d), lambda i: (i, 0)),
        pl.BlockSpec((None, N_MOD, d), lambda i: (mod_row(i), 0, 0)),
        full((1, d)), full((d, f)), full((d, f)), full((f, d)),
    ]
    args = [tok, m, g.reshape(1, d), wg, wu, wd]
    if final:
        in_specs.append(full((1, d)))
        args.append(final_g.reshape(1, d))
    return pl.pallas_call(
        functools.partial(_ffn_kernel, k=k, tf=tf, final=final),
        grid=(n_rows // tm,),
        in_specs=in_specs,
        out_specs=pl.BlockSpec((tm, d), lambda i: (i, 0)),
        out_shape=jax.ShapeDtypeStruct((n_rows, d), F32),
        compiler_params=_cparams(("parallel",)),
        name="ffn_final" if final else "ffn",
    )(*args)


def _even_in_kernel(x_ref, m_ref, g_ref, w1_ref, qn_ref, kvn_ref, wq_ref, wqp_ref, wk_ref, wv_ref,
                    c_ref, s_ref, q_out, k_out, v_out, s_out):
    x = x_ref[...]
    m = m_ref[...]
    hb = (_rms(x, g_ref[...]) * (1.0 + m[4:5]) + m[3:4]).astype(BF16)
    p = _dot(hb, w1_ref[...])
    o = 0
    cq = p[:, o:o + Q_RANK]; o += Q_RANK
    ckv = p[:, o:o + KV_RANK]; o += KV_RANK
    kra = p[:, o:o + HEAD_LANES]; o += HEAD_LANES
    krb = p[:, o:o + HEAD_LANES]; o += HEAD_LANES
    s_out[...] = p[:, o:].astype(BF16)
    cos, sin = c_ref[...], s_ref[...]
    cqn = _rms(cq, qn_ref[...]).astype(BF16)
    qa = _dot(cqn, wq_ref[...])
    qb = _dot(cqn, wqp_ref[...])
    ckvn = _rms(ckv, kvn_ref[...]).astype(BF16)
    kn = _dot(ckvn, wk_ref[...])
    vv = _dot(ckvn, wv_ref[...])
    kr = kra * cos + krb * sin
    ones = (lax.broadcasted_iota(jnp.int32, (1, HEAD_LANES), 1) == ONES_LANE).astype(F32)
    scale = (QK_NOPE + QK_ROPE) ** -0.5
    for h in range(MLA_HEADS):
        sl = slice(h * HEAD_LANES, (h + 1) * HEAD_LANES)
        q_out[h] = ((qa[:, sl] * cos + qb[:, sl] * sin) * scale).astype(BF16)
        k_out[h] = (kn[:, sl] + kr).astype(BF16)
        v_out[h] = (vv[:, sl] + ones).astype(BF16)


def _rope_partner(w):
    half = w.shape[-1] // 2
    return jnp.concatenate([-w[..., half:], w[..., :half]], axis=-1)


def _even_in_weights(w_in, w_uq, w_ukv):
    d = w_in.shape[0]
    o = 0
    w_cq = w_in[:, o:o + Q_RANK]; o += Q_RANK
    w_ckv = w_in[:, o:o + KV_RANK]; o += KV_RANK
    w_kr = w_in[:, o:o + QK_ROPE]; o += QK_ROPE
    w_s = w_in[:, o:]
    pad_l = jnp.zeros((d, QK_NOPE), F32)
    pad_r = jnp.zeros((d, HEAD_LANES - QK_NOPE - QK_ROPE), F32)
    kr_blk = jnp.concatenate([pad_l, w_kr, pad_r], axis=1)
    kr_blk_p = jnp.concatenate([pad_l, _rope_partner(w_kr), pad_r], axis=1)
    w1 = jnp.concatenate([w_cq, w_ckv, kr_blk, kr_blk_p, w_s], axis=1).astype(BF16)

    dk = QK_NOPE + QK_ROPE
    wq = w_uq.reshape(Q_RANK, MLA_HEADS, dk)
    zq = jnp.zeros((Q_RANK, MLA_HEADS, HEAD_LANES - dk), F32)
    wq_pad = jnp.concatenate([wq, zq], axis=-1)
    wq_par = jnp.concatenate([jnp.zeros((Q_RANK, MLA_HEADS, QK_NOPE), F32), _rope_partner(wq[..., QK_NOPE:]), zq], axis=-1)
    wkv = w_ukv.reshape(KV_RANK, MLA_HEADS, QK_NOPE + V_DIM)
    wk = jnp.concatenate([wkv[..., :QK_NOPE], jnp.zeros((KV_RANK, MLA_HEADS, HEAD_LANES - QK_NOPE), F32)], axis=-1)
    wv = jnp.concatenate([wkv[..., QK_NOPE:], jnp.zeros((KV_RANK, MLA_HEADS, HEAD_LANES - V_DIM), F32)], axis=-1)
    flat = lambda w: w.reshape(w.shape[0], MLA_HEADS * HEAD_LANES).astype(BF16)
    return w1, flat(wq_pad), flat(wq_par), flat(wk), flat(wv)


def _rope_tables(n, n_ident):
    rows = n // GRID_W
    row = np.repeat(np.arange(rows, dtype=np.float32), GRID_W)
    col = np.tile(np.arange(GRID_W, dtype=np.float32), rows)
    n_freq = QK_ROPE // 4
    inv_freq = (np.float32(ROPE_BASE) ** (-np.arange(n_freq, dtype=np.float32) / n_freq)).astype(np.float32)
    ang = np.concatenate([row[:, None] * inv_freq, col[:, None] * inv_freq], axis=-1).astype(np.float64)
    ang = np.concatenate([ang, np.zeros((n_ident, QK_ROPE // 2))], axis=0)
    tot = n + n_ident
    pad = HEAD_LANES - QK_NOPE - QK_ROPE
    cos = np.concatenate([np.ones((tot, QK_NOPE)), np.cos(ang), np.cos(ang), np.ones((tot, pad))], axis=1)
    sin = np.concatenate([np.zeros((tot, QK_NOPE)), np.sin(ang), np.sin(ang), np.zeros((tot, pad))], axis=1)
    return jnp.asarray(cos, F32), jnp.asarray(sin, F32)


def _even_in(tok, m, g, weights, q_norm, kv_norm, *, n, tm, mod_row, n_lat_rows):
    t, d = tok.shape
    w1, wq, wqp, wk, wv = weights
    cos, sin = _rope_tables(n, tm)
    tiles_lat = n_lat_rows // tm
    tpb = n // tm
    tab_row = lambda i: jnp.where(i < tiles_lat, i % tpb, tpb)
    full = lambda a: pl.BlockSpec(a.shape, lambda i: (0,) * a.ndim)
    hl = MLA_HEADS * HEAD_LANES
    s_ch = w1.shape[1] - Q_RANK - KV_RANK - 2 * HEAD_LANES
    qn, kvn = q_norm.reshape(1, Q_RANK), kv_norm.reshape(1, KV_RANK)
    gg = g.reshape(1, d)
    head_spec = pl.BlockSpec((MLA_HEADS, tm, HEAD_LANES), lambda i: (0, i, 0))
    head_shape = jax.ShapeDtypeStruct((MLA_HEADS, t, HEAD_LANES), BF16)
    return pl.pallas_call(
        _even_in_kernel,
        grid=(t // tm,),
        in_specs=[
            pl.BlockSpec((tm, d), lambda i: (i, 0)),
            pl.BlockSpec((None, N_MOD, d), lambda i: (mod_row(i), 0, 0)),
            full(gg), full(w1), full(qn), full(kvn), full(wq), full(wqp), full(wk), full(wv),
            pl.BlockSpec((tm, HEAD_LANES), lambda i: (tab_row(i), 0)),
            pl.BlockSpec((tm, HEAD_LANES), lambda i: (tab_row(i), 0)),
        ],
        out_specs=[head_spec, head_spec, head_spec, pl.BlockSpec((tm, s_ch), lambda i: (i, 0))],
        out_shape=[head_shape, head_shape, head_shape, jax.ShapeDtypeStruct((t, s_ch), BF16)],
        compiler_params=_cparams(("parallel",)),
        name="even_in",
    )(tok, m, gg, w1, qn, kvn, wq, wqp, wk, wv, cos, sin)


def _attn_kernel(q_ref, kl_ref, vl_ref, kc_ref, vc_ref, o_ref, *, tk, n_q_lat):
    q = q_ref[...]
    tq = q.shape[0]

    def step(k, v, carry):
        m, acc = carry
        s = lax.dot_general(q, k, (((1,), (1,)), ((), ())), preferred_element_type=F32)
        m_new = jnp.maximum(m, jnp.max(s, axis=-1, keepdims=True))
        p = jnp.exp(s - m_new)
        acc = jnp.exp(m - m_new) * acc + _dot(p.astype(BF16), v)
        return m_new, acc

    carry = (jnp.full((tq, 1), NEG_BIG, F32), jnp.zeros((tq, HEAD_LANES), F32))
    carry = step(kc_ref[...], vc_ref[...], carry)
    n_chunks = jnp.where(pl.program_id(2) < n_q_lat, kl_ref.shape[0] // tk, 0)

    def body(c, carry):
        start = pl.multiple_of(c * tk, tk)
        return step(kl_ref[pl.ds(start, tk), :], vl_ref[pl.ds(start, tk), :], carry)

    _, acc = lax.fori_loop(0, n_chunks, body, carry)
    o_ref[...] = (acc / acc[:, ONES_LANE:ONES_LANE + 1]).astype(BF16)


def _attention(q, k, v, *, b, n, nc):
    h, t, _ = q.shape
    tq = nc
    n_q_lat = n // tq
    tk = _row_tile(n)
    lat_blocks = (b * n) // nc
    q_idx = lambda bi, hi, i: (hi, jnp.where(i < n_q_lat, bi * n_q_lat + i, lat_blocks + bi), 0)
    lat_spec = pl.BlockSpec((None, n, HEAD_LANES), lambda bi, hi, i: (hi, bi, 0))
    ctx_spec = pl.BlockSpec((None, nc, HEAD_LANES), lambda bi, hi, i: (hi, lat_blocks + bi, 0))
    return pl.pallas_call(
        functools.partial(_attn_kernel, tk=tk, n_q_lat=n_q_lat),
        grid=(b, h, n_q_lat + 1),
        in_specs=[pl.BlockSpec((None, tq, HEAD_LANES), q_idx), lat_spec, lat_spec, ctx_spec, ctx_spec],
        out_specs=pl.BlockSpec((None, tq, HEAD_LANES), q_idx),
        out_shape=jax.ShapeDtypeStruct((h, t, HEAD_LANES), BF16),
        compiler_params=_cparams(("parallel", "parallel", "arbitrary")),
        name="attention",
    )(q, k, v, k, v)


def _s5_operators(lam_re, lam_im, log_step, b_re, b_im, c_re, c_im, d_skip):
    tc = S5_CHUNK
    g_n, p_n = lam_re.shape[1], lam_re.shape[2]
    lam = lax.complex(lam_re.astype(F32), lam_im.astype(F32))
    step = jnp.exp(log_step.astype(F32))[..., None]
    la = lam * step
    a_bar = jnp.exp(la)
    bb = ((a_bar - 1.0) / lam)[..., None] * lax.complex(b_re.astype(F32), b_im.astype(F32))
    cm = lax.complex(c_re.astype(F32), c_im.astype(F32))
    kk = jnp.arange(tc + 1, dtype=F32)
    apow = jnp.exp(la[..., None] * kk)

    win_f = jnp.einsum('gps,gpc->gscp', apow[0][..., tc - 1::-1][..., :tc], bb[0])
    win_b = jnp.einsum('gps,gpc->gscp', apow[1][..., :tc], bb[1])
    wout_f = jnp.einsum('gcp,gpt->gptc', cm[0], apow[0][..., 1:])
    wout_b = jnp.einsum('gcp,gpt->gptc', cm[1], apow[1][..., tc:0:-1])
    kf = jnp.real(jnp.einsum('gop,gpk,gpi->gkoi', cm[0], apow[0][..., :tc], bb[0]))
    kb = jnp.real(jnp.einsum('gop,gpk,gpi->gkoi', cm[1], apow[1][..., :tc], bb[1]))
    s_idx = jnp.arange(tc)[:, None]
    t_idx = jnp.arange(tc)[None, :]
    lag = t_idx - s_idx
    kf_st = jnp.where((lag >= 0)[None, :, :, None, None], kf[:, jnp.clip(lag, 0, tc - 1)], 0.0)
    kb_st = jnp.where((lag <= 0)[None, :, :, None, None], kb[:, jnp.clip(-lag, 0, tc - 1)], 0.0)
    eye_t = jnp.eye(tc, dtype=F32)[None, :, :, None, None]
    eye_c = jnp.eye(S5_GROUP, dtype=F32)[None, None, None]
    dsk = d_skip.astype(F32).reshape(g_n, 1, 1, S5_GROUP, 1)
    loc = kf_st + kb_st + eye_t * eye_c * dsk
    loc = jnp.transpose(loc, (0, 1, 4, 2, 3))

    gc = tc * S5_GROUP
    win_f, win_b = win_f.reshape(g_n, gc, p_n), win_b.reshape(g_n, gc, p_n)
    wout_f, wout_b = wout_f.reshape(g_n, p_n, gc), wout_b.reshape(g_n, p_n, gc)
    loc = loc.reshape(g_n, gc, gc)

    def pair_diag(x):
        x = x.reshape(g_n // 2, 2, x.shape[1], x.shape[2])
        z = jnp.zeros_like(x[:, 0])
        return jnp.concatenate([jnp.concatenate([x[:, 0], z], axis=2), jnp.concatenate([z, x[:, 1]], axis=2)], axis=1)

    w_in = jnp.concatenate([pair_diag(jnp.real(win_f)), pair_diag(jnp.imag(win_f)),
                            pair_diag(jnp.real(win_b)), pair_diag(jnp.imag(win_b))], axis=2)
    w_out = jnp.concatenate([pair_diag(jnp.real(wout_f)), pair_diag(-jnp.imag(wout_f)),
                             pair_diag(jnp.real(wout_b)), pair_diag(-jnp.imag(wout_b))], axis=1)
    a_tc = apow[..., tc].reshape(2, g_n // 2, 2 * p_n)
    a_rows = jnp.stack([jnp.real(a_tc[0]), jnp.imag(a_tc[0]), jnp.real(a_tc[1]), jnp.imag(a_tc[1])], axis=1)
    a_rows = jnp.concatenate([a_rows, jnp.zeros((g_n // 2, SUBLANES - 4, 2 * p_n), F32)], axis=1)
    return w_in.astype(BF16), w_out.astype(BF16), pair_diag(loc).astype(BF16), a_rows


def _s5_kernel(u_ref, win_ref, wout_ref, loc_ref, a_ref, y_ref, s_scr, h_scr, *, n_lat, n_ctx, batch):
    u = u_ref[...]
    s_scr[...] = _dot(u, win_ref[...])
    w = a_ref.shape[1]
    a = a_ref[...]
    bc = lambda r: jnp.broadcast_to(a[r:r + 1], (SUBLANES, w))
    lo = lax.broadcasted_iota(jnp.int32, (SUBLANES, w), 0) < batch

    def cmul_add(ar, ai, hr, hi, sr, si):
        return ar * hr - ai * hi + sr, ar * hi + ai * hr + si

    def tile(i, carry, col, ar, ai, first_lo):
        hr, hi = carry
        r0 = pl.multiple_of(i * SUBLANES, SUBLANES)
        sr = s_scr[pl.ds(r0, SUBLANES), col:col + w]
        si = s_scr[pl.ds(r0, SUBLANES), col + w:col + 2 * w]
        first = lo if first_lo else ~lo
        c1r, c1i = cmul_add(ar, ai, hr, hi, sr, si)
        h1r = jnp.where(first, c1r, pltpu.roll(c1r, batch, 0))
        h1i = jnp.where(first, c1i, pltpu.roll(c1i, batch, 0))
        c2r, c2i = cmul_add(ar, ai, h1r, h1i, sr, si)
        h2r = jnp.where(first, pltpu.roll(c2r, batch, 0), c2r)
        h2i = jnp.where(first, pltpu.roll(c2i, batch, 0), c2i)
        h_scr[pl.ds(r0, SUBLANES), col:col + w] = jnp.where(first, hr, h1r)
        h_scr[pl.ds(r0, SUBLANES), col + w:col + 2 * w] = jnp.where(first, hi, h1i)
        return h2r, h2i

    zero = (jnp.zeros((SUBLANES, w), F32), jnp.zeros((SUBLANES, w), F32))
    arf, aif, arb, aib = bc(0), bc(1), bc(2), bc(3)
    fwd = lambda i, c: tile(i, c, 0, arf, aif, True)
    bwd = lambda i, c: tile(i, c, 2 * w, arb, aib, False)
    c = lax.fori_loop(0, n_ctx, lambda i, c: fwd(n_lat + i, c), zero)
    lax.fori_loop(0, n_lat, fwd, c)
    c = lax.fori_loop(0, n_ctx, lambda i, c: bwd(n_lat + n_ctx - 1 - i, c), zero)
    lax.fori_loop(0, n_lat, lambda i, c: bwd(n_lat - 1 - i, c), c)
    y_ref[...] = _dot(u, loc_ref[...]) + _dot(h_scr[...].astype(BF16), wout_ref[...])


def _s5(s_tok, ops, *, b, n, nc):
    assert SUBLANES % b == 0 and SUBLANES // b == 2, "chunk recurrence is laid out for a batch of four"
    w_in, w_out, loc, a_rows = ops
    tc = S5_CHUNK
    ch = s_tok.shape[1]
    gp = ch // (2 * S5_GROUP)
    wid = 2 * tc * S5_GROUP

    def to_chunks(x, length):
        x = x.reshape(b, length // tc, tc, gp, 2, S5_GROUP)
        return jnp.transpose(x, (3, 1, 0, 4, 2, 5)).reshape(gp, (length // tc) * b, wid)

    def from_chunks(y, length):
        y = y.reshape(gp, length // tc, b, 2, tc, S5_GROUP)
        return jnp.transpose(y, (2, 1, 4, 0, 3, 5)).reshape(b * length, ch)

    u = jnp.concatenate([to_chunks(s_tok[:b * n], n), to_chunks(s_tok[b * n:], nc)], axis=1)
    r = u.shape[1]
    n_lat, n_ctx = (n // tc) * b // SUBLANES, (nc // tc) * b // SUBLANES
    sw = w_in.shape[2]
    y = pl.pallas_call(
        functools.partial(_s5_kernel, n_lat=n_lat, n_ctx=n_ctx, batch=b),
        grid=(gp,),
        in_specs=[
            pl.BlockSpec((None, r, wid), lambda g: (g, 0, 0)),
            pl.BlockSpec((None, wid, sw), lambda g: (g, 0, 0)),
            pl.BlockSpec((None, sw, wid), lambda g: (g, 0, 0)),
            pl.BlockSpec((None, wid, wid), lambda g: (g, 0, 0)),
            pl.BlockSpec((None, SUBLANES, sw // 4), lambda g: (g, 0, 0)),
        ],
        out_specs=pl.BlockSpec((None, r, wid), lambda g: (g, 0, 0)),
        out_shape=jax.ShapeDtypeStruct((gp, r, wid), F32),
        scratch_shapes=[pltpu.VMEM((r, sw), F32), pltpu.VMEM((r, sw), F32)],
        compiler_params=_cparams(("parallel",)),
        name="s5",
    )(u, w_in, w_out, loc, a_rows)
    r_lat = (n // tc) * b
    return jnp.concatenate([from_chunks(y[:, :r_lat], n), from_chunks(y[:, r_lat:], nc)], axis=0)


def _even_out_kernel(x_ref, m_ref, o_ref, ys_ref, wglu_ref, bglu_ref, woo_ref, wos_ref, out_ref):
    y = _gelu_tanh(ys_ref[...])
    y = y * _sigmoid(_dot(y.astype(BF16), wglu_ref[...]) + bglu_ref[...])
    oc = jnp.concatenate([o_ref[h] for h in range(MLA_HEADS)], axis=-1)
    mix = _dot(oc, woo_ref[...]) + _dot(y.astype(BF16), wos_ref[...])
    out_ref[...] = x_ref[...] + m_ref[...][5:6] * mix


def _even_out(tok, m, o, ys, w_glu, b_glu, w_out, *, tm, mod_row):
    t, d = tok.shape
    sc = ys.shape[1]
    wo = w_out[:MLA_HEADS * V_DIM].reshape(MLA_HEADS, V_DIM, d)
    woo = jnp.concatenate([wo, jnp.zeros((MLA_HEADS, HEAD_LANES - V_DIM, d), F32)], axis=1)
    woo = woo.reshape(MLA_HEADS * HEAD_LANES, d).astype(BF16)
    wos = w_out[MLA_HEADS * V_DIM:].astype(BF16)
    wg = w_glu.astype(BF16)
    bg = b_glu.reshape(1, sc).astype(F32)
    full = lambda a: pl.BlockSpec(a.shape, lambda i: (0,) * a.ndim)
    return pl.pallas_call(
        _even_out_kernel,
        grid=(t // tm,),
        in_specs=[
            pl.BlockSpec((tm, d), lambda i: (i, 0)),
            pl.BlockSpec((None, N_MOD, d), lambda i: (mod_row(i), 0, 0)),
            pl.BlockSpec((MLA_HEADS, tm, HEAD_LANES), lambda i: (0, i, 0)),
            pl.BlockSpec((tm, sc), lambda i: (i, 0)),
            full(wg), full(bg), full(woo), full(wos),
        ],
        out_specs=pl.BlockSpec((tm, d), lambda i: (i, 0)),
        out_shape=jax.ShapeDtypeStruct((t, d), F32),
        compiler_params=_cparams(("parallel",)),
        name="even_out",
    )(tok, m, o, ys, wg, bg, woo, wos)


def _odd_in_kernel(x_ref, m_ref, g_ref, w_ref, hy_out, lx_out, lg_out):
    m = m_ref[...]
    hb = (_rms(x_ref[...], g_ref[...]) * (1.0 + m[4:5]) + m[3:4]).astype(BF16)
    p = _dot(hb, w_ref[...])
    n_hy, n_lx = hy_out.shape[1], lx_out.shape[1]
    hy_out[...] = p[:, :n_hy]
    lx_out[...] = p[:, n_hy:n_hy + n_lx]
    lg_out[...] = p[:, n_hy + n_lx:]


def _odd_in(tok, m, g, w_in, *, hy_w, lru_w, tm, mod_row):
    t, d = tok.shape
    w = w_in.astype(BF16)
    gg = g.reshape(1, d)
    full = lambda a: pl.BlockSpec(a.shape, lambda i: (0,) * a.ndim)
    row = lambda c: pl.BlockSpec((tm, c), lambda i: (i, 0))
    return pl.pallas_call(
        _odd_in_kernel,
        grid=(t // tm,),
        in_specs=[row(d), pl.BlockSpec((None, N_MOD, d), lambda i: (mod_row(i), 0, 0)), full(gg), full(w)],
        out_specs=[row(hy_w), row(lru_w), row(lru_w)],
        out_shape=[jax.ShapeDtypeStruct((t, c), F32) for c in (hy_w, lru_w, lru_w)],
        compiler_params=_cparams(("parallel",)),
        name="odd_in",
    )(tok, m, gg, w)


def _shift_down(cur, prev8, k):
    r = pltpu.roll(cur, k, 0)
    row = lax.broadcasted_iota(jnp.int32, prev8.shape, 0)
    head = jnp.where(row < k, pltpu.roll(prev8, k, 0), r[:SUBLANES])
    return jnp.concatenate([head, r[SUBLANES:]], axis=0)


def _shift_up(cur, next8, k):
    rows = cur.shape[0]
    r = pltpu.roll(cur, rows - k, 0)
    row = lax.broadcasted_iota(jnp.int32, next8.shape, 0)
    tail = jnp.where(row >= SUBLANES - k, pltpu.roll(next8, SUBLANES - k, 0), r[rows - SUBLANES:])
    return jnp.concatenate([r[:rows - SUBLANES], tail], axis=0)


def _halo_specs(tt, width, n_tiles, tile_of):
    per = tt // SUBLANES
    last = n_tiles * per - 1
    cur = pl.BlockSpec((None, tt, width), lambda b, i: (b, tile_of(i), 0))
    prev = pl.BlockSpec((None, SUBLANES, width), lambda b, i: (b, jnp.maximum(tile_of(i) * per - 1, 0), 0))
    nxt = pl.BlockSpec((None, SUBLANES, width), lambda b, i: (b, jnp.minimum((tile_of(i) + 1) * per, last), 0))
    return cur, prev, nxt


def _hyena_prep_kernel(cur_ref, prev_ref, next_ref, w_ref, b_ref, z_out, x0_out):
    i = pl.program_id(1)
    cur = cur_ref[...]
    prev8 = prev_ref[...] * (i > 0).astype(F32)
    next8 = next_ref[...] * (i < pl.num_programs(1) - 1).astype(F32)
    w = w_ref[...]
    u = w[0:1] * _shift_down(cur, prev8, 1) + w[1:2] * cur + w[2:3] * _shift_up(cur, next8, 1) + b_ref[...]
    c = z_out.shape[1]
    x0_out[...] = u[:, :c]
    z_out[...] = u[:, 2 * c:] * u[:, c:2 * c]


def _hyena_prep(hy, conv_w, conv_b, *, tt):
    b, n, c3 = hy.shape
    c = c3 // 3
    cur, prev, nxt = _halo_specs(tt, c3, n // tt, lambda i: i)
    w8 = jnp.concatenate([conv_w.astype(F32), jnp.zeros((SUBLANES - conv_w.shape[0], c3), F32)], axis=0)
    out_spec = pl.BlockSpec((None, tt, c), lambda bi, i: (bi, i, 0))
    return pl.pallas_call(
        _hyena_prep_kernel,
        grid=(b, n // tt),
        in_specs=[cur, prev, nxt, pl.BlockSpec((SUBLANES, c3), lambda bi, i: (0, 0)),
                  pl.BlockSpec((1, c3), lambda bi, i: (0, 0))],
        out_specs=[out_spec, out_spec],
        out_shape=[jax.ShapeDtypeStruct((b, n, c), F32)] * 2,
        compiler_params=_cparams(("parallel", "parallel")),
        name="hyena_prep",
    )(hy, hy, hy, w8, conv_b.reshape(1, c3).astype(F32))


class _Dft:
    def __init__(self, n):
        self.n = n
        self.N = 2 * n
        self.N2 = FFT_N2
        self.N1 = self.N // self.N2
        self.A = n // self.N2
        self.K1 = self.N1 // 2 + 1
        self.K1p = -(-self.K1 // SUBLANES) * SUBLANES
        N, N1, N2, A, K1, K1p = self.N, self.N1, self.N2, self.A, self.K1, self.K1p
        b = np.arange(N2)[:, None, None]
        k1 = np.arange(K1p)[None, :, None]
        a = np.arange(A)[None, None, :]
        phi = 2.0 * np.pi * ((a * k1 % N1) / N1 + (b * k1 % N) / N)
        live = (k1 < K1).astype(np.float64)
        f1 = np.concatenate([np.cos(phi) * live, -np.sin(phi) * live], axis=1)
        wgt = np.where((k1 == 0) | (k1 == N1 // 2), 1.0, 2.0) * live / N
        g1 = np.concatenate([np.cos(phi) * wgt, -np.sin(phi) * wgt], axis=1)
        g1 = np.transpose(g1, (0, 2, 1))
        ang = 2.0 * np.pi * (np.arange(N2)[:, None] * np.arange(N2)[None, :] % N2) / N2
        c, s = np.cos(ang), np.sin(ang)
        f2 = np.block([[c, s], [-s, c]])
        g2 = np.block([[c, -s], [s, c]])
        self.f1, self.f2, self.g2, self.g1 = (self._split(x) for x in (f1, f2, g2, g1))

    @staticmethod
    def _split(x):
        x32 = jnp.asarray(x, F32)
        hi = x32.astype(BF16)
        lo = (x32 - hi.astype(F32)).astype(BF16)
        return jnp.stack([hi, lo])


def _dft_forward(src_ref, f1_ref, f2_ref, y_scr, put, dft):
    n2, a_n, k1p2 = dft.N2, dft.A, 2 * dft.K1p

    def stage1(b, _):
        xb = src_ref[pl.ds(b, a_n, stride=n2), :]
        x_hi, x_lo = _split(xb)
        r0 = pl.multiple_of(b * k1p2, SUBLANES)
        y_scr[pl.ds(r0, k1p2), :] = _dot3(f1_ref[0, b], f1_ref[1, b], x_hi, x_lo)
        return 0

    lax.fori_loop(0, n2, stage1, 0)
    f2_hi, f2_lo = f2_ref[0], f2_ref[1]

    def stage2(k1, _):
        yr = y_scr[pl.ds(k1, n2, stride=k1p2), :]
        yi = y_scr[pl.ds(dft.K1p + k1, n2, stride=k1p2), :]
        y_hi, y_lo = _split(jnp.concatenate([yr, yi], axis=0))
        put(k1, _dot3(f2_hi, f2_lo, y_hi, y_lo))
        return 0

    lax.fori_loop(0, dft.K1, stage2, 0)


def _hyena_filter_kernel(feat_ref, w1_ref, b1_ref, w2_ref, b2_ref, w3f_ref, w3b_ref, freq_ref, delta_ref,
                         f1_ref, f2_ref, h_out, filt_scr, y_scr, *, dft):
    n = dft.n
    feat = feat_ref[...]
    t = feat[:, 0:1]
    freq = freq_ref[...]
    f_hi, f_lo = _split(feat)
    w_hi, w_lo = _split(w1_ref[...])
    h = jnp.sin(freq[0:1] * (_dot3(f_hi, f_lo, w_hi, w_lo) + b1_ref[...]))
    h_hi, h_lo = _split(h)
    w_hi, w_lo = _split(w2_ref[...])
    h = jnp.sin(freq[1:2] * (_dot3(h_hi, h_lo, w_hi, w_lo) + b2_ref[...]))
    h_hi, h_lo = _split(h)
    decay = jnp.exp(-t * delta_ref[...])
    n2 = dft.N2

    w_hi, w_lo = _split(w3f_ref[...])
    filt_scr[...] = _dot3(h_hi, h_lo, w_hi, w_lo) * decay

    def put_fwd(k1, x):
        h_out[k1] = x

    _dft_forward(filt_scr, f1_ref, f2_ref, y_scr, put_fwd, dft)

    w_hi, w_lo = _split(w3b_ref[...])
    row = lax.broadcasted_iota(jnp.int32, (n, LANES), 0)
    filt_scr[...] = jnp.where(row == 0, 0.0, _dot3(h_hi, h_lo, w_hi, w_lo) * decay)
    sign = jnp.where(lax.broadcasted_iota(jnp.int32, (2 * n2, LANES), 0) < n2, 1.0, -1.0)

    def put_bwd(k1, x):
        h_out[k1] = h_out[k1] + sign * x

    _dft_forward(filt_scr, f1_ref, f2_ref, y_scr, put_bwd, dft)


def _hyena_features(n):
    t = np.linspace(0.0, 1.0, n, dtype=np.float32).astype(np.float64)[:, None]
    w = (2.0 * np.pi * np.arange(n, dtype=np.float64)[:, None] / n).astype(np.float32).astype(np.float64)
    bands = np.linspace(1e-4, HY_BANDS - 1, HY_BANDS, dtype=np.float32).astype(np.float64)[None, :]
    bw = (bands * w).astype(np.float32).astype(np.float64)
    z = np.concatenate([t, np.cos(bw), -np.sin(bw)], axis=-1)
    z = np.concatenate([z, np.zeros((n, LANES - z.shape[1]))], axis=-1)
    return jnp.asarray(z, F32)


def _hyena_filter_spectrum(n, w1, b1, w2, b2, w3, sin_freq, dft):
    ch = w3.shape[1] // 2
    hid = w1.shape[1]
    feat = _hyena_features(n)
    w1p = jnp.concatenate([w1.astype(F32), jnp.zeros((LANES - w1.shape[0], hid), F32)], axis=0)
    deltas = jnp.asarray(np.abs(np.linspace(HY_MIN_DECAY, HY_MAX_DECAY, ch, dtype=np.float32)), F32).reshape(1, ch)
    freq8 = jnp.concatenate([sin_freq.astype(F32), jnp.zeros((SUBLANES - 2, hid), F32)], axis=0)
    full = lambda a: pl.BlockSpec(a.shape, lambda c: (0,) * a.ndim)
    tiles = ch // LANES
    args = (feat, w1p, b1.reshape(1, hid).astype(F32), w2.astype(F32), b2.reshape(1, hid).astype(F32))
    return pl.pallas_call(
        functools.partial(_hyena_filter_kernel, dft=dft),
        grid=(tiles,),
        in_specs=[full(a) for a in args] + [
            pl.BlockSpec((hid, LANES), lambda c: (0, c)),
            pl.BlockSpec((hid, LANES), lambda c: (0, tiles + c)),
            full(freq8),
            pl.BlockSpec((1, LANES), lambda c: (0, c)),
            full(dft.f1), full(dft.f2),
        ],
        out_specs=pl.BlockSpec((dft.K1, 2 * dft.N2, LANES), lambda c: (0, 0, c)),
        out_shape=jax.ShapeDtypeStruct((dft.K1, 2 * dft.N2, ch), F32),
        scratch_shapes=[pltpu.VMEM((n, LANES), F32), pltpu.VMEM((dft.N2 * 2 * dft.K1p, LANES), F32)],
        compiler_params=_cparams(("parallel",)),
        name="hyena_filter",
    )(*args, w3.astype(F32), w3.astype(F32), freq8, deltas, dft.f1, dft.f2)


def _hyena_conv_kernel(z_ref, h_ref, f1_ref, f2_ref, g2_ref, g1_ref, y_out, y_scr, x_scr, *, dft):
    n2, a_n, k1p = dft.N2, dft.A, dft.K1p

    def put(k1, x):
        hk = h_ref[k1]
        xr, xi, hr, hi = x[:n2], x[n2:], hk[:n2], hk[n2:]
        p_hi, p_lo = _split(jnp.concatenate([xr * hr - xi * hi, xr * hi + xi * hr], axis=0))
        r0 = pl.multiple_of(k1 * 2 * n2, SUBLANES)
        x_scr[pl.ds(r0, 2 * n2), :] = _dot3(g2_ref[0], g2_ref[1], p_hi, p_lo)

    if k1p > dft.K1:
        x_scr[dft.K1 * 2 * n2:, :] = jnp.zeros(((k1p - dft.K1) * 2 * n2, LANES), F32)
    _dft_forward(z_ref, f1_ref, f2_ref, y_scr, put, dft)

    def last(b, _):
        yr = x_scr[pl.ds(b, k1p, stride=2 * n2), :]
        yi = x_scr[pl.ds(n2 + b, k1p, stride=2 * n2), :]
        v_hi, v_lo = _split(jnp.concatenate([yr, yi], axis=0))
        y_out[pl.ds(b, a_n, stride=n2), :] = _dot3(g1_ref[0, b], g1_ref[1, b], v_hi, v_lo)
        return 0

    lax.fori_loop(0, n2, last, 0)


def _hyena_conv(z, spec, dft):
    b, n, ch = z.shape
    once = pl.Buffered(1)
    full = lambda a: pl.BlockSpec(a.shape, lambda c, bi: (0,) * a.ndim, pipeline_mode=once)
    io_spec = pl.BlockSpec((None, n, LANES), lambda c, bi: (bi, 0, c), pipeline_mode=once)
    return pl.pallas_call(
        functools.partial(_hyena_conv_kernel, dft=dft),
        grid=(ch // LANES, b),
        in_specs=[io_spec, pl.BlockSpec((dft.K1, 2 * dft.N2, LANES), lambda c, bi: (0, 0, c), pipeline_mode=once),
                  full(dft.f1), full(dft.f2), full(dft.g2), full(dft.g1)],
        out_specs=io_spec,
        out_shape=jax.ShapeDtypeStruct((b, n, ch), F32),
        scratch_shapes=[pltpu.VMEM((dft.N2 * 2 * dft.K1p, LANES), F32), pltpu.VMEM((dft.K1p * 2 * dft.N2, LANES), F32)],
        compiler_params=_cparams(("parallel", "arbitrary")),
        name="hyena_conv",
    )(z, spec, dft.f1, dft.f2, dft.g2, dft.g1)


def _lru_tile_terms(cur, prev8, next8, conv_w, conv_b, wg, bg, sp, reverse):
    rows, c = cur.shape
    xc = (conv_w[0:1] * _shift_down(cur, prev8, 2) + conv_w[1:2] * _shift_down(cur, prev8, 1)
          + conv_w[2:3] * cur + conv_w[3:4] * _shift_up(cur, next8, 1) + conv_b)
    g = _dot(xc.astype(BF16), wg) + bg
    r, ig = _sigmoid(g[:, :c]), _sigmoid(g[:, c:])
    log_a = -LRU_C * r * sp
    a = jnp.exp(log_a)
    th = jnp.tanh(log_a)
    bb = jnp.sqrt(-2.0 * th / (1.0 - th)) * (ig * xc)
    sub = lax.broadcasted_iota(jnp.int32, (rows, c), 0) & (SUBLANES - 1)
    for k in (1, 2, 4):
        if reverse:
            ok = sub < SUBLANES - k
            a_s, b_s = pltpu.roll(a, rows - k, 0), pltpu.roll(bb, rows - k, 0)
        else:
            ok = sub >= k
            a_s, b_s = pltpu.roll(a, k, 0), pltpu.roll(bb, k, 0)
        bb = jnp.where(ok, bb + a * b_s, bb)
        a = jnp.where(ok, a * a_s, a)
    return a, bb


def _lru_kernel(ctx_ref, curf_ref, prevf_ref, nextf_ref, curb_ref, prevb_ref, nextb_ref,
                cw_ref, cb_ref, wgf_ref, wgb_ref, bgf_ref, bgb_ref, sp_ref,
                hf_out, hb_out, a_scr, b_scr, carry_scr, *, batch, tt):
    i = pl.program_id(0)
    nt = pl.num_programs(0)
    cw, cb = cw_ref[...], cb_ref[...]
    sp = sp_ref[...]
    c = cw.shape[1]
    params = ((wgf_ref, bgf_ref, sp[0:1]), (wgb_ref, bgb_ref, sp[1:2]))

    def carry_scan(rows, write):
        groups = rows // SUBLANES

        def body(j, carries):
            new = []
            for d in range(2):
                jj = j if d == 0 else groups - 1 - j
                r0 = pl.multiple_of(jj * SUBLANES, SUBLANES)
                for bi in range(batch):
                    h8 = b_scr[d, bi, pl.ds(r0, SUBLANES), :] + a_scr[d, bi, pl.ds(r0, SUBLANES), :] * carries[d * batch + bi]
                    write(d, bi, r0, h8)
                    edge = h8[SUBLANES - 1:SUBLANES] if d == 0 else h8[0:1]
                    new.append(jnp.broadcast_to(edge, (SUBLANES, c)))
            return tuple(new)

        init = tuple(carry_scr[d, bi] for d in range(2) for bi in range(batch))
        out = lax.fori_loop(0, groups, body, init)
        for d in range(2):
            for bi in range(batch):
                carry_scr[d, bi] = out[d * batch + bi]

    @pl.when(i == 0)
    def _():
        carry_scr[...] = jnp.zeros(carry_scr.shape, F32)
        nc = ctx_ref.shape[1]
        zero8 = jnp.zeros((SUBLANES, c), F32)
        for d in range(2):
            wg, bg, spd = params[d]
            for bi in range(batch):
                a, bb = _lru_tile_terms(ctx_ref[bi], zero8, zero8, cw, cb, wg[...], bg[...], spd, d == 1)
                a_scr[d, bi, :nc, :] = a
                b_scr[d, bi, :nc, :] = bb
        carry_scan(nc, lambda d, bi, r0, h8: None)

    tiles = ((curf_ref, prevf_ref, nextf_ref, i), (curb_ref, prevb_ref, nextb_ref, nt - 1 - i))
    for d in range(2):
        cur_ref, prev_ref, next_ref, ti = tiles[d]
        wg, bg, spd = params[d]
        has_prev = (ti > 0).astype(F32)
        has_next = (ti < nt - 1).astype(F32)
        for bi in range(batch):
            a, bb = _lru_tile_terms(cur_ref[bi], prev_ref[bi] * has_prev, next_ref[bi] * has_next,
                                    cw, cb, wg[...], bg[...], spd, d == 1)
            a_scr[d, bi, :tt, :] = a
            b_scr[d, bi, :tt, :] = bb

    def write(d, bi, r0, h8):
        if d == 0:
            hf_out[bi, pl.ds(r0, SUBLANES), :] = h8
        else:
            hb_out[bi, pl.ds(r0, SUBLANES), :] = h8

    carry_scan(tt, write)


def _lru_gate_weights(w_a, w_x):
    def dense(w):
        nb, bs = w.shape[0], w.shape[1]
        eye = jnp.eye(nb, dtype=F32)
        return jnp.einsum('hij,hk->hikj', w.astype(F32), eye).reshape(nb * bs, nb * bs)
    return [jnp.concatenate([dense(w_a[d]), dense(w_x[d])], axis=1).astype(BF16) for d in range(2)]


def _lru(lx_lat, lx_ctx, conv_w, conv_b, w_a, b_a, w_x, b_x, lam, *, tt):
    b, n, c = lx_lat.shape
    nc = lx_ctx.shape[1]
    nt = n // tt
    wgf, wgb = _lru_gate_weights(w_a, w_x)
    bgf = jnp.concatenate([b_a[0], b_x[0]]).reshape(1, 2 * c).astype(F32)
    bgb = jnp.concatenate([b_a[1], b_x[1]]).reshape(1, 2 * c).astype(F32)
    sp = jax.nn.softplus(-lam.astype(F32))
    sp8 = jnp.concatenate([sp, jnp.zeros((SUBLANES - 2, c), F32)], axis=0)
    cw8 = jnp.concatenate([conv_w.astype(F32), jnp.zeros((SUBLANES - conv_w.shape[0], c), F32)], axis=0)
    cb = conv_b.reshape(1, c).astype(F32)
    per = tt // SUBLANES
    last = n // SUBLANES - 1

    def specs(tile_of):
        cur = pl.BlockSpec((b, tt, c), lambda i: (0, tile_of(i), 0))
        prev = pl.BlockSpec((b, SUBLANES, c), lambda i: (0, jnp.maximum(tile_of(i) * per - 1, 0), 0))
        nxt = pl.BlockSpec((b, SUBLANES, c), lambda i: (0, jnp.minimum((tile_of(i) + 1) * per, last), 0))
        return [cur, prev, nxt]

    fwd_tile = lambda i: i
    bwd_tile = lambda i: nt - 1 - i
    full = lambda a: pl.BlockSpec(a.shape, lambda i: (0,) * a.ndim)
    rows_scr = max(tt, nc)
    return pl.pallas_call(
        functools.partial(_lru_kernel, batch=b, tt=tt),
        grid=(nt,),
        in_specs=[full(lx_ctx)] + specs(fwd_tile) + specs(bwd_tile)
        + [full(cw8), full(cb), full(wgf), full(wgb), full(bgf), full(bgb), full(sp8)],
        out_specs=[pl.BlockSpec((b, tt, c), lambda i: (0, i, 0)), pl.BlockSpec((b, tt, c), lambda i: (0, nt - 1 - i, 0))],
        out_shape=[jax.ShapeDtypeStruct((b, n, c), F32)] * 2,
        scratch_shapes=[pltpu.VMEM((2, b, rows_scr, c), F32), pltpu.VMEM((2, b, rows_scr, c), F32),
                        pltpu.VMEM((2, b, SUBLANES, c), F32)],
        compiler_params=_cparams(("arbitrary",)),
        name="rglru",
    )(lx_ctx, lx_lat, lx_lat, lx_lat, lx_lat, lx_lat, lx_lat, cw8, cb, wgf, wgb, bgf, bgb, sp8)


def _odd_out_kernel(x_ref, m_ref, yc_ref, z_ref, x0_ref, hb_ref, hf_ref, hbk_ref, lg_ref, wh_ref, wr_ref, out_ref):
    z = z_ref[...]
    hy = x0_ref[...] * (yc_ref[...] + z * hb_ref[...])
    r = (hf_ref[...] + hbk_ref[...]) * _gelu_tanh(lg_ref[...])
    mix = _dot(hy.astype(BF16), wh_ref[...]) + _dot(r.astype(BF16), wr_ref[...])
    out_ref[...] = x_ref[...] + m_ref[...][5:6] * mix


def _odd_out(tok, m, yc, z, x0, hy_bias, hf, hb, lg, w_out, *, n_rows, tm, mod_row):
    d = tok.shape[1]
    c = yc.shape[1]
    wh, wr = w_out[:c].astype(BF16), w_out[c:].astype(BF16)
    hbias = hy_bias.reshape(1, c).astype(F32)
    full = lambda a: pl.BlockSpec(a.shape, lambda i: (0,) * a.ndim)
    row = lambda w: pl.BlockSpec((tm, w), lambda i: (i, 0))
    return pl.pallas_call(
        _odd_out_kernel,
        grid=(n_rows // tm,),
        in_specs=[row(d), pl.BlockSpec((None, N_MOD, d), lambda i: (mod_row(i), 0, 0)),
                  row(c), row(c), row(c), full(hbias), row(c), row(c), row(c), full(wh), full(wr)],
        out_specs=row(d),
        out_shape=jax.ShapeDtypeStruct((n_rows, d), F32),
        compiler_params=_cparams(("parallel",)),
        name="odd_out",
    )(tok, m, yc, z, x0, hbias, hf, hb, lg, wh, wr)


def kernel(x, c, ctx, c_ctx, mod_w, mod_b, norm_ffn1, norm_mix, norm_ffn2, ffn1_w_gate, ffn1_w_up, ffn1_w_down, ffn2_w_gate, ffn2_w_up, ffn2_w_down, ev_w_in, mla_q_norm, mla_w_uq, mla_kv_norm, mla_w_ukv, s5_lambda_re, s5_lambda_im, s5_log_step, s5_b_re, s5_b_im, s5_c_re, s5_c_im, s5_d, s5_w_glu, s5_b_glu, ev_w_out, od_w_in, hy_conv_w, hy_conv_b, hy_filt_w1, hy_filt_b1, hy_filt_w2, hy_filt_b2, hy_filt_w3, hy_sin_freq, hy_bias, lru_conv_w, lru_conv_b, lru_w_a, lru_b_a, lru_w_x, lru_b_x, lru_lambda, od_w_out, final_norm):
    b, n, d = x.shape
    nc = ctx.shape[1]
    depth = mod_w.shape[0]
    assert depth == 2, "the trunk is laid out for one even and one odd layer"
    n_lat_rows = b * n
    t = n_lat_rows + b * nc
    tm = _row_tile(n, b * nc)
    tpb = n // tm
    mod_row = lambda i: jnp.minimum(i // tpb, b)

    tok = jnp.concatenate([x.reshape(n_lat_rows, d), ctx.reshape(b * nc, d)], axis=0)
    cond8 = jnp.concatenate([c, c_ctx[None, :], jnp.zeros((SUBLANES - b - 1, d), F32)], axis=0)
    m = _adaln(cond8, mod_w, mod_b)
    bf = lambda w: w.astype(BF16)

    tok = _ffn(tok, m[0], 0, norm_ffn1[0], bf(ffn1_w_gate[0]), bf(ffn1_w_up[0]), bf(ffn1_w_down[0]),
               n_rows=t, tm=tm, mod_row=mod_row)
    ev_w = _even_in_weights(ev_w_in[0], mla_w_uq[0], mla_w_ukv[0])
    q, k, v, s_tok = _even_in(tok, m[0], norm_mix[0], ev_w, mla_q_norm[0], mla_kv_norm[0],
                              n=n, tm=tm, mod_row=mod_row, n_lat_rows=n_lat_rows)
    o = _attention(q, k, v, b=b, n=n, nc=nc)
    s5_ops = _s5_operators(s5_lambda_re[0], s5_lambda_im[0], s5_log_step[0], s5_b_re[0], s5_b_im[0],
                           s5_c_re[0], s5_c_im[0], s5_d[0])
    ys = _s5(s_tok, s5_ops, b=b, n=n, nc=nc)
    tok = _even_out(tok, m[0], o, ys, s5_w_glu[0], s5_b_glu[0], ev_w_out[0], tm=tm, mod_row=mod_row)
    tok = _ffn(tok, m[0], 2, norm_ffn2[0], bf(ffn2_w_gate[0]), bf(ffn2_w_up[0]), bf(ffn2_w_down[0]),
               n_rows=t, tm=tm, mod_row=mod_row)

    tok = _ffn(tok, m[1], 0, norm_ffn1[1], bf(ffn1_w_gate[1]), bf(ffn1_w_up[1]), bf(ffn1_w_down[1]),
               n_rows=t, tm=tm, mod_row=mod_row)
    hy_w = hy_conv_w.shape[2]
    lru_w = lru_conv_w.shape[2]
    hy, lx, lg = _odd_in(tok, m[1], norm_mix[1], od_w_in[0], hy_w=hy_w, lru_w=lru_w, tm=tm, mod_row=mod_row)
    tt = _row_tile(n)
    z, x0 = _hyena_prep(hy[:n_lat_rows].reshape(b, n, hy_w), hy_conv_w[0], hy_conv_b[0], tt=tt)
    dft = _Dft(n)
    spec = _hyena_filter_spectrum(n, hy_filt_w1[0], hy_filt_b1[0], hy_filt_w2[0], hy_filt_b2[0], hy_filt_w3[0],
                                  hy_sin_freq[0], dft)
    yc = _hyena_conv(z, spec, dft)
    hf, hb = _lru(lx[:n_lat_rows].reshape(b, n, lru_w), lx[n_lat_rows:].reshape(b, nc, lru_w), lru_conv_w[0],
                  lru_conv_b[0], lru_w_a[0], lru_b_a[0], lru_w_x[0], lru_b_x[0], lru_lambda[0], tt=min(tt, 256))
    flat = lambda a: a.reshape(n_lat_rows, a.shape[2])
    lat = _odd_out(tok, m[1], flat(yc), flat(z), flat(x0), hy_bias[0], flat(hf), flat(hb), lg, od_w_out[0],
                   n_rows=n_lat_rows, tm=tm, mod_row=mod_row)
    lat = _ffn(lat, m[1], 2, norm_ffn2[1], bf(ffn2_w_gate[1]), bf(ffn2_w_up[1]), bf(ffn2_w_down[1]),
               n_rows=n_lat_rows, tm=tm, mod_row=mod_row, final_g=final_norm)
    return lat.reshape(b, n, d)
```

```python
import functools
import math

import numpy as np
import jax
import jax.numpy as jnp
from jax import lax
from jax.experimental import pallas as pl
from jax.experimental.pallas import tpu as pltpu

F32 = jnp.float32
BF16 = jnp.bfloat16

GRID_W = 64
N_MOD = 9
NORM_EPS = 1e-6
ROPE_BASE = 10000.0
MLA_HEADS = 8
QK_NOPE = 64
QK_ROPE = 32
V_DIM = 64
Q_RANK = 384
KV_RANK = 256
S5_GROUP = 16
S5_STATE = 64
HY_BANDS = 16
HY_DECAY_TARGET = 1e-2
HY_MIN_DECAY = math.log(HY_DECAY_TARGET) / 1.5
HY_MAX_DECAY = math.log(HY_DECAY_TARGET) / 0.3
LRU_BLOCKS = 8
LRU_C = 8.0

LANES = 128
SUBLANES = 8
VMEM_BYTES_V7X = 64 * 1024 * 1024
VMEM_LIMIT = VMEM_BYTES_V7X - 8 * 1024 * 1024

HEAD_LANES = LANES
ONES_LANE = V_DIM
ATT_V_ROWS = V_DIM + 16
ATT_Q_TILE = 2048
ATT_Q_LANES = 256
S5_CHUNK = 16
FFT_N2 = 128
NEG_BIG = -1e30


def _cparams(sem):
    return pltpu.CompilerParams(dimension_semantics=sem, vmem_limit_bytes=VMEM_LIMIT)


def _dot(a, b):
    return jnp.dot(a, b, preferred_element_type=F32)


def _split(x):
    hi = x.astype(BF16)
    lo = (x - hi.astype(F32)).astype(BF16)
    return hi, lo


def _dot3(a_hi, a_lo, b_hi, b_lo):
    return _dot(a_hi, b_hi) + _dot(a_lo, b_hi) + _dot(a_hi, b_lo)


def _sigmoid(x):
    return 1.0 / (1.0 + jnp.exp(-x))


def _gelu_tanh(x):
    return 0.5 * x * (1.0 + jnp.tanh(math.sqrt(2.0 / math.pi) * (x + 0.044715 * (x * x * x))))


def _rms(x, g):
    return x * lax.rsqrt(jnp.mean(x * x, axis=-1, keepdims=True) + NORM_EPS) * g


def _row_tile(*lengths):
    for t in (512, 256, 128, 64, 32, 16, 8):
        if all(l % t == 0 for l in lengths):
            return t
    raise ValueError(f"no row tile divides {lengths}")


def _adaln_kernel(c_ref, w_ref, b_ref, o_ref):
    c = c_ref[...]
    s_hi, s_lo = _split(c * _sigmoid(c))
    w_hi, w_lo = _split(w_ref[...])
    o_ref[...] = _dot3(s_hi, s_lo, w_hi, w_lo) + b_ref[...]


def _adaln(cond8, mod_w, mod_b):
    depth, d, nd = mod_w.shape
    tn = d
    out = pl.pallas_call(
        _adaln_kernel,
        grid=(depth, nd // tn),
        in_specs=[
            pl.BlockSpec((SUBLANES, d), lambda l, j: (0, 0)),
            pl.BlockSpec((None, d, tn), lambda l, j: (l, 0, j)),
            pl.BlockSpec((None, 1, tn), lambda l, j: (l, 0, j)),
        ],
        out_specs=pl.BlockSpec((None, SUBLANES, tn), lambda l, j: (l, 0, j)),
        out_shape=jax.ShapeDtypeStruct((depth, SUBLANES, nd), F32),
        compiler_params=_cparams(("arbitrary", "arbitrary")),
        name="adaln",
    )(cond8, mod_w, mod_b.reshape(depth, 1, nd))
    return out.reshape(depth, SUBLANES, N_MOD, d)


def _ffn_kernel(x_ref, m_ref, g_ref, wg_ref, wu_ref, wd_ref, *rest, k, tf, final):
    if final:
        fn_ref, o_ref = rest
    else:
        (o_ref,) = rest
    x = x_ref[...]
    m = m_ref[...]
    shift, scale, gate = m[3 * k:3 * k + 1], m[3 * k + 1:3 * k + 2], m[3 * k + 2:3 * k + 3]
    hb = (_rms(x, g_ref[...]) * (1.0 + scale) + shift).astype(BF16)
    acc = jnp.zeros(x.shape, F32)
    for f in range(wg_ref.shape[1] // tf):
        sl = slice(f * tf, (f + 1) * tf)
        gt = _dot(hb, wg_ref[:, sl])
        up = _dot(hb, wu_ref[:, sl])
        acc = acc + _dot((gt * _sigmoid(gt) * up).astype(BF16), wd_ref[sl, :])
    y = x + (0.5 * gate) * acc
    if final:
        y = _rms(y, fn_ref[...])
    o_ref[...] = y


def _ffn(tok, m, k, g, wg, wu, wd, *, n_rows, tm, mod_row, final_g=None):
    d = tok.shape[1]
    f = wg.shape[1]
    tf = 256 if f % 256 == 0 else LANES
    final = final_g is not None
    full = lambda shape: pl.BlockSpec(shape, lambda i: (0,) * len(shape), pipeline_mode=pl.Buffered(1))
    in_specs = [
        pl.BlockSpec((tm, d), lambda i: (i, 0)),
        pl.BlockSpec((None, N_MOD, d), lambda i: (mod_row(i), 0, 0)),
        full((1, d)), full((d, f)), full((d, f)), full((f, d)),
    ]
    args = [tok, m, g.reshape(1, d), wg, wu, wd]
    if final:
        in_specs.append(full((1, d)))
        args.append(final_g.reshape(1, d))
    return pl.pallas_call(
        functools.partial(_ffn_kernel, k=k, tf=tf, final=final),
        grid=(n_rows // tm,),
        in_specs=in_specs,
        out_specs=pl.BlockSpec((tm, d), lambda i: (i, 0)),
        out_shape=jax.ShapeDtypeStruct((n_rows, d), F32),
        compiler_params=_cparams(("parallel",)),
        name="ffn_final" if final else "ffn",
    )(*args)


def _even_in_kernel(x_ref, m_ref, g_ref, w1_ref, qn_ref, kvn_ref, wq_ref, wqp_ref, wk_ref, wv_ref,
                    c_ref, s_ref, qt_out, k_out, vt_out, u_out, s_scr):
    x = x_ref[...]
    m = m_ref[...]
    hb = (_rms(x, g_ref[...]) * (1.0 + m[4:5]) + m[3:4]).astype(BF16)
    p = _dot(hb, w1_ref[...])
    o = 0
    cq = p[:, o:o + Q_RANK]; o += Q_RANK
    ckv = p[:, o:o + KV_RANK]; o += KV_RANK
    kra = p[:, o:o + HEAD_LANES]; o += HEAD_LANES
    krb = p[:, o:o + HEAD_LANES]; o += HEAD_LANES
    pw = 2 * S5_GROUP
    per_tile = LANES // pw
    chunks = s_scr.shape[1] // S5_CHUNK
    for lt in range(s_scr.shape[0]):
        s_scr[lt] = p[:, o + lt * LANES:o + (lt + 1) * LANES]
        by_tok = [s_scr[lt, pl.ds(t, chunks, stride=S5_CHUNK), :] for t in range(S5_CHUNK)]
        for l in range(per_tile):
            u_out[lt * per_tile + l] = jnp.concatenate([r[:, l * pw:(l + 1) * pw] for r in by_tok], axis=-1).astype(BF16)
    cos, sin = c_ref[...], s_ref[...]
    cqn = _rms(cq, qn_ref[...]).astype(BF16)
    qa = _dot(cqn, wq_ref[...])
    qb = _dot(cqn, wqp_ref[...])
    ckvn = _rms(ckv, kvn_ref[...]).astype(BF16)
    kn = _dot(ckvn, wk_ref[...])
    vv = _dot(ckvn, wv_ref[...])
    kr = kra * cos + krb * sin
    ones = (lax.broadcasted_iota(jnp.int32, (1, HEAD_LANES), 1) == ONES_LANE).astype(F32)
    scale = (QK_NOPE + QK_ROPE) ** -0.5 * math.log2(math.e)
    for h in range(MLA_HEADS):
        sl = slice(h * HEAD_LANES, (h + 1) * HEAD_LANES)
        qt_out[h] = ((qa[:, sl] * cos + qb[:, sl] * sin) * scale).T.astype(BF16)
        k_out[h] = (kn[:, sl] + kr).astype(BF16)
        vt_out[h] = (vv[:, sl] + ones).T.astype(BF16)


def _rope_partner(w):
    half = w.shape[-1] // 2
    return jnp.concatenate([-w[..., half:], w[..., :half]], axis=-1)


def _even_in_weights(w_in, w_uq, w_ukv):
    d = w_in.shape[0]
    o = 0
    w_cq = w_in[:, o:o + Q_RANK]; o += Q_RANK
    w_ckv = w_in[:, o:o + KV_RANK]; o += KV_RANK
    w_kr = w_in[:, o:o + QK_ROPE]; o += QK_ROPE
    w_s = w_in[:, o:]
    pad_l = jnp.zeros((d, QK_NOPE), F32)
    pad_r = jnp.zeros((d, HEAD_LANES - QK_NOPE - QK_ROPE), F32)
    kr_blk = jnp.concatenate([pad_l, w_kr, pad_r], axis=1)
    kr_blk_p = jnp.concatenate([pad_l, _rope_partner(w_kr), pad_r], axis=1)
    w1 = jnp.concatenate([w_cq, w_ckv, kr_blk, kr_blk_p, w_s], axis=1).astype(BF16)

    dk = QK_NOPE + QK_ROPE
    wq = w_uq.reshape(Q_RANK, MLA_HEADS, dk)
    zq = jnp.zeros((Q_RANK, MLA_HEADS, HEAD_LANES - dk), F32)
    wq_pad = jnp.concatenate([wq, zq], axis=-1)
    wq_par = jnp.concatenate([jnp.zeros((Q_RANK, MLA_HEADS, QK_NOPE), F32), _rope_partner(wq[..., QK_NOPE:]), zq], axis=-1)
    wkv = w_ukv.reshape(KV_RANK, MLA_HEADS, QK_NOPE + V_DIM)
    wk = jnp.concatenate([wkv[..., :QK_NOPE], jnp.zeros((KV_RANK, MLA_HEADS, HEAD_LANES - QK_NOPE), F32)], axis=-1)
    wv = jnp.concatenate([wkv[..., QK_NOPE:], jnp.zeros((KV_RANK, MLA_HEADS, HEAD_LANES - V_DIM), F32)], axis=-1)
    flat = lambda w: w.reshape(w.shape[0], MLA_HEADS * HEAD_LANES).astype(BF16)
    return w1, flat(wq_pad), flat(wq_par), flat(wk), flat(wv)


def _rope_tables(n, n_ident):
    rows = n // GRID_W
    row = np.repeat(np.arange(rows, dtype=np.float32), GRID_W)
    col = np.tile(np.arange(GRID_W, dtype=np.float32), rows)
    n_freq = QK_ROPE // 4
    inv_freq = (np.float32(ROPE_BASE) ** (-np.arange(n_freq, dtype=np.float32) / n_freq)).astype(np.float32)
    ang = np.concatenate([row[:, None] * inv_freq, col[:, None] * inv_freq], axis=-1).astype(np.float64)
    ang = np.concatenate([ang, np.zeros((n_ident, QK_ROPE // 2))], axis=0)
    tot = n + n_ident
    pad = HEAD_LANES - QK_NOPE - QK_ROPE
    cos = np.concatenate([np.ones((tot, QK_NOPE)), np.cos(ang), np.cos(ang), np.ones((tot, pad))], axis=1)
    sin = np.concatenate([np.zeros((tot, QK_NOPE)), np.sin(ang), np.sin(ang), np.zeros((tot, pad))], axis=1)
    return jnp.asarray(cos, F32), jnp.asarray(sin, F32)


def _even_in(tok, m, g, weights, q_norm, kv_norm, *, n, tm, mod_row, n_lat_rows):
    t, d = tok.shape
    w1, wq, wqp, wk, wv = weights
    cos, sin = _rope_tables(n, tm)
    tiles_lat = n_lat_rows // tm
    tpb = n // tm
    tab_row = lambda i: jnp.where(i < tiles_lat, i % tpb, tpb)
    full = lambda a: pl.BlockSpec(a.shape, lambda i: (0,) * a.ndim)
    hl = MLA_HEADS * HEAD_LANES
    s_ch = w1.shape[1] - Q_RANK - KV_RANK - 2 * HEAD_LANES
    qn, kvn = q_norm.reshape(1, Q_RANK), kv_norm.reshape(1, KV_RANK)
    gg = g.reshape(1, d)
    head_spec = pl.BlockSpec((MLA_HEADS, tm, HEAD_LANES), lambda i: (0, i, 0))
    head_shape = jax.ShapeDtypeStruct((MLA_HEADS, t, HEAD_LANES), BF16)
    head_t_spec = pl.BlockSpec((MLA_HEADS, HEAD_LANES, tm), lambda i: (0, 0, i))
    head_t_shape = jax.ShapeDtypeStruct((MLA_HEADS, HEAD_LANES, t), BF16)
    pairs = s_ch // (2 * S5_GROUP)
    u_w = 2 * S5_GROUP * S5_CHUNK
    return pl.pallas_call(
        _even_in_kernel,
        grid=(t // tm,),
        in_specs=[
            pl.BlockSpec((tm, d), lambda i: (i, 0)),
            pl.BlockSpec((None, N_MOD, d), lambda i: (mod_row(i), 0, 0)),
            full(gg), full(w1), full(qn), full(kvn), full(wq), full(wqp), full(wk), full(wv),
            pl.BlockSpec((tm, HEAD_LANES), lambda i: (tab_row(i), 0)),
            pl.BlockSpec((tm, HEAD_LANES), lambda i: (tab_row(i), 0)),
        ],
        out_specs=[head_t_spec, head_spec, head_t_spec,
                   pl.BlockSpec((pairs, tm // S5_CHUNK, u_w), lambda i: (0, i, 0))],
        out_shape=[head_t_shape, head_shape, head_t_shape,
                   jax.ShapeDtypeStruct((pairs, t // S5_CHUNK, u_w), BF16)],
        scratch_shapes=[pltpu.VMEM((s_ch // LANES, tm, LANES), F32)],
        compiler_params=_cparams(("parallel",)),
        name="even_in",
    )(tok, m, gg, w1, qn, kvn, wq, wqp, wk, wv, cos, sin)


def _attn_kernel(qt_ref, kc_ref, vtc_ref, *rest, tk, lat):
    if lat:
        kl_ref, vtl_ref, o_ref = rest
    else:
        _, o_ref = rest
    tq = qt_ref.shape[1]
    lanes = min(ATT_Q_LANES, tq)
    subs = [slice(j * lanes, (j + 1) * lanes) for j in range(tq // lanes)]

    def step(k, vt, carry):
        scores = [_dot(k, qt_ref[:, sl]) for sl in subs]
        out = []
        for s, (m, acc) in zip(scores, carry):
            m_new = jnp.maximum(m, jnp.max(s, axis=0, keepdims=True))
            p = jnp.exp2(s - m_new).astype(BF16)
            out.append((m_new, jnp.exp2(m - m_new) * acc + _dot(vt, p)))
        return tuple(out)

    carry = tuple((jnp.full((1, lanes), NEG_BIG, F32), jnp.zeros((ATT_V_ROWS, lanes), F32)) for _ in subs)
    carry = step(kc_ref[...], vtc_ref[:ATT_V_ROWS, :], carry)
    if lat:
        def body(c, carry):
            start = pl.multiple_of(c * tk, tk)
            return step(kl_ref[pl.ds(start, tk), :], vtl_ref[:ATT_V_ROWS, pl.ds(start, tk)], carry)

        n_chunks = kl_ref.shape[0] // tk
        carry = lax.fori_loop(0, n_chunks, body, carry, unroll=2 if n_chunks % 2 == 0 else 1)
    for sl, (_, acc) in zip(subs, carry):
        o = acc / acc[ONES_LANE:ONES_LANE + 1]
        o = jnp.concatenate([o, jnp.zeros((HEAD_LANES - ATT_V_ROWS, lanes), F32)], axis=0)
        o_ref[sl, :] = o.T.astype(BF16)


def _attention(qt, k, vt, *, b, n, nc):
    h, t, _ = k.shape
    tq = min(ATT_Q_TILE, n)
    n_q = n // tq
    tk = _row_tile(n)
    lat_blocks = (b * n) // nc
    out_shape = jax.ShapeDtypeStruct((h, t, HEAD_LANES), BF16)
    o = pl.pallas_call(
        functools.partial(_attn_kernel, tk=tk, lat=True),
        grid=(b, h, n_q),
        in_specs=[
            pl.BlockSpec((None, HEAD_LANES, tq), lambda bi, hi, i: (hi, 0, bi * n_q + i)),
            pl.BlockSpec((None, nc, HEAD_LANES), lambda bi, hi, i: (hi, lat_blocks + bi, 0)),
            pl.BlockSpec((None, HEAD_LANES, nc), lambda bi, hi, i: (hi, 0, lat_blocks + bi)),
            pl.BlockSpec((None, n, HEAD_LANES), lambda bi, hi, i: (hi, bi, 0)),
            pl.BlockSpec((None, HEAD_LANES, n), lambda bi, hi, i: (hi, 0, bi)),
        ],
        out_specs=pl.BlockSpec((None, tq, HEAD_LANES), lambda bi, hi, i: (hi, bi * n_q + i, 0)),
        out_shape=out_shape,
        compiler_params=_cparams(("parallel", "parallel", "arbitrary")),
        name="attention",
    )(qt, k, vt, k, vt)
    return pl.pallas_call(
        functools.partial(_attn_kernel, tk=tk, lat=False),
        grid=(b, h),
        in_specs=[
            pl.BlockSpec((None, HEAD_LANES, nc), lambda bi, hi: (hi, 0, lat_blocks + bi)),
            pl.BlockSpec((None, nc, HEAD_LANES), lambda bi, hi: (hi, lat_blocks + bi, 0)),
            pl.BlockSpec((None, HEAD_LANES, nc), lambda bi, hi: (hi, 0, lat_blocks + bi)),
            pl.BlockSpec(memory_space=pl.ANY),
        ],
        out_specs=pl.BlockSpec((None, nc, HEAD_LANES), lambda bi, hi: (hi, lat_blocks + bi, 0)),
        out_shape=out_shape,
        input_output_aliases={3: 0},
        compiler_params=_cparams(("parallel", "parallel")),
        name="attention_ctx",
    )(qt, k, vt, o)


def _s5_operators(lam_re, lam_im, log_step, b_re, b_im, c_re, c_im, d_skip):
    tc = S5_CHUNK
    g_n, p_n = lam_re.shape[1], lam_re.shape[2]
    lam = lax.complex(lam_re.astype(F32), lam_im.astype(F32))
    step = jnp.exp(log_step.astype(F32))[..., None]
    la = lam * step
    a_bar = jnp.exp(la)
    bb = ((a_bar - 1.0) / lam)[..., None] * lax.complex(b_re.astype(F32), b_im.astype(F32))
    cm = lax.complex(c_re.astype(F32), c_im.astype(F32))
    kk = jnp.arange(tc + 1, dtype=F32)
    apow = jnp.exp(la[..., None] * kk)

    win_f = jnp.einsum('gps,gpc->gscp', apow[0][..., tc - 1::-1][..., :tc], bb[0])
    win_b = jnp.einsum('gps,gpc->gscp', apow[1][..., :tc], bb[1])
    wout_f = jnp.einsum('gcp,gpt->gptc', cm[0], apow[0][..., 1:])
    wout_b = jnp.einsum('gcp,gpt->gptc', cm[1], apow[1][..., tc:0:-1])
    kf = jnp.real(jnp.einsum('gop,gpk,gpi->gkoi', cm[0], apow[0][..., :tc], bb[0]))
    kb = jnp.real(jnp.einsum('gop,gpk,gpi->gkoi', cm[1], apow[1][..., :tc], bb[1]))
    s_idx = jnp.arange(tc)[:, None]
    t_idx = jnp.arange(tc)[None, :]
    lag = t_idx - s_idx
    kf_st = jnp.where((lag >= 0)[None, :, :, None, None], kf[:, jnp.clip(lag, 0, tc - 1)], 0.0)
    kb_st = jnp.where((lag <= 0)[None, :, :, None, None], kb[:, jnp.clip(-lag, 0, tc - 1)], 0.0)
    eye_t = jnp.eye(tc, dtype=F32)[None, :, :, None, None]
    eye_c = jnp.eye(S5_GROUP, dtype=F32)[None, None, None]
    dsk = d_skip.astype(F32).reshape(g_n, 1, 1, S5_GROUP, 1)
    loc = kf_st + kb_st + eye_t * eye_c * dsk
    loc = jnp.transpose(loc, (0, 1, 4, 2, 3))

    gp = g_n // 2
    gc2 = 2 * tc * S5_GROUP
    eye2 = jnp.eye(2, dtype=F32)
    pair = lambda x: x.reshape((gp, 2) + x.shape[1:])
    w_in_part = lambda w: jnp.einsum('qgscp,gh->qsgchp', pair(w), eye2).reshape(gp, gc2, 2 * p_n)
    w_out_part = lambda w: jnp.einsum('qgptc,gh->qgpthc', pair(w), eye2).reshape(gp, 2 * p_n, gc2)
    w_in = jnp.concatenate([w_in_part(jnp.real(win_f)), w_in_part(jnp.imag(win_f)),
                            w_in_part(jnp.real(win_b)), w_in_part(jnp.imag(win_b))], axis=2)
    w_out = jnp.concatenate([w_out_part(jnp.real(wout_f)), w_out_part(-jnp.imag(wout_f)),
                             w_out_part(jnp.real(wout_b)), w_out_part(-jnp.imag(wout_b))], axis=1)
    loc = jnp.einsum('qgsito,gh->qsgitho', pair(loc), eye2).reshape(gp, gc2, gc2)
    a_tc = apow[..., tc].reshape(2, gp, 2 * p_n)
    a_rows = jnp.stack([jnp.real(a_tc[0]), jnp.imag(a_tc[0]), jnp.real(a_tc[1]), jnp.imag(a_tc[1])], axis=1)
    a_rows = jnp.concatenate([a_rows, jnp.zeros((gp, SUBLANES - 4, 2 * p_n), F32)], axis=1)
    return w_in.astype(BF16), w_out.astype(BF16), loc.astype(BF16), a_rows


def _s5_kernel(u_ref, win_ref, wout_ref, loc_ref, a_ref, y_ref, s_scr, h_scr, *, batch, lat_chunks, ctx_chunks):
    u = u_ref[...]
    w = a_ref.shape[1]
    n_parts = s_scr.shape[0]
    s_all = _dot(u, win_ref[...])
    for k in range(n_parts):
        s_scr[k] = s_all[:, k * w:(k + 1) * w]
    a = a_ref[...]
    bc = lambda r: jnp.broadcast_to(a[r:r + 1], (batch, w))
    arf, aif, arb, aib = bc(0), bc(1), bc(2), bc(3)

    def chunk(rows, carry, part, ar, ai):
        hr, hi = carry
        sr = s_scr[part, rows, :]
        si = s_scr[part + 1, rows, :]
        h_scr[part, rows, :] = hr
        h_scr[part + 1, rows, :] = hi
        return ar * hr - ai * hi + sr, ar * hi + ai * hr + si

    def segment(base, per):
        def body(j, carry):
            cf, cb = carry
            cf = chunk(pl.ds(base + j, batch, stride=per), cf, 0, arf, aif)
            cb = chunk(pl.ds(base + per - 1 - j, batch, stride=per), cb, 2, arb, aib)
            return cf, cb
        return body

    zero = (jnp.zeros((batch, w), F32), jnp.zeros((batch, w), F32))
    carry = lax.fori_loop(0, ctx_chunks, segment(batch * lat_chunks, ctx_chunks), (zero, zero))
    lax.fori_loop(0, lat_chunks, segment(0, lat_chunks), carry)
    h_all = jnp.concatenate([h_scr[k] for k in range(n_parts)], axis=-1).astype(BF16)
    y_ref[...] = _dot(u, loc_ref[...]) + _dot(h_all, wout_ref[...])


def _s5(u, ops, *, b, n, nc):
    w_in, w_out, loc, a_rows = ops
    gp, r, wid = u.shape
    sw = w_in.shape[2]
    return pl.pallas_call(
        functools.partial(_s5_kernel, batch=b, lat_chunks=n // S5_CHUNK, ctx_chunks=nc // S5_CHUNK),
        grid=(gp,),
        in_specs=[
            pl.BlockSpec((None, r, wid), lambda g: (g, 0, 0)),
            pl.BlockSpec((None, wid, sw), lambda g: (g, 0, 0)),
            pl.BlockSpec((None, sw, wid), lambda g: (g, 0, 0)),
            pl.BlockSpec((None, wid, wid), lambda g: (g, 0, 0)),
            pl.BlockSpec((None, SUBLANES, sw // 4), lambda g: (g, 0, 0)),
        ],
        out_specs=pl.BlockSpec((None, r, wid), lambda g: (g, 0, 0)),
        out_shape=jax.ShapeDtypeStruct((gp, r, wid), F32),
        scratch_shapes=[pltpu.VMEM((4, r, sw // 4), F32), pltpu.VMEM((4, r, sw // 4), F32)],
        compiler_params=_cparams(("parallel",)),
        name="s5",
    )(u, w_in, w_out, loc, a_rows)


def _even_out_kernel(x_ref, m_ref, o_ref, yc_ref, wglu_ref, bglu_ref, woo_ref, wos_ref, out_ref, ys_scr):
    pw = 2 * S5_GROUP
    per_tile = LANES // pw
    chunks = yc_ref.shape[1]
    for lt in range(ys_scr.shape[0]):
        by_pair = [yc_ref[lt * per_tile + l] for l in range(per_tile)]
        for t in range(S5_CHUNK):
            ys_scr[lt, pl.ds(t, chunks, stride=S5_CHUNK), :] = jnp.concatenate(
                [y[:, t * pw:(t + 1) * pw] for y in by_pair], axis=-1)
    y = _gelu_tanh(jnp.concatenate([ys_scr[lt] for lt in range(ys_scr.shape[0])], axis=-1))
    y = y * _sigmoid(_dot(y.astype(BF16), wglu_ref[...]) + bglu_ref[...])
    oc = jnp.concatenate([o_ref[h] for h in range(MLA_HEADS)], axis=-1)
    mix = _dot(oc, woo_ref[...]) + _dot(y.astype(BF16), wos_ref[...])
    out_ref[...] = x_ref[...] + m_ref[...][5:6] * mix


def _even_out(tok, m, o, yc, w_glu, b_glu, w_out, *, tm, mod_row):
    t, d = tok.shape
    pairs, _, wid = yc.shape
    sc = pairs * 2 * S5_GROUP
    wo = w_out[:MLA_HEADS * V_DIM].reshape(MLA_HEADS, V_DIM, d)
    woo = jnp.concatenate([wo, jnp.zeros((MLA_HEADS, HEAD_LANES - V_DIM, d), F32)], axis=1)
    woo = woo.reshape(MLA_HEADS * HEAD_LANES, d).astype(BF16)
    wos = w_out[MLA_HEADS * V_DIM:].astype(BF16)
    wg = w_glu.astype(BF16)
    bg = b_glu.reshape(1, sc).astype(F32)
    full = lambda a: pl.BlockSpec(a.shape, lambda i: (0,) * a.ndim)
    return pl.pallas_call(
        _even_out_kernel,
        grid=(t // tm,),
        in_specs=[
            pl.BlockSpec((tm, d), lambda i: (i, 0)),
            pl.BlockSpec((None, N_MOD, d), lambda i: (mod_row(i), 0, 0)),
            pl.BlockSpec((MLA_HEADS, tm, HEAD_LANES), lambda i: (0, i, 0)),
            pl.BlockSpec((pairs, tm // S5_CHUNK, wid), lambda i: (0, i, 0)),
            full(wg), full(bg), full(woo), full(wos),
        ],
        out_specs=pl.BlockSpec((tm, d), lambda i: (i, 0)),
        out_shape=jax.ShapeDtypeStruct((t, d), F32),
        scratch_shapes=[pltpu.VMEM((sc // LANES, tm, LANES), F32)],
        compiler_params=_cparams(("parallel",)),
        name="even_out",
    )(tok, m, o, yc, wg, bg, woo, wos)


def _odd_in_kernel(x_ref, m_ref, g_ref, w_ref, hy_out, lx_out, lg_out):
    m = m_ref[...]
    hb = (_rms(x_ref[...], g_ref[...]) * (1.0 + m[4:5]) + m[3:4]).astype(BF16)
    p = _dot(hb, w_ref[...])
    n_hy, n_lx = hy_out.shape[1], lx_out.shape[1]
    hy_out[...] = p[:, :n_hy]
    lx_out[...] = p[:, n_hy:n_hy + n_lx]
    lg_out[...] = p[:, n_hy + n_lx:]


def _odd_in(tok, m, g, w_in, *, hy_w, lru_w, tm, mod_row):
    t, d = tok.shape
    w = w_in.astype(BF16)
    gg = g.reshape(1, d)
    full = lambda a: pl.BlockSpec(a.shape, lambda i: (0,) * a.ndim)
    row = lambda c: pl.BlockSpec((tm, c), lambda i: (i, 0))
    return pl.pallas_call(
        _odd_in_kernel,
        grid=(t // tm,),
        in_specs=[row(d), pl.BlockSpec((None, N_MOD, d), lambda i: (mod_row(i), 0, 0)), full(gg), full(w)],
        out_specs=[row(hy_w), row(lru_w), row(lru_w)],
        out_shape=[jax.ShapeDtypeStruct((t, c), F32) for c in (hy_w, lru_w, lru_w)],
        compiler_params=_cparams(("parallel",)),
        name="odd_in",
    )(tok, m, gg, w)


def _shift_down(cur, prev8, k):
    r = pltpu.roll(cur, k, 0)
    row = lax.broadcasted_iota(jnp.int32, prev8.shape, 0)
    head = jnp.where(row < k, pltpu.roll(prev8, k, 0), r[:SUBLANES])
    return jnp.concatenate([head, r[SUBLANES:]], axis=0)


def _shift_up(cur, next8, k):
    rows = cur.shape[0]
    r = pltpu.roll(cur, rows - k, 0)
    row = lax.broadcasted_iota(jnp.int32, next8.shape, 0)
    tail = jnp.where(row >= SUBLANES - k, pltpu.roll(next8, SUBLANES - k, 0), r[rows - SUBLANES:])
    return jnp.concatenate([r[:rows - SUBLANES], tail], axis=0)


def _halo_specs(tt, width, n_tiles, tile_of):
    per = tt // SUBLANES
    last = n_tiles * per - 1
    cur = pl.BlockSpec((None, tt, width), lambda b, i: (b, tile_of(i), 0))
    prev = pl.BlockSpec((None, SUBLANES, width), lambda b, i: (b, jnp.maximum(tile_of(i) * per - 1, 0), 0))
    nxt = pl.BlockSpec((None, SUBLANES, width), lambda b, i: (b, jnp.minimum((tile_of(i) + 1) * per, last), 0))
    return cur, prev, nxt


def _hyena_prep_kernel(cur_ref, prev_ref, next_ref, w_ref, b_ref, z_out, x0_out):
    i = pl.program_id(1)
    cur = cur_ref[...]
    prev8 = prev_ref[...] * (i > 0).astype(F32)
    next8 = next_ref[...] * (i < pl.num_programs(1) - 1).astype(F32)
    w = w_ref[...]
    u = w[0:1] * _shift_down(cur, prev8, 1) + w[1:2] * cur + w[2:3] * _shift_up(cur, next8, 1) + b_ref[...]
    c = z_out.shape[1]
    x0_out[...] = u[:, :c]
    z_out[...] = u[:, 2 * c:] * u[:, c:2 * c]


def _hyena_prep(hy, conv_w, conv_b, *, tt):
    b, n, c3 = hy.shape
    c = c3 // 3
    cur, prev, nxt = _halo_specs(tt, c3, n // tt, lambda i: i)
    w8 = jnp.concatenate([conv_w.astype(F32), jnp.zeros((SUBLANES - conv_w.shape[0], c3), F32)], axis=0)
    out_spec = pl.BlockSpec((None, tt, c), lambda bi, i: (bi, i, 0))
    return pl.pallas_call(
        _hyena_prep_kernel,
        grid=(b, n // tt),
        in_specs=[cur, prev, nxt, pl.BlockSpec((SUBLANES, c3), lambda bi, i: (0, 0)),
                  pl.BlockSpec((1, c3), lambda bi, i: (0, 0))],
        out_specs=[out_spec, out_spec],
        out_shape=[jax.ShapeDtypeStruct((b, n, c), F32)] * 2,
        compiler_params=_cparams(("parallel", "parallel")),
        name="hyena_prep",
    )(hy, hy, hy, w8, conv_b.reshape(1, c3).astype(F32))


class _Dft:
    def __init__(self, n):
        self.n = n
        self.N = 2 * n
        self.N2 = FFT_N2
        self.N1 = self.N // self.N2
        self.A = n // self.N2
        self.K1 = self.N1 // 2 + 1
        self.K1p = -(-self.K1 // SUBLANES) * SUBLANES
        N, N1, N2, A, K1, K1p = self.N, self.N1, self.N2, self.A, self.K1, self.K1p
        b = np.arange(N2)[:, None, None]
        k1 = np.arange(K1p)[None, :, None]
        a = np.arange(A)[None, None, :]
        phi = 2.0 * np.pi * ((a * k1 % N1) / N1 + (b * k1 % N) / N)
        live = (k1 < K1).astype(np.float64)
        f1 = np.concatenate([np.cos(phi) * live, -np.sin(phi) * live], axis=1)
        wgt = np.where((k1 == 0) | (k1 == N1 // 2), 1.0, 2.0) * live / N
        g1 = np.concatenate([np.cos(phi) * wgt, -np.sin(phi) * wgt], axis=1)
        g1 = np.transpose(g1, (0, 2, 1))
        ang = 2.0 * np.pi * (np.arange(N2)[:, None] * np.arange(N2)[None, :] % N2) / N2
        c, s = np.cos(ang), np.sin(ang)
        f2 = np.block([[c, s], [-s, c]])
        g2 = np.block([[c, -s], [s, c]])
        self.f1, self.f2, self.g2, self.g1 = (jnp.asarray(x, F32).astype(BF16) for x in (f1, f2, g2, g1))


def _dft_forward(src_ref, f1_ref, f2_ref, y_scr, put, dft):
    n2, a_n, k1p2 = dft.N2, dft.A, 2 * dft.K1p

    def stage1(b, _):
        xb = src_ref[pl.ds(b, a_n, stride=n2), :]
        r0 = pl.multiple_of(b * k1p2, SUBLANES)
        y_scr[pl.ds(r0, k1p2), :] = _dot(f1_ref[b], xb.astype(BF16))
        return 0

    lax.fori_loop(0, n2, stage1, 0)

    def stage2(k1, _):
        yr = y_scr[pl.ds(k1, n2, stride=k1p2), :]
        yi = y_scr[pl.ds(dft.K1p + k1, n2, stride=k1p2), :]
        put(k1, _dot(f2_ref[...], jnp.concatenate([yr, yi], axis=0).astype(BF16)))
        return 0

    lax.fori_loop(0, dft.K1, stage2, 0)


def _hyena_filter_kernel(feat_ref, w1_ref, b1_ref, w2_ref, b2_ref, w3f_ref, w3b_ref, freq_ref, delta_ref,
                         f1_ref, f2_ref, h_out, filt_scr, y_scr, *, dft):
    n = dft.n
    feat = feat_ref[...]
    t = feat[:, 0:1]
    freq = freq_ref[...]
    f_hi, f_lo = _split(feat)
    w_hi, w_lo = _split(w1_ref[...])
    h = jnp.sin(freq[0:1] * (_dot3(f_hi, f_lo, w_hi, w_lo) + b1_ref[...]))
    h_hi, h_lo = _split(h)
    w_hi, w_lo = _split(w2_ref[...])
    h = jnp.sin(freq[1:2] * (_dot3(h_hi, h_lo, w_hi, w_lo) + b2_ref[...]))
    h_hi, h_lo = _split(h)
    decay = jnp.exp(-t * delta_ref[...])
    n2 = dft.N2

    w_hi, w_lo = _split(w3f_ref[...])
    filt_scr[...] = _dot3(h_hi, h_lo, w_hi, w_lo) * decay

    def put_fwd(k1, x):
        h_out[k1] = x

    _dft_forward(filt_scr, f1_ref, f2_ref, y_scr, put_fwd, dft)

    w_hi, w_lo = _split(w3b_ref[...])
    row = lax.broadcasted_iota(jnp.int32, (n, LANES), 0)
    filt_scr[...] = jnp.where(row == 0, 0.0, _dot3(h_hi, h_lo, w_hi, w_lo) * decay)
    sign = jnp.where(lax.broadcasted_iota(jnp.int32, (2 * n2, LANES), 0) < n2, 1.0, -1.0)

    def put_bwd(k1, x):
        h_out[k1] = h_out[k1] + sign * x

    _dft_forward(filt_scr, f1_ref, f2_ref, y_scr, put_bwd, dft)


def _hyena_features(n):
    t = np.linspace(0.0, 1.0, n, dtype=np.float32).astype(np.float64)[:, None]
    w = (2.0 * np.pi * np.arange(n, dtype=np.float64)[:, None] / n).astype(np.float32).astype(np.float64)
    bands = np.linspace(1e-4, HY_BANDS - 1, HY_BANDS, dtype=np.float32).astype(np.float64)[None, :]
    bw = (bands * w).astype(np.float32).astype(np.float64)
    z = np.concatenate([t, np.cos(bw), -np.sin(bw)], axis=-1)
    z = np.concatenate([z, np.zeros((n, LANES - z.shape[1]))], axis=-1)
    return jnp.asarray(z, F32)


def _hyena_filter_spectrum(n, w1, b1, w2, b2, w3, sin_freq, dft):
    ch = w3.shape[1] // 2
    hid = w1.shape[1]
    feat = _hyena_features(n)
    w1p = jnp.concatenate([w1.astype(F32), jnp.zeros((LANES - w1.shape[0], hid), F32)], axis=0)
    deltas = jnp.asarray(np.abs(np.linspace(HY_MIN_DECAY, HY_MAX_DECAY, ch, dtype=np.float32)), F32).reshape(1, ch)
    freq8 = jnp.concatenate([sin_freq.astype(F32), jnp.zeros((SUBLANES - 2, hid), F32)], axis=0)
    full = lambda a: pl.BlockSpec(a.shape, lambda c: (0,) * a.ndim)
    tiles = ch // LANES
    args = (feat, w1p, b1.reshape(1, hid).astype(F32), w2.astype(F32), b2.reshape(1, hid).astype(F32))
    return pl.pallas_call(
        functools.partial(_hyena_filter_kernel, dft=dft),
        grid=(tiles,),
        in_specs=[full(a) for a in args] + [
            pl.BlockSpec((hid, LANES), lambda c: (0, c)),
            pl.BlockSpec((hid, LANES), lambda c: (0, tiles + c)),
            full(freq8),
            pl.BlockSpec((1, LANES), lambda c: (0, c)),
            full(dft.f1), full(dft.f2),
        ],
        out_specs=pl.BlockSpec((dft.K1, 2 * dft.N2, LANES), lambda c: (0, 0, c)),
        out_shape=jax.ShapeDtypeStruct((dft.K1, 2 * dft.N2, ch), F32),
        scratch_shapes=[pltpu.VMEM((n, LANES), F32), pltpu.VMEM((dft.N2 * 2 * dft.K1p, LANES), F32)],
        compiler_params=_cparams(("parallel",)),
        name="hyena_filter",
    )(*args, w3.astype(F32), w3.astype(F32), freq8, deltas, dft.f1, dft.f2)


def _hyena_conv_kernel(z_ref, h_ref, f1_ref, f2_ref, g2_ref, g1_ref, y_out, y_scr, x_scr, *, dft):
    n2, a_n, k1p = dft.N2, dft.A, dft.K1p

    def put(k1, x):
        hk = h_ref[k1]
        xr, xi, hr, hi = x[:n2], x[n2:], hk[:n2], hk[n2:]
        prod = jnp.concatenate([xr * hr - xi * hi, xr * hi + xi * hr], axis=0).astype(BF16)
        r0 = pl.multiple_of(k1 * 2 * n2, SUBLANES)
        x_scr[pl.ds(r0, 2 * n2), :] = _dot(g2_ref[...], prod)

    if k1p > dft.K1:
        x_scr[dft.K1 * 2 * n2:, :] = jnp.zeros(((k1p - dft.K1) * 2 * n2, LANES), F32)
    _dft_forward(z_ref, f1_ref, f2_ref, y_scr, put, dft)

    def last(b, _):
        yr = x_scr[pl.ds(b, k1p, stride=2 * n2), :]
        yi = x_scr[pl.ds(n2 + b, k1p, stride=2 * n2), :]
        y_out[pl.ds(b, a_n, stride=n2), :] = _dot(g1_ref[b], jnp.concatenate([yr, yi], axis=0).astype(BF16))
        return 0

    lax.fori_loop(0, n2, last, 0)


def _hyena_conv(z, spec, dft):
    b, n, ch = z.shape
    once = pl.Buffered(1)
    full = lambda a: pl.BlockSpec(a.shape, lambda c, bi: (0,) * a.ndim, pipeline_mode=once)
    io_spec = pl.BlockSpec((None, n, LANES), lambda c, bi: (bi, 0, c))
    return pl.pallas_call(
        functools.partial(_hyena_conv_kernel, dft=dft),
        grid=(ch // LANES, b),
        in_specs=[io_spec, pl.BlockSpec((dft.K1, 2 * dft.N2, LANES), lambda c, bi: (0, 0, c), pipeline_mode=once),
                  full(dft.f1), full(dft.f2), full(dft.g2), full(dft.g1)],
        out_specs=io_spec,
        out_shape=jax.ShapeDtypeStruct((b, n, ch), F32),
        scratch_shapes=[pltpu.VMEM((dft.N2 * 2 * dft.K1p, LANES), F32), pltpu.VMEM((dft.K1p * 2 * dft.N2, LANES), F32)],
        compiler_params=_cparams(("parallel", "arbitrary")),
        name="hyena_conv",
    )(z, spec, dft.f1, dft.f2, dft.g2, dft.g1)


def _lru_tile_terms(cur, prev8, next8, conv_w, conv_b, wg, bg, sp, reverse):
    rows, c = cur.shape
    xc = (conv_w[0:1] * _shift_down(cur, prev8, 2) + conv_w[1:2] * _shift_down(cur, prev8, 1)
          + conv_w[2:3] * cur + conv_w[3:4] * _shift_up(cur, next8, 1) + conv_b)
    g = _dot(xc.astype(BF16), wg) + bg
    r, ig = _sigmoid(g[:, :c]), _sigmoid(g[:, c:])
    log_a = -LRU_C * r * sp
    a = jnp.exp(log_a)
    th = jnp.tanh(log_a)
    bb = jnp.sqrt(-2.0 * th / (1.0 - th)) * (ig * xc)
    sub = lax.broadcasted_iota(jnp.int32, (rows, c), 0) & (SUBLANES - 1)
    for k in (1, 2, 4):
        if reverse:
            ok = sub < SUBLANES - k
            a_s, b_s = pltpu.roll(a, rows - k, 0), pltpu.roll(bb, rows - k, 0)
        else:
            ok = sub >= k
            a_s, b_s = pltpu.roll(a, k, 0), pltpu.roll(bb, k, 0)
        bb = jnp.where(ok, bb + a * b_s, bb)
        a = jnp.where(ok, a * a_s, a)
    return a, bb


def _lru_kernel(ctx_ref, curf_ref, prevf_ref, nextf_ref, curb_ref, prevb_ref, nextb_ref,
                cw_ref, cb_ref, wgf_ref, wgb_ref, bgf_ref, bgb_ref, sp_ref,
                hf_out, hb_out, a_scr, b_scr, carry_scr, *, batch, tt):
    i = pl.program_id(0)
    nt = pl.num_programs(0)
    cw, cb = cw_ref[...], cb_ref[...]
    sp = sp_ref[...]
    c = cw.shape[1]
    params = ((wgf_ref, bgf_ref, sp[0:1]), (wgb_ref, bgb_ref, sp[1:2]))

    def carry_scan(rows, write):
        groups = rows // SUBLANES

        def body(j, carries):
            new = []
            for d in range(2):
                jj = j if d == 0 else groups - 1 - j
                r0 = pl.multiple_of(jj * SUBLANES, SUBLANES)
                for bi in range(batch):
                    h8 = b_scr[d, bi, pl.ds(r0, SUBLANES), :] + a_scr[d, bi, pl.ds(r0, SUBLANES), :] * carries[d * batch + bi]
                    write(d, bi, r0, h8)
                    edge = h8[SUBLANES - 1:SUBLANES] if d == 0 else h8[0:1]
                    new.append(jnp.broadcast_to(edge, (SUBLANES, c)))
            return tuple(new)

        init = tuple(carry_scr[d, bi] for d in range(2) for bi in range(batch))
        out = lax.fori_loop(0, groups, body, init)
        for d in range(2):
            for bi in range(batch):
                carry_scr[d, bi] = out[d * batch + bi]

    @pl.when(i == 0)
    def _():
        carry_scr[...] = jnp.zeros(carry_scr.shape, F32)
        nc = ctx_ref.shape[1]
        zero8 = jnp.zeros((SUBLANES, c), F32)
        for d in range(2):
            wg, bg, spd = params[d]
            for bi in range(batch):
                a, bb = _lru_tile_terms(ctx_ref[bi], zero8, zero8, cw, cb, wg[...], bg[...], spd, d == 1)
                a_scr[d, bi, :nc, :] = a
                b_scr[d, bi, :nc, :] = bb
        carry_scan(nc, lambda d, bi, r0, h8: None)

    tiles = ((curf_ref, prevf_ref, nextf_ref, i), (curb_ref, prevb_ref, nextb_ref, nt - 1 - i))
    for d in range(2):
        cur_ref, prev_ref, next_ref, ti = tiles[d]
        wg, bg, spd = params[d]
        has_prev = (ti > 0).astype(F32)
        has_next = (ti < nt - 1).astype(F32)
        for bi in range(batch):
            a, bb = _lru_tile_terms(cur_ref[bi], prev_ref[bi] * has_prev, next_ref[bi] * has_next,
                                    cw, cb, wg[...], bg[...], spd, d == 1)
            a_scr[d, bi, :tt, :] = a
            b_scr[d, bi, :tt, :] = bb

    def write(d, bi, r0, h8):
        if d == 0:
            hf_out[bi, pl.ds(r0, SUBLANES), :] = h8
        else:
            hb_out[bi, pl.ds(r0, SUBLANES), :] = h8

    carry_scan(tt, write)


def _lru_gate_weights(w_a, w_x):
    def dense(w):
        nb, bs = w.shape[0], w.shape[1]
        eye = jnp.eye(nb, dtype=F32)
        return jnp.einsum('hij,hk->hikj', w.astype(F32), eye).reshape(nb * bs, nb * bs)
    return [jnp.concatenate([dense(w_a[d]), dense(w_x[d])], axis=1).astype(BF16) for d in range(2)]


def _lru(lx_lat, lx_ctx, conv_w, conv_b, w_a, b_a, w_x, b_x, lam, *, tt):
    b, n, c = lx_lat.shape
    nc = lx_ctx.shape[1]
    nt = n // tt
    wgf, wgb = _lru_gate_weights(w_a, w_x)
    bgf = jnp.concatenate([b_a[0], b_x[0]]).reshape(1, 2 * c).astype(F32)
    bgb = jnp.concatenate([b_a[1], b_x[1]]).reshape(1, 2 * c).astype(F32)
    sp = jax.nn.softplus(-lam.astype(F32))
    sp8 = jnp.concatenate([sp, jnp.zeros((SUBLANES - 2, c), F32)], axis=0)
    cw8 = jnp.concatenate([conv_w.astype(F32), jnp.zeros((SUBLANES - conv_w.shape[0], c), F32)], axis=0)
    cb = conv_b.reshape(1, c).astype(F32)
    per = tt // SUBLANES
    last = n // SUBLANES - 1

    def specs(tile_of):
        cur = pl.BlockSpec((b, tt, c), lambda i: (0, tile_of(i), 0))
        prev = pl.BlockSpec((b, SUBLANES, c), lambda i: (0, jnp.maximum(tile_of(i) * per - 1, 0), 0))
        nxt = pl.BlockSpec((b, SUBLANES, c), lambda i: (0, jnp.minimum((tile_of(i) + 1) * per, last), 0))
        return [cur, prev, nxt]

    fwd_tile = lambda i: i
    bwd_tile = lambda i: nt - 1 - i
    full = lambda a: pl.BlockSpec(a.shape, lambda i: (0,) * a.ndim)
    rows_scr = max(tt, nc)
    return pl.pallas_call(
        functools.partial(_lru_kernel, batch=b, tt=tt),
        grid=(nt,),
        in_specs=[full(lx_ctx)] + specs(fwd_tile) + specs(bwd_tile)
        + [full(cw8), full(cb), full(wgf), full(wgb), full(bgf), full(bgb), full(sp8)],
        out_specs=[pl.BlockSpec((b, tt, c), lambda i: (0, i, 0)), pl.BlockSpec((b, tt, c), lambda i: (0, nt - 1 - i, 0))],
        out_shape=[jax.ShapeDtypeStruct((b, n, c), F32)] * 2,
        scratch_shapes=[pltpu.VMEM((2, b, rows_scr, c), F32), pltpu.VMEM((2, b, rows_scr, c), F32),
                        pltpu.VMEM((2, b, SUBLANES, c), F32)],
        compiler_params=_cparams(("arbitrary",)),
        name="rglru",
    )(lx_ctx, lx_lat, lx_lat, lx_lat, lx_lat, lx_lat, lx_lat, cw8, cb, wgf, wgb, bgf, bgb, sp8)


def _odd_out_kernel(x_ref, m_ref, yc_ref, z_ref, x0_ref, hb_ref, hf_ref, hbk_ref, lg_ref, wh_ref, wr_ref, out_ref):
    z = z_ref[...]
    hy = x0_ref[...] * (yc_ref[...] + z * hb_ref[...])
    r = (hf_ref[...] + hbk_ref[...]) * _gelu_tanh(lg_ref[...])
    mix = _dot(hy.astype(BF16), wh_ref[...]) + _dot(r.astype(BF16), wr_ref[...])
    out_ref[...] = x_ref[...] + m_ref[...][5:6] * mix


def _odd_out(tok, m, yc, z, x0, hy_bias, hf, hb, lg, w_out, *, n_rows, tm, mod_row):
    d = tok.shape[1]
    c = yc.shape[1]
    wh, wr = w_out[:c].astype(BF16), w_out[c:].astype(BF16)
    hbias = hy_bias.reshape(1, c).astype(F32)
    full = lambda a: pl.BlockSpec(a.shape, lambda i: (0,) * a.ndim)
    row = lambda w: pl.BlockSpec((tm, w), lambda i: (i, 0))
    return pl.pallas_call(
        _odd_out_kernel,
        grid=(n_rows // tm,),
        in_specs=[row(d), pl.BlockSpec((None, N_MOD, d), lambda i: (mod_row(i), 0, 0)),
                  row(c), row(c), row(c), full(hbias), row(c), row(c), row(c), full(wh), full(wr)],
        out_specs=row(d),
        out_shape=jax.ShapeDtypeStruct((n_rows, d), F32),
        compiler_params=_cparams(("parallel",)),
        name="odd_out",
    )(tok, m, yc, z, x0, hbias, hf, hb, lg, wh, wr)


def kernel(x, c, ctx, c_ctx, mod_w, mod_b, norm_ffn1, norm_mix, norm_ffn2, ffn1_w_gate, ffn1_w_up, ffn1_w_down, ffn2_w_gate, ffn2_w_up, ffn2_w_down, ev_w_in, mla_q_norm, mla_w_uq, mla_kv_norm, mla_w_ukv, s5_lambda_re, s5_lambda_im, s5_log_step, s5_b_re, s5_b_im, s5_c_re, s5_c_im, s5_d, s5_w_glu, s5_b_glu, ev_w_out, od_w_in, hy_conv_w, hy_conv_b, hy_filt_w1, hy_filt_b1, hy_filt_w2, hy_filt_b2, hy_filt_w3, hy_sin_freq, hy_bias, lru_conv_w, lru_conv_b, lru_w_a, lru_b_a, lru_w_x, lru_b_x, lru_lambda, od_w_out, final_norm):
    b, n, d = x.shape
    nc = ctx.shape[1]
    depth = mod_w.shape[0]
    assert depth == 2, "the trunk is laid out for one even and one odd layer"
    n_lat_rows = b * n
    t = n_lat_rows + b * nc
    tm = _row_tile(n, b * nc)
    tpb = n // tm
    mod_row = lambda i: jnp.minimum(i // tpb, b)

    tok = jnp.concatenate([x.reshape(n_lat_rows, d), ctx.reshape(b * nc, d)], axis=0)
    cond8 = jnp.concatenate([c, c_ctx[None, :], jnp.zeros((SUBLANES - b - 1, d), F32)], axis=0)
    m = _adaln(cond8, mod_w, mod_b)
    bf = lambda w: w.astype(BF16)

    tok = _ffn(tok, m[0], 0, norm_ffn1[0], bf(ffn1_w_gate[0]), bf(ffn1_w_up[0]), bf(ffn1_w_down[0]),
               n_rows=t, tm=tm, mod_row=mod_row)
    ev_w = _even_in_weights(ev_w_in[0], mla_w_uq[0], mla_w_ukv[0])
    qt, k, vt, s_chunks = _even_in(tok, m[0], norm_mix[0], ev_w, mla_q_norm[0], mla_kv_norm[0],
                                   n=n, tm=tm, mod_row=mod_row, n_lat_rows=n_lat_rows)
    o = _attention(qt, k, vt, b=b, n=n, nc=nc)
    s5_ops = _s5_operators(s5_lambda_re[0], s5_lambda_im[0], s5_log_step[0], s5_b_re[0], s5_b_im[0],
                           s5_c_re[0], s5_c_im[0], s5_d[0])
    ys = _s5(s_chunks, s5_ops, b=b, n=n, nc=nc)
    tok = _even_out(tok, m[0], o, ys, s5_w_glu[0], s5_b_glu[0], ev_w_out[0], tm=tm, mod_row=mod_row)
    tok = _ffn(tok, m[0], 2, norm_ffn2[0], bf(ffn2_w_gate[0]), bf(ffn2_w_up[0]), bf(ffn2_w_down[0]),
               n_rows=t, tm=tm, mod_row=mod_row)

    tok = _ffn(tok, m[1], 0, norm_ffn1[1], bf(ffn1_w_gate[1]), bf(ffn1_w_up[1]), bf(ffn1_w_down[1]),
               n_rows=t, tm=tm, mod_row=mod_row)
    hy_w = hy_conv_w.shape[2]
    lru_w = lru_conv_w.shape[2]
    hy, lx, lg = _odd_in(tok, m[1], norm_mix[1], od_w_in[0], hy_w=hy_w, lru_w=lru_w, tm=tm, mod_row=mod_row)
    tt = _row_tile(n)
    z, x0 = _hyena_prep(hy[:n_lat_rows].reshape(b, n, hy_w), hy_conv_w[0], hy_conv_b[0], tt=tt)
    dft = _Dft(n)
    spec = _hyena_filter_spectrum(n, hy_filt_w1[0], hy_filt_b1[0], hy_filt_w2[0], hy_filt_b2[0], hy_filt_w3[0],
                                  hy_sin_freq[0], dft)
    yc = _hyena_conv(z, spec, dft)
    hf, hb = _lru(lx[:n_lat_rows].reshape(b, n, lru_w), lx[n_lat_rows:].reshape(b, nc, lru_w), lru_conv_w[0],
                  lru_conv_b[0], lru_w_a[0], lru_b_a[0], lru_w_x[0], lru_b_x[0], lru_lambda[0], tt=min(tt, 256))
    flat = lambda a: a.reshape(n_lat_rows, a.shape[2])
    lat = _odd_out(tok, m[1], flat(yc), flat(z), flat(x0), hy_bias[0], flat(hf), flat(hb), lg, od_w_out[0],
                   n_rows=n_lat_rows, tm=tm, mod_row=mod_row)
    lat = _ffn(lat, m[1], 2, norm_ffn2[1], bf(ffn2_w_gate[1]), bf(ffn2_w_up[1]), bf(ffn2_w_down[1]),
               n_rows=n_lat_rows, tm=tm, mod_row=mod_row, final_g=final_norm)
    return lat.reshape(b, n, d)
```

```python
import functools
import math

import numpy as np
import jax
import jax.numpy as jnp
from jax import lax
from jax.experimental import pallas as pl
from jax.experimental.pallas import tpu as pltpu

F32 = jnp.float32
BF16 = jnp.bfloat16

GRID_W = 64
N_MOD = 9
NORM_EPS = 1e-6
ROPE_BASE = 10000.0
MLA_HEADS = 8
QK_NOPE = 64
QK_ROPE = 32
V_DIM = 64
Q_RANK = 384
KV_RANK = 256
S5_GROUP = 16
S5_STATE = 64
HY_BANDS = 16
HY_DECAY_TARGET = 1e-2
HY_MIN_DECAY = math.log(HY_DECAY_TARGET) / 1.5
HY_MAX_DECAY = math.log(HY_DECAY_TARGET) / 0.3
LRU_BLOCKS = 8
LRU_C = 8.0

LANES = 128
SUBLANES = 8
VMEM_BYTES_V7X = 64 * 1024 * 1024
VMEM_LIMIT = VMEM_BYTES_V7X - 8 * 1024 * 1024

HEAD_LANES = LANES
ONES_LANE = V_DIM
ATT_V_ROWS = V_DIM + 16
ATT_Q_TILE = 2048
ATT_Q_LANES = 256
S5_CHUNK = 8
FFT_N2 = 128
DFT_UNROLL = 8
NEG_BIG = -1e30


def _cparams(sem):
    return pltpu.CompilerParams(dimension_semantics=sem, vmem_limit_bytes=VMEM_LIMIT)


def _dot(a, b):
    return jnp.dot(a, b, preferred_element_type=F32)


def _split(x):
    hi = x.astype(BF16)
    lo = (x - hi.astype(F32)).astype(BF16)
    return hi, lo


def _dot3(a_hi, a_lo, b_hi, b_lo):
    return _dot(a_hi, b_hi) + _dot(a_lo, b_hi) + _dot(a_hi, b_lo)


def _sigmoid(x):
    return 1.0 / (1.0 + jnp.exp(-x))


def _gelu_tanh(x):
    return 0.5 * x * (1.0 + jnp.tanh(math.sqrt(2.0 / math.pi) * (x + 0.044715 * (x * x * x))))


def _rms(x, g):
    return x * lax.rsqrt(jnp.mean(x * x, axis=-1, keepdims=True) + NORM_EPS) * g


def _row_tile(*lengths):
    for t in (512, 256, 128, 64, 32, 16, 8):
        if all(l % t == 0 for l in lengths):
            return t
    raise ValueError(f"no row tile divides {lengths}")


def _adaln_kernel(c_ref, w_ref, b_ref, o_ref):
    c = c_ref[...]
    s_hi, s_lo = _split(c * _sigmoid(c))
    w_hi, w_lo = _split(w_ref[...])
    o_ref[...] = _dot3(s_hi, s_lo, w_hi, w_lo) + b_ref[...]


def _adaln(cond8, mod_w, mod_b):
    depth, d, nd = mod_w.shape
    tn = d
    out = pl.pallas_call(
        _adaln_kernel,
        grid=(depth, nd // tn),
        in_specs=[
            pl.BlockSpec((SUBLANES, d), lambda l, j: (0, 0)),
            pl.BlockSpec((None, d, tn), lambda l, j: (l, 0, j)),
            pl.BlockSpec((None, 1, tn), lambda l, j: (l, 0, j)),
        ],
        out_specs=pl.BlockSpec((None, SUBLANES, tn), lambda l, j: (l, 0, j)),
        out_shape=jax.ShapeDtypeStruct((depth, SUBLANES, nd), F32),
        compiler_params=_cparams(("arbitrary", "arbitrary")),
        name="adaln",
    )(cond8, mod_w, mod_b.reshape(depth, 1, nd))
    return out.reshape(depth, SUBLANES, N_MOD, d)


def _ffn_kernel(x_ref, m_ref, g_ref, wg_ref, wu_ref, wd_ref, *rest, k, tf, final, tiles_first):
    rest = list(rest)
    x = x_ref[...]
    if tiles_first is not None:
        x = jnp.where(pl.program_id(0) < tiles_first, x, rest.pop(0)[...])
    if final:
        fn_ref, o_ref = rest
    else:
        (o_ref,) = rest
    m = m_ref[...]
    shift, scale, gate = m[3 * k:3 * k + 1], m[3 * k + 1:3 * k + 2], m[3 * k + 2:3 * k + 3]
    hb = (_rms(x, g_ref[...]) * (1.0 + scale) + shift).astype(BF16)
    acc = jnp.zeros(x.shape, F32)
    for f in range(wg_ref.shape[1] // tf):
        sl = slice(f * tf, (f + 1) * tf)
        gt = _dot(hb, wg_ref[:, sl])
        up = _dot(hb, wu_ref[:, sl])
        acc = acc + _dot((gt * _sigmoid(gt) * up).astype(BF16), wd_ref[sl, :])
    y = x + (0.5 * gate) * acc
    if final:
        y = _rms(y, fn_ref[...])
    o_ref[...] = y


def _ffn(tok, m, k, g, wg, wu, wd, *, n_rows, tm, mod_row, final_g=None, tok_tail=None):
    d = tok.shape[1]
    f = wg.shape[1]
    tf = 256 if f % 256 == 0 else LANES
    final = final_g is not None
    tiles_first = None if tok_tail is None else tok.shape[0] // tm
    full = lambda shape: pl.BlockSpec(shape, lambda i: (0,) * len(shape), pipeline_mode=pl.Buffered(1))
    in_specs = [
        pl.BlockSpec((tm, d), lambda i: (i, 0) if tiles_first is None else (jnp.minimum(i, tiles_first - 1), 0)),
        pl.BlockSpec((None, N_MOD, d), lambda i: (mod_row(i), 0, 0)),
        full((1, d)), full((d, f)), full((d, f)), full((f, d)),
    ]
    args = [tok, m, g.reshape(1, d), wg, wu, wd]
    if tok_tail is not None:
        in_specs.append(pl.BlockSpec((tm, d), lambda i: (jnp.maximum(i - tiles_first, 0), 0)))
        args.append(tok_tail)
    if final:
        in_specs.append(full((1, d)))
        args.append(final_g.reshape(1, d))
    return pl.pallas_call(
        functools.partial(_ffn_kernel, k=k, tf=tf, final=final, tiles_first=tiles_first),
        grid=(n_rows // tm,),
        in_specs=in_specs,
        out_specs=pl.BlockSpec((tm, d), lambda i: (i, 0)),
        out_shape=jax.ShapeDtypeStruct((n_rows, d), F32),
        compiler_params=_cparams(("parallel",)),
        name="ffn_final" if final else "ffn",
    )(*args)


def _even_in_kernel(x_ref, m_ref, g_ref, w1_ref, qn_ref, kvn_ref, wq_ref, wqp_ref, wk_ref, wv_ref,
                    c_ref, s_ref, qt_out, k_out, vt_out, u_out, s_scr):
    x = x_ref[...]
    m = m_ref[...]
    hb = (_rms(x, g_ref[...]) * (1.0 + m[4:5]) + m[3:4]).astype(BF16)
    p = _dot(hb, w1_ref[...])
    o = 0
    cq = p[:, o:o + Q_RANK]; o += Q_RANK
    ckv = p[:, o:o + KV_RANK]; o += KV_RANK
    kra = p[:, o:o + HEAD_LANES]; o += HEAD_LANES
    krb = p[:, o:o + HEAD_LANES]; o += HEAD_LANES
    pw = 2 * S5_GROUP
    per_tile = LANES // pw
    chunks = s_scr.shape[1] // S5_CHUNK
    for lt in range(s_scr.shape[0]):
        s_scr[lt] = p[:, o + lt * LANES:o + (lt + 1) * LANES]
        by_tok = [s_scr[lt, pl.ds(t, chunks, stride=S5_CHUNK), :] for t in range(S5_CHUNK)]
        for l in range(per_tile):
            u_out[lt * per_tile + l] = jnp.concatenate([r[:, l * pw:(l + 1) * pw] for r in by_tok], axis=-1).astype(BF16)
    cos, sin = c_ref[...], s_ref[...]
    cqn = _rms(cq, qn_ref[...]).astype(BF16)
    qa = _dot(cqn, wq_ref[...])
    qb = _dot(cqn, wqp_ref[...])
    ckvn = _rms(ckv, kvn_ref[...]).astype(BF16)
    kn = _dot(ckvn, wk_ref[...])
    vv = _dot(ckvn, wv_ref[...])
    kr = kra * cos + krb * sin
    ones = (lax.broadcasted_iota(jnp.int32, (1, HEAD_LANES), 1) == ONES_LANE).astype(F32)
    scale = (QK_NOPE + QK_ROPE) ** -0.5 * math.log2(math.e)
    for h in range(MLA_HEADS):
        sl = slice(h * HEAD_LANES, (h + 1) * HEAD_LANES)
        qt_out[h] = ((qa[:, sl] * cos + qb[:, sl] * sin) * scale).T.astype(BF16)
        k_out[h] = (kn[:, sl] + kr).astype(BF16)
        vt_out[h] = (vv[:, sl] + ones).T.astype(BF16)


def _rope_partner(w):
    half = w.shape[-1] // 2
    return jnp.concatenate([-w[..., half:], w[..., :half]], axis=-1)


def _even_in_weights(w_in, w_uq, w_ukv):
    d = w_in.shape[0]
    o = 0
    w_cq = w_in[:, o:o + Q_RANK]; o += Q_RANK
    w_ckv = w_in[:, o:o + KV_RANK]; o += KV_RANK
    w_kr = w_in[:, o:o + QK_ROPE]; o += QK_ROPE
    w_s = w_in[:, o:]
    pad_l = jnp.zeros((d, QK_NOPE), F32)
    pad_r = jnp.zeros((d, HEAD_LANES - QK_NOPE - QK_ROPE), F32)
    kr_blk = jnp.concatenate([pad_l, w_kr, pad_r], axis=1)
    kr_blk_p = jnp.concatenate([pad_l, _rope_partner(w_kr), pad_r], axis=1)
    w1 = jnp.concatenate([w_cq, w_ckv, kr_blk, kr_blk_p, w_s], axis=1).astype(BF16)

    dk = QK_NOPE + QK_ROPE
    wq = w_uq.reshape(Q_RANK, MLA_HEADS, dk)
    zq = jnp.zeros((Q_RANK, MLA_HEADS, HEAD_LANES - dk), F32)
    wq_pad = jnp.concatenate([wq, zq], axis=-1)
    wq_par = jnp.concatenate([jnp.zeros((Q_RANK, MLA_HEADS, QK_NOPE), F32), _rope_partner(wq[..., QK_NOPE:]), zq], axis=-1)
    wkv = w_ukv.reshape(KV_RANK, MLA_HEADS, QK_NOPE + V_DIM)
    wk = jnp.concatenate([wkv[..., :QK_NOPE], jnp.zeros((KV_RANK, MLA_HEADS, HEAD_LANES - QK_NOPE), F32)], axis=-1)
    wv = jnp.concatenate([wkv[..., QK_NOPE:], jnp.zeros((KV_RANK, MLA_HEADS, HEAD_LANES - V_DIM), F32)], axis=-1)
    flat = lambda w: w.reshape(w.shape[0], MLA_HEADS * HEAD_LANES).astype(BF16)
    return w1, flat(wq_pad), flat(wq_par), flat(wk), flat(wv)


def _rope_tables(n, n_ident):
    rows = n // GRID_W
    row = np.repeat(np.arange(rows, dtype=np.float32), GRID_W)
    col = np.tile(np.arange(GRID_W, dtype=np.float32), rows)
    n_freq = QK_ROPE // 4
    inv_freq = (np.float32(ROPE_BASE) ** (-np.arange(n_freq, dtype=np.float32) / n_freq)).astype(np.float32)
    ang = np.concatenate([row[:, None] * inv_freq, col[:, None] * inv_freq], axis=-1).astype(np.float64)
    ang = np.concatenate([ang, np.zeros((n_ident, QK_ROPE // 2))], axis=0)
    tot = n + n_ident
    pad = HEAD_LANES - QK_NOPE - QK_ROPE
    cos = np.concatenate([np.ones((tot, QK_NOPE)), np.cos(ang), np.cos(ang), np.ones((tot, pad))], axis=1)
    sin = np.concatenate([np.zeros((tot, QK_NOPE)), np.sin(ang), np.sin(ang), np.zeros((tot, pad))], axis=1)
    return jnp.asarray(cos, F32), jnp.asarray(sin, F32)


def _even_in(tok, m, g, weights, q_norm, kv_norm, *, n, tm, mod_row, n_lat_rows):
    t, d = tok.shape
    w1, wq, wqp, wk, wv = weights
    cos, sin = _rope_tables(n, tm)
    tiles_lat = n_lat_rows // tm
    tpb = n // tm
    tab_row = lambda i: jnp.where(i < tiles_lat, i % tpb, tpb)
    full = lambda a: pl.BlockSpec(a.shape, lambda i: (0,) * a.ndim)
    hl = MLA_HEADS * HEAD_LANES
    s_ch = w1.shape[1] - Q_RANK - KV_RANK - 2 * HEAD_LANES
    qn, kvn = q_norm.reshape(1, Q_RANK), kv_norm.reshape(1, KV_RANK)
    gg = g.reshape(1, d)
    head_spec = pl.BlockSpec((MLA_HEADS, tm, HEAD_LANES), lambda i: (0, i, 0))
    head_shape = jax.ShapeDtypeStruct((MLA_HEADS, t, HEAD_LANES), BF16)
    head_t_spec = pl.BlockSpec((MLA_HEADS, HEAD_LANES, tm), lambda i: (0, 0, i))
    head_t_shape = jax.ShapeDtypeStruct((MLA_HEADS, HEAD_LANES, t), BF16)
    pairs = s_ch // (2 * S5_GROUP)
    u_w = 2 * S5_GROUP * S5_CHUNK
    return pl.pallas_call(
        _even_in_kernel,
        grid=(t // tm,),
        in_specs=[
            pl.BlockSpec((tm, d), lambda i: (i, 0)),
            pl.BlockSpec((None, N_MOD, d), lambda i: (mod_row(i), 0, 0)),
            full(gg), full(w1), full(qn), full(kvn), full(wq), full(wqp), full(wk), full(wv),
            pl.BlockSpec((tm, HEAD_LANES), lambda i: (tab_row(i), 0)),
            pl.BlockSpec((tm, HEAD_LANES), lambda i: (tab_row(i), 0)),
        ],
        out_specs=[head_t_spec, head_spec, head_t_spec,
                   pl.BlockSpec((pairs, tm // S5_CHUNK, u_w), lambda i: (0, i, 0))],
        out_shape=[head_t_shape, head_shape, head_t_shape,
                   jax.ShapeDtypeStruct((pairs, t // S5_CHUNK, u_w), BF16)],
        scratch_shapes=[pltpu.VMEM((s_ch // LANES, tm, LANES), F32)],
        compiler_params=_cparams(("parallel",)),
        name="even_in",
    )(tok, m, gg, w1, qn, kvn, wq, wqp, wk, wv, cos, sin)


def _attn_kernel(qt_ref, kc_ref, vtc_ref, *rest, tk, lat):
    if lat:
        kl_ref, vtl_ref, o_ref = rest
    else:
        _, o_ref = rest
    tq = qt_ref.shape[1]
    lanes = min(ATT_Q_LANES, tq)
    subs = [slice(j * lanes, (j + 1) * lanes) for j in range(tq // lanes)]

    def step(k, vt, carry):
        scores = [_dot(k, qt_ref[:, sl]) for sl in subs]
        out = []
        for s, (m, acc) in zip(scores, carry):
            m_new = jnp.maximum(m, jnp.max(s, axis=0, keepdims=True))
            p = jnp.exp2(s - m_new).astype(BF16)
            out.append((m_new, jnp.exp2(m - m_new) * acc + _dot(vt, p)))
        return tuple(out)

    carry = tuple((jnp.full((1, lanes), NEG_BIG, F32), jnp.zeros((ATT_V_ROWS, lanes), F32)) for _ in subs)
    carry = step(kc_ref[...], vtc_ref[:ATT_V_ROWS, :], carry)
    if lat:
        def body(c, carry):
            start = pl.multiple_of(c * tk, tk)
            return step(kl_ref[pl.ds(start, tk), :], vtl_ref[:ATT_V_ROWS, pl.ds(start, tk)], carry)

        n_chunks = kl_ref.shape[0] // tk
        carry = lax.fori_loop(0, n_chunks, body, carry, unroll=2 if n_chunks % 2 == 0 else 1)
    for sl, (_, acc) in zip(subs, carry):
        o = acc / acc[ONES_LANE:ONES_LANE + 1]
        o = jnp.concatenate([o, jnp.zeros((HEAD_LANES - ATT_V_ROWS, lanes), F32)], axis=0)
        o_ref[sl, :] = o.T.astype(BF16)


def _attention(qt, k, vt, *, b, n, nc):
    h, t, _ = k.shape
    tq = min(ATT_Q_TILE, n)
    n_q = n // tq
    tk = _row_tile(n)
    lat_blocks = (b * n) // nc
    out_shape = jax.ShapeDtypeStruct((h, t, HEAD_LANES), BF16)
    o = pl.pallas_call(
        functools.partial(_attn_kernel, tk=tk, lat=True),
        grid=(b, h, n_q),
        in_specs=[
            pl.BlockSpec((None, HEAD_LANES, tq), lambda bi, hi, i: (hi, 0, bi * n_q + i)),
            pl.BlockSpec((None, nc, HEAD_LANES), lambda bi, hi, i: (hi, lat_blocks + bi, 0)),
            pl.BlockSpec((None, HEAD_LANES, nc), lambda bi, hi, i: (hi, 0, lat_blocks + bi)),
            pl.BlockSpec((None, n, HEAD_LANES), lambda bi, hi, i: (hi, bi, 0)),
            pl.BlockSpec((None, HEAD_LANES, n), lambda bi, hi, i: (hi, 0, bi)),
        ],
        out_specs=pl.BlockSpec((None, tq, HEAD_LANES), lambda bi, hi, i: (hi, bi * n_q + i, 0)),
        out_shape=out_shape,
        compiler_params=_cparams(("parallel", "parallel", "arbitrary")),
        name="attention",
    )(qt, k, vt, k, vt)
    return pl.pallas_call(
        functools.partial(_attn_kernel, tk=tk, lat=False),
        grid=(b, h),
        in_specs=[
            pl.BlockSpec((None, HEAD_LANES, nc), lambda bi, hi: (hi, 0, lat_blocks + bi)),
            pl.BlockSpec((None, nc, HEAD_LANES), lambda bi, hi: (hi, lat_blocks + bi, 0)),
            pl.BlockSpec((None, HEAD_LANES, nc), lambda bi, hi: (hi, 0, lat_blocks + bi)),
            pl.BlockSpec(memory_space=pl.ANY),
        ],
        out_specs=pl.BlockSpec((None, nc, HEAD_LANES), lambda bi, hi: (hi, lat_blocks + bi, 0)),
        out_shape=out_shape,
        input_output_aliases={3: 0},
        compiler_params=_cparams(("parallel", "parallel")),
        name="attention_ctx",
    )(qt, k, vt, o)


def _s5_operators(lam_re, lam_im, log_step, b_re, b_im, c_re, c_im, d_skip):
    tc = S5_CHUNK
    g_n, p_n = lam_re.shape[1], lam_re.shape[2]
    lam = lax.complex(lam_re.astype(F32), lam_im.astype(F32))
    step = jnp.exp(log_step.astype(F32))[..., None]
    la = lam * step
    a_bar = jnp.exp(la)
    bb = ((a_bar - 1.0) / lam)[..., None] * lax.complex(b_re.astype(F32), b_im.astype(F32))
    cm = lax.complex(c_re.astype(F32), c_im.astype(F32))
    kk = jnp.arange(tc + 1, dtype=F32)
    apow = jnp.exp(la[..., None] * kk)

    win_f = jnp.einsum('gps,gpc->gscp', apow[0][..., tc - 1::-1][..., :tc], bb[0])
    win_b = jnp.einsum('gps,gpc->gscp', apow[1][..., :tc], bb[1])
    wout_f = jnp.einsum('gcp,gpt->gptc', cm[0], apow[0][..., 1:])
    wout_b = jnp.einsum('gcp,gpt->gptc', cm[1], apow[1][..., tc:0:-1])
    kf = jnp.real(jnp.einsum('gop,gpk,gpi->gkoi', cm[0], apow[0][..., :tc], bb[0]))
    kb = jnp.real(jnp.einsum('gop,gpk,gpi->gkoi', cm[1], apow[1][..., :tc], bb[1]))
    s_idx = jnp.arange(tc)[:, None]
    t_idx = jnp.arange(tc)[None, :]
    lag = t_idx - s_idx
    kf_st = jnp.where((lag >= 0)[None, :, :, None, None], kf[:, jnp.clip(lag, 0, tc - 1)], 0.0)
    kb_st = jnp.where((lag <= 0)[None, :, :, None, None], kb[:, jnp.clip(-lag, 0, tc - 1)], 0.0)
    eye_t = jnp.eye(tc, dtype=F32)[None, :, :, None, None]
    eye_c = jnp.eye(S5_GROUP, dtype=F32)[None, None, None]
    dsk = d_skip.astype(F32).reshape(g_n, 1, 1, S5_GROUP, 1)
    loc = kf_st + kb_st + eye_t * eye_c * dsk
    loc = jnp.transpose(loc, (0, 1, 4, 2, 3))

    gp = g_n // 2
    gc2 = 2 * tc * S5_GROUP
    eye2 = jnp.eye(2, dtype=F32)
    pair = lambda x: x.reshape((gp, 2) + x.shape[1:])
    w_in_part = lambda w: jnp.einsum('qgscp,gh->qsgchp', pair(w), eye2).reshape(gp, gc2, 2 * p_n)
    w_out_part = lambda w: jnp.einsum('qgptc,gh->qgpthc', pair(w), eye2).reshape(gp, 2 * p_n, gc2)
    w_in = jnp.concatenate([w_in_part(jnp.real(win_f)), w_in_part(jnp.imag(win_f)),
                            w_in_part(jnp.real(win_b)), w_in_part(jnp.imag(win_b))], axis=2)
    w_out = jnp.concatenate([w_out_part(jnp.real(wout_f)), w_out_part(-jnp.imag(wout_f)),
                             w_out_part(jnp.real(wout_b)), w_out_part(-jnp.imag(wout_b))], axis=1)
    loc = jnp.einsum('qgsito,gh->qsgitho', pair(loc), eye2).reshape(gp, gc2, gc2)
    a_tc = apow[..., tc].reshape(2, gp, 2 * p_n)
    a_rows = jnp.stack([jnp.real(a_tc[0]), jnp.imag(a_tc[0]), jnp.real(a_tc[1]), jnp.imag(a_tc[1])], axis=1)
    a_rows = jnp.concatenate([a_rows, jnp.zeros((gp, SUBLANES - 4, 2 * p_n), F32)], axis=1)
    return w_in.astype(BF16), w_out.astype(BF16), loc.astype(BF16), a_rows


def _s5_kernel(u_ref, win_ref, wout_ref, loc_ref, a_ref, y_ref, s_scr, h_scr, *, batch, lat_chunks, ctx_chunks):
    u = u_ref[...]
    w = a_ref.shape[1]
    n_parts = s_scr.shape[0]
    s_all = _dot(u, win_ref[...])
    for k in range(n_parts):
        s_scr[k] = s_all[:, k * w:(k + 1) * w]
    a = a_ref[...]
    bc = lambda r: jnp.broadcast_to(a[r:r + 1], (batch, w))
    arf, aif, arb, aib = bc(0), bc(1), bc(2), bc(3)

    def chunk(rows, carry, part, ar, ai):
        hr, hi = carry
        sr = s_scr[part, rows, :]
        si = s_scr[part + 1, rows, :]
        h_scr[part, rows, :] = hr
        h_scr[part + 1, rows, :] = hi
        return ar * hr - ai * hi + sr, ar * hi + ai * hr + si

    def segment(base, per):
        def body(j, carry):
            cf, cb = carry
            cf = chunk(pl.ds(base + j, batch, stride=per), cf, 0, arf, aif)
            cb = chunk(pl.ds(base + per - 1 - j, batch, stride=per), cb, 2, arb, aib)
            return cf, cb
        return body

    zero = (jnp.zeros((batch, w), F32), jnp.zeros((batch, w), F32))
    carry = lax.fori_loop(0, ctx_chunks, segment(batch * lat_chunks, ctx_chunks), (zero, zero))
    lax.fori_loop(0, lat_chunks, segment(0, lat_chunks), carry)
    h_all = jnp.concatenate([h_scr[k] for k in range(n_parts)], axis=-1).astype(BF16)
    y_ref[...] = _dot(u, loc_ref[...]) + _dot(h_all, wout_ref[...])


def _s5(u, ops, *, b, n, nc):
    w_in, w_out, loc, a_rows = ops
    gp, r, wid = u.shape
    sw = w_in.shape[2]
    return pl.pallas_call(
        functools.partial(_s5_kernel, batch=b, lat_chunks=n // S5_CHUNK, ctx_chunks=nc // S5_CHUNK),
        grid=(gp,),
        in_specs=[
            pl.BlockSpec((None, r, wid), lambda g: (g, 0, 0)),
            pl.BlockSpec((None, wid, sw), lambda g: (g, 0, 0)),
            pl.BlockSpec((None, sw, wid), lambda g: (g, 0, 0)),
            pl.BlockSpec((None, wid, wid), lambda g: (g, 0, 0)),
            pl.BlockSpec((None, SUBLANES, sw // 4), lambda g: (g, 0, 0)),
        ],
        out_specs=pl.BlockSpec((None, r, wid), lambda g: (g, 0, 0)),
        out_shape=jax.ShapeDtypeStruct((gp, r, wid), F32),
        scratch_shapes=[pltpu.VMEM((4, r, sw // 4), F32), pltpu.VMEM((4, r, sw // 4), F32)],
        compiler_params=_cparams(("parallel",)),
        name="s5",
    )(u, w_in, w_out, loc, a_rows)


def _even_out_kernel(x_ref, m_ref, o_ref, yc_ref, wglu_ref, bglu_ref, woo_ref, wos_ref, out_ref, ys_scr):
    pw = 2 * S5_GROUP
    per_tile = LANES // pw
    chunks = yc_ref.shape[1]
    for lt in range(ys_scr.shape[0]):
        by_pair = [yc_ref[lt * per_tile + l] for l in range(per_tile)]
        for t in range(S5_CHUNK):
            ys_scr[lt, pl.ds(t, chunks, stride=S5_CHUNK), :] = jnp.concatenate(
                [y[:, t * pw:(t + 1) * pw] for y in by_pair], axis=-1)
    y = _gelu_tanh(jnp.concatenate([ys_scr[lt] for lt in range(ys_scr.shape[0])], axis=-1))
    y = y * _sigmoid(_dot(y.astype(BF16), wglu_ref[...]) + bglu_ref[...])
    oc = jnp.concatenate([o_ref[h] for h in range(MLA_HEADS)], axis=-1)
    mix = _dot(oc, woo_ref[...]) + _dot(y.astype(BF16), wos_ref[...])
    out_ref[...] = x_ref[...] + m_ref[...][5:6] * mix


def _even_out(tok, m, o, yc, w_glu, b_glu, w_out, *, tm, mod_row):
    t, d = tok.shape
    pairs, _, wid = yc.shape
    sc = pairs * 2 * S5_GROUP
    wo = w_out[:MLA_HEADS * V_DIM].reshape(MLA_HEADS, V_DIM, d)
    woo = jnp.concatenate([wo, jnp.zeros((MLA_HEADS, HEAD_LANES - V_DIM, d), F32)], axis=1)
    woo = woo.reshape(MLA_HEADS * HEAD_LANES, d).astype(BF16)
    wos = w_out[MLA_HEADS * V_DIM:].astype(BF16)
    wg = w_glu.astype(BF16)
    bg = b_glu.reshape(1, sc).astype(F32)
    full = lambda a: pl.BlockSpec(a.shape, lambda i: (0,) * a.ndim)
    return pl.pallas_call(
        _even_out_kernel,
        grid=(t // tm,),
        in_specs=[
            pl.BlockSpec((tm, d), lambda i: (i, 0)),
            pl.BlockSpec((None, N_MOD, d), lambda i: (mod_row(i), 0, 0)),
            pl.BlockSpec((MLA_HEADS, tm, HEAD_LANES), lambda i: (0, i, 0)),
            pl.BlockSpec((pairs, tm // S5_CHUNK, wid), lambda i: (0, i, 0)),
            full(wg), full(bg), full(woo), full(wos),
        ],
        out_specs=pl.BlockSpec((tm, d), lambda i: (i, 0)),
        out_shape=jax.ShapeDtypeStruct((t, d), F32),
        scratch_shapes=[pltpu.VMEM((sc // LANES, tm, LANES), F32)],
        compiler_params=_cparams(("parallel",)),
        name="even_out",
    )(tok, m, o, yc, wg, bg, woo, wos)


def _odd_in_kernel(x_ref, m_ref, g_ref, w_ref, hy_out, lx_out, lg_out, lxc_out, *, tiles_lat):
    m = m_ref[...]
    hb = (_rms(x_ref[...], g_ref[...]) * (1.0 + m[4:5]) + m[3:4]).astype(BF16)
    n_hy, n_lx = hy_out.shape[1], lx_out.shape[1]
    is_lat = pl.program_id(0) < tiles_lat

    @pl.when(is_lat)
    def _():
        p = _dot(hb, w_ref[...])
        hy_out[...] = p[:, :n_hy]
        lx_out[...] = p[:, n_hy:n_hy + n_lx]
        lg_out[...] = p[:, n_hy + n_lx:]

    @pl.when(jnp.logical_not(is_lat))
    def _():
        lxc_out[...] = _dot(hb, w_ref[:, n_hy:n_hy + n_lx])


def _odd_in(tok, m, g, w_in, *, hy_w, lru_w, tm, mod_row, n_lat_rows):
    t, d = tok.shape
    w = w_in.astype(BF16)
    gg = g.reshape(1, d)
    tiles_lat = n_lat_rows // tm
    full = lambda a: pl.BlockSpec(a.shape, lambda i: (0,) * a.ndim)
    lat_row = lambda c: pl.BlockSpec((tm, c), lambda i: (jnp.minimum(i, tiles_lat - 1), 0))
    ctx_row = lambda c: pl.BlockSpec((tm, c), lambda i: (jnp.maximum(i - tiles_lat, 0), 0))
    return pl.pallas_call(
        functools.partial(_odd_in_kernel, tiles_lat=tiles_lat),
        grid=(t // tm,),
        in_specs=[pl.BlockSpec((tm, d), lambda i: (i, 0)),
                  pl.BlockSpec((None, N_MOD, d), lambda i: (mod_row(i), 0, 0)), full(gg), full(w)],
        out_specs=[lat_row(hy_w), lat_row(lru_w), lat_row(lru_w), ctx_row(lru_w)],
        out_shape=[jax.ShapeDtypeStruct((n_lat_rows, c), F32) for c in (hy_w, lru_w, lru_w)]
        + [jax.ShapeDtypeStruct((t - n_lat_rows, lru_w), F32)],
        compiler_params=_cparams(("arbitrary",)),
        name="odd_in",
    )(tok, m, gg, w)


def _shift_down(cur, prev8, k):
    r = pltpu.roll(cur, k, 0)
    row = lax.broadcasted_iota(jnp.int32, prev8.shape, 0)
    head = jnp.where(row < k, pltpu.roll(prev8, k, 0), r[:SUBLANES])
    return jnp.concatenate([head, r[SUBLANES:]], axis=0)


def _shift_up(cur, next8, k):
    rows = cur.shape[0]
    r = pltpu.roll(cur, rows - k, 0)
    row = lax.broadcasted_iota(jnp.int32, next8.shape, 0)
    tail = jnp.where(row >= SUBLANES - k, pltpu.roll(next8, SUBLANES - k, 0), r[rows - SUBLANES:])
    return jnp.concatenate([r[:rows - SUBLANES], tail], axis=0)


def _halo_specs(tt, width, n_tiles, tile_of):
    per = tt // SUBLANES
    last = n_tiles * per - 1
    cur = pl.BlockSpec((None, tt, width), lambda b, i: (b, tile_of(i), 0))
    prev = pl.BlockSpec((None, SUBLANES, width), lambda b, i: (b, jnp.maximum(tile_of(i) * per - 1, 0), 0))
    nxt = pl.BlockSpec((None, SUBLANES, width), lambda b, i: (b, jnp.minimum((tile_of(i) + 1) * per, last), 0))
    return cur, prev, nxt


def _hyena_prep_kernel(cur_ref, prev_ref, next_ref, w_ref, b_ref, z_out, x0_out):
    i = pl.program_id(1)
    cur = cur_ref[...]
    prev8 = prev_ref[...] * (i > 0).astype(F32)
    next8 = next_ref[...] * (i < pl.num_programs(1) - 1).astype(F32)
    w = w_ref[...]
    u = w[0:1] * _shift_down(cur, prev8, 1) + w[1:2] * cur + w[2:3] * _shift_up(cur, next8, 1) + b_ref[...]
    c = z_out.shape[1]
    x0_out[...] = u[:, :c]
    z_out[...] = u[:, 2 * c:] * u[:, c:2 * c]


def _hyena_prep(hy, conv_w, conv_b, *, tt):
    b, n, c3 = hy.shape
    c = c3 // 3
    cur, prev, nxt = _halo_specs(tt, c3, n // tt, lambda i: i)
    w8 = jnp.concatenate([conv_w.astype(F32), jnp.zeros((SUBLANES - conv_w.shape[0], c3), F32)], axis=0)
    out_spec = pl.BlockSpec((None, tt, c), lambda bi, i: (bi, i, 0))
    return pl.pallas_call(
        _hyena_prep_kernel,
        grid=(b, n // tt),
        in_specs=[cur, prev, nxt, pl.BlockSpec((SUBLANES, c3), lambda bi, i: (0, 0)),
                  pl.BlockSpec((1, c3), lambda bi, i: (0, 0))],
        out_specs=[out_spec, out_spec],
        out_shape=[jax.ShapeDtypeStruct((b, n, c), F32)] * 2,
        compiler_params=_cparams(("parallel", "parallel")),
        name="hyena_prep",
    )(hy, hy, hy, w8, conv_b.reshape(1, c3).astype(F32))


class _Dft:
    def __init__(self, n):
        self.n = n
        self.N = 2 * n
        self.N2 = FFT_N2
        self.N1 = self.N // self.N2
        self.A = n // self.N2
        self.K1 = self.N1 // 2 + 1
        self.K1p = -(-self.K1 // SUBLANES) * SUBLANES
        N, N1, N2, A, K1, K1p = self.N, self.N1, self.N2, self.A, self.K1, self.K1p
        b = np.arange(N2)[:, None, None]
        k1 = np.arange(K1p)[None, :, None]
        a = np.arange(A)[None, None, :]
        phi = 2.0 * np.pi * ((a * k1 % N1) / N1 + (b * k1 % N) / N)
        live = (k1 < K1).astype(np.float64)
        f1 = np.concatenate([np.cos(phi) * live, -np.sin(phi) * live], axis=1)
        wgt = np.where((k1 == 0) | (k1 == N1 // 2), 1.0, 2.0) * live / N
        g1 = np.concatenate([np.cos(phi) * wgt, -np.sin(phi) * wgt], axis=1)
        g1 = np.transpose(g1, (0, 2, 1))
        ang = 2.0 * np.pi * (np.arange(N2)[:, None] * np.arange(N2)[None, :] % N2) / N2
        c, s = np.cos(ang), np.sin(ang)
        f2 = np.block([[c, s], [-s, c]])
        g2 = np.block([[c, -s], [s, c]])
        self.f1, self.f2, self.g2, self.g1 = (jnp.asarray(x, F32).astype(BF16) for x in (f1, f2, g2, g1))


def _dft_forward(src_ref, f1_ref, f2_ref, y_scr, put, dft):
    n2, a_n, k1p2 = dft.N2, dft.A, 2 * dft.K1p

    def stage1(b, _):
        xb = src_ref[pl.ds(b, a_n, stride=n2), :]
        r0 = pl.multiple_of(b * k1p2, SUBLANES)
        y_scr[pl.ds(r0, k1p2), :] = _dot(f1_ref[b], xb.astype(BF16))
        return 0

    lax.fori_loop(0, n2, stage1, 0, unroll=DFT_UNROLL)

    def stage2(k1, _):
        yr = y_scr[pl.ds(k1, n2, stride=k1p2), :]
        yi = y_scr[pl.ds(dft.K1p + k1, n2, stride=k1p2), :]
        put(k1, _dot(f2_ref[...], jnp.concatenate([yr, yi], axis=0).astype(BF16)))
        return 0

    even_planes = dft.K1 - dft.K1 % DFT_UNROLL
    lax.fori_loop(0, even_planes, stage2, 0, unroll=DFT_UNROLL)
    for k1 in range(even_planes, dft.K1):
        stage2(k1, 0)


def _hyena_filter_kernel(feat_ref, w1_ref, b1_ref, w2_ref, b2_ref, w3f_ref, w3b_ref, freq_ref, delta_ref,
                         f1_ref, f2_ref, h_out, filt_scr, y_scr, hid_scr, *, dft):
    n = dft.n
    rb = _row_tile(n)
    blocks = n // rb
    rows_of = lambda r: pl.ds(pl.multiple_of(r * rb, rb), rb)

    @pl.when(pl.program_id(0) == 0)
    def _():
        freq = freq_ref[...]
        w1_hi, w1_lo = _split(w1_ref[...])
        w2_hi, w2_lo = _split(w2_ref[...])

        def hidden(r, _):
            f_hi, f_lo = _split(feat_ref[rows_of(r), :])
            h = jnp.sin(freq[0:1] * (_dot3(f_hi, f_lo, w1_hi, w1_lo) + b1_ref[...]))
            h_hi, h_lo = _split(h)
            hid_scr[rows_of(r), :] = jnp.sin(freq[1:2] * (_dot3(h_hi, h_lo, w2_hi, w2_lo) + b2_ref[...]))
            return 0

        lax.fori_loop(0, blocks, hidden, 0)

    def fill_filter(w3_ref):
        w_hi, w_lo = _split(w3_ref[...])

        def blk(r, _):
            h_hi, h_lo = _split(hid_scr[rows_of(r), :])
            decay = jnp.exp(-feat_ref[rows_of(r), 0:1] * delta_ref[...])
            filt_scr[rows_of(r), :] = _dot3(h_hi, h_lo, w_hi, w_lo) * decay
            return 0

        lax.fori_loop(0, blocks, blk, 0)

    n2 = dft.N2
    fill_filter(w3f_ref)

    def put_fwd(k1, x):
        h_out[k1] = x

    _dft_forward(filt_scr, f1_ref, f2_ref, y_scr, put_fwd, dft)

    fill_filter(w3b_ref)
    first = lax.broadcasted_iota(jnp.int32, (SUBLANES, LANES), 0) == 0
    filt_scr[:SUBLANES, :] = jnp.where(first, 0.0, filt_scr[:SUBLANES, :])
    sign = jnp.where(lax.broadcasted_iota(jnp.int32, (2 * n2, LANES), 0) < n2, 1.0, -1.0)

    def put_bwd(k1, x):
        h_out[k1] = h_out[k1] + sign * x

    _dft_forward(filt_scr, f1_ref, f2_ref, y_scr, put_bwd, dft)


def _hyena_features(n):
    t = np.linspace(0.0, 1.0, n, dtype=np.float32).astype(np.float64)[:, None]
    w = (2.0 * np.pi * np.arange(n, dtype=np.float64)[:, None] / n).astype(np.float32).astype(np.float64)
    bands = np.linspace(1e-4, HY_BANDS - 1, HY_BANDS, dtype=np.float32).astype(np.float64)[None, :]
    bw = (bands * w).astype(np.float32).astype(np.float64)
    z = np.concatenate([t, np.cos(bw), -np.sin(bw)], axis=-1)
    z = np.concatenate([z, np.zeros((n, LANES - z.shape[1]))], axis=-1)
    return jnp.asarray(z, F32)


def _hyena_filter_spectrum(n, w1, b1, w2, b2, w3, sin_freq, dft):
    ch = w3.shape[1] // 2
    hid = w1.shape[1]
    feat = _hyena_features(n)
    w1p = jnp.concatenate([w1.astype(F32), jnp.zeros((LANES - w1.shape[0], hid), F32)], axis=0)
    deltas = jnp.asarray(np.abs(np.linspace(HY_MIN_DECAY, HY_MAX_DECAY, ch, dtype=np.float32)), F32).reshape(1, ch)
    freq8 = jnp.concatenate([sin_freq.astype(F32), jnp.zeros((SUBLANES - 2, hid), F32)], axis=0)
    full = lambda a: pl.BlockSpec(a.shape, lambda c: (0,) * a.ndim, pipeline_mode=pl.Buffered(1))
    tiles = ch // LANES
    args = (feat, w1p, b1.reshape(1, hid).astype(F32), w2.astype(F32), b2.reshape(1, hid).astype(F32))
    return pl.pallas_call(
        functools.partial(_hyena_filter_kernel, dft=dft),
        grid=(tiles,),
        in_specs=[full(a) for a in args] + [
            pl.BlockSpec((hid, LANES), lambda c: (0, c)),
            pl.BlockSpec((hid, LANES), lambda c: (0, tiles + c)),
            full(freq8),
            pl.BlockSpec((1, LANES), lambda c: (0, c)),
            full(dft.f1), full(dft.f2),
        ],
        out_specs=pl.BlockSpec((dft.K1, 2 * dft.N2, LANES), lambda c: (0, 0, c)),
        out_shape=jax.ShapeDtypeStruct((dft.K1, 2 * dft.N2, ch), F32),
        scratch_shapes=[pltpu.VMEM((n, LANES), F32), pltpu.VMEM((dft.N2 * 2 * dft.K1p, LANES), F32),
                        pltpu.VMEM((n, hid), F32)],
        compiler_params=_cparams(("arbitrary",)),
        name="hyena_filter",
    )(*args, w3.astype(F32), w3.astype(F32), freq8, deltas, dft.f1, dft.f2)


def _hyena_conv_kernel(z_ref, h_ref, f1_ref, f2_ref, g2_ref, g1_ref, y_out, y_scr, x_scr, *, dft):
    n2, a_n, k1p = dft.N2, dft.A, dft.K1p

    def put(k1, x):
        hk = h_ref[k1]
        xr, xi, hr, hi = x[:n2], x[n2:], hk[:n2], hk[n2:]
        prod = jnp.concatenate([xr * hr - xi * hi, xr * hi + xi * hr], axis=0).astype(BF16)
        r0 = pl.multiple_of(k1 * 2 * n2, SUBLANES)
        x_scr[pl.ds(r0, 2 * n2), :] = _dot(g2_ref[...], prod)

    if k1p > dft.K1:
        x_scr[dft.K1 * 2 * n2:, :] = jnp.zeros(((k1p - dft.K1) * 2 * n2, LANES), F32)
    _dft_forward(z_ref, f1_ref, f2_ref, y_scr, put, dft)

    def last(b, _):
        yr = x_scr[pl.ds(b, k1p, stride=2 * n2), :]
        yi = x_scr[pl.ds(n2 + b, k1p, stride=2 * n2), :]
        y_out[pl.ds(b, a_n, stride=n2), :] = _dot(g1_ref[b], jnp.concatenate([yr, yi], axis=0).astype(BF16))
        return 0

    lax.fori_loop(0, n2, last, 0, unroll=DFT_UNROLL)


def _hyena_conv(z, spec, dft):
    b, n, ch = z.shape
    once = pl.Buffered(1)
    full = lambda a: pl.BlockSpec(a.shape, lambda c, bi: (0,) * a.ndim, pipeline_mode=once)
    io_spec = pl.BlockSpec((None, n, LANES), lambda c, bi: (bi, 0, c))
    return pl.pallas_call(
        functools.partial(_hyena_conv_kernel, dft=dft),
        grid=(ch // LANES, b),
        in_specs=[io_spec, pl.BlockSpec((dft.K1, 2 * dft.N2, LANES), lambda c, bi: (0, 0, c), pipeline_mode=once),
                  full(dft.f1), full(dft.f2), full(dft.g2), full(dft.g1)],
        out_specs=io_spec,
        out_shape=jax.ShapeDtypeStruct((b, n, ch), F32),
        scratch_shapes=[pltpu.VMEM((dft.N2 * 2 * dft.K1p, LANES), F32), pltpu.VMEM((dft.K1p * 2 * dft.N2, LANES), F32)],
        compiler_params=_cparams(("parallel", "arbitrary")),
        name="hyena_conv",
    )(z, spec, dft.f1, dft.f2, dft.g2, dft.g1)


def _lru_tile_terms(cur, prev8, next8, conv_w, conv_b, wg, bg, sp, reverse):
    rows, c = cur.shape
    xc = (conv_w[0:1] * _shift_down(cur, prev8, 2) + conv_w[1:2] * _shift_down(cur, prev8, 1)
          + conv_w[2:3] * cur + conv_w[3:4] * _shift_up(cur, next8, 1) + conv_b)
    g = _dot(xc.astype(BF16), wg) + bg
    r, ig = _sigmoid(g[:, :c]), _sigmoid(g[:, c:])
    log_a = -LRU_C * r * sp
    a = jnp.exp(log_a)
    th = jnp.tanh(log_a)
    bb = jnp.sqrt(-2.0 * th / (1.0 - th)) * (ig * xc)
    sub = lax.broadcasted_iota(jnp.int32, (rows, c), 0) & (SUBLANES - 1)
    for k in (1, 2, 4):
        if reverse:
            ok = sub < SUBLANES - k
            a_s, b_s = pltpu.roll(a, rows - k, 0), pltpu.roll(bb, rows - k, 0)
        else:
            ok = sub >= k
            a_s, b_s = pltpu.roll(a, k, 0), pltpu.roll(bb, k, 0)
        bb = jnp.where(ok, bb + a * b_s, bb)
        a = jnp.where(ok, a * a_s, a)
    return a, bb


def _lru_kernel(ctx_ref, curf_ref, prevf_ref, nextf_ref, curb_ref, prevb_ref, nextb_ref,
                cw_ref, cb_ref, wgf_ref, wgb_ref, bgf_ref, bgb_ref, sp_ref,
                hf_out, hb_out, a_scr, b_scr, carry_scr, *, batch, tt):
    i = pl.program_id(0)
    nt = pl.num_programs(0)
    cw, cb = cw_ref[...], cb_ref[...]
    sp = sp_ref[...]
    c = cw.shape[1]
    params = ((wgf_ref, bgf_ref, sp[0:1]), (wgb_ref, bgb_ref, sp[1:2]))

    def carry_scan(rows, write):
        groups = rows // SUBLANES

        def body(j, carries):
            new = []
            for d in range(2):
                jj = j if d == 0 else groups - 1 - j
                r0 = pl.multiple_of(jj * SUBLANES, SUBLANES)
                for bi in range(batch):
                    h8 = b_scr[d, bi, pl.ds(r0, SUBLANES), :] + a_scr[d, bi, pl.ds(r0, SUBLANES), :] * carries[d * batch + bi]
                    write(d, bi, r0, h8)
                    edge = h8[SUBLANES - 1:SUBLANES] if d == 0 else h8[0:1]
                    new.append(jnp.broadcast_to(edge, (SUBLANES, c)))
            return tuple(new)

        init = tuple(carry_scr[d, bi] for d in range(2) for bi in range(batch))
        out = lax.fori_loop(0, groups, body, init)
        for d in range(2):
            for bi in range(batch):
                carry_scr[d, bi] = out[d * batch + bi]

    @pl.when(i == 0)
    def _():
        carry_scr[...] = jnp.zeros(carry_scr.shape, F32)
        nc = ctx_ref.shape[1]
        zero8 = jnp.zeros((SUBLANES, c), F32)
        for d in range(2):
            wg, bg, spd = params[d]
            for bi in range(batch):
                a, bb = _lru_tile_terms(ctx_ref[bi], zero8, zero8, cw, cb, wg[...], bg[...], spd, d == 1)
                a_scr[d, bi, :nc, :] = a
                b_scr[d, bi, :nc, :] = bb
        carry_scan(nc, lambda d, bi, r0, h8: None)

    tiles = ((curf_ref, prevf_ref, nextf_ref, i), (curb_ref, prevb_ref, nextb_ref, nt - 1 - i))
    for d in range(2):
        cur_ref, prev_ref, next_ref, ti = tiles[d]
        wg, bg, spd = params[d]
        has_prev = (ti > 0).astype(F32)
        has_next = (ti < nt - 1).astype(F32)
        for bi in range(batch):
            a, bb = _lru_tile_terms(cur_ref[bi], prev_ref[bi] * has_prev, next_ref[bi] * has_next,
                                    cw, cb, wg[...], bg[...], spd, d == 1)
            a_scr[d, bi, :tt, :] = a
            b_scr[d, bi, :tt, :] = bb

    def write(d, bi, r0, h8):
        if d == 0:
            hf_out[bi, pl.ds(r0, SUBLANES), :] = h8
        else:
            hb_out[bi, pl.ds(r0, SUBLANES), :] = h8

    carry_scan(tt, write)


def _lru_gate_weights(w_a, w_x):
    def dense(w):
        nb, bs = w.shape[0], w.shape[1]
        eye = jnp.eye(nb, dtype=F32)
        return jnp.einsum('hij,hk->hikj', w.astype(F32), eye).reshape(nb * bs, nb * bs)
    return [jnp.concatenate([dense(w_a[d]), dense(w_x[d])], axis=1).astype(BF16) for d in range(2)]


def _lru(lx_lat, lx_ctx, conv_w, conv_b, w_a, b_a, w_x, b_x, lam, *, tt):
    b, n, c = lx_lat.shape
    nc = lx_ctx.shape[1]
    nt = n // tt
    wgf, wgb = _lru_gate_weights(w_a, w_x)
    bgf = jnp.concatenate([b_a[0], b_x[0]]).reshape(1, 2 * c).astype(F32)
    bgb = jnp.concatenate([b_a[1], b_x[1]]).reshape(1, 2 * c).astype(F32)
    sp = jax.nn.softplus(-lam.astype(F32))
    sp8 = jnp.concatenate([sp, jnp.zeros((SUBLANES - 2, c), F32)], axis=0)
    cw8 = jnp.concatenate([conv_w.astype(F32), jnp.zeros((SUBLANES - conv_w.shape[0], c), F32)], axis=0)
    cb = conv_b.reshape(1, c).astype(F32)
    per = tt // SUBLANES
    last = n // SUBLANES - 1

    def specs(tile_of):
        cur = pl.BlockSpec((b, tt, c), lambda i: (0, tile_of(i), 0))
        prev = pl.BlockSpec((b, SUBLANES, c), lambda i: (0, jnp.maximum(tile_of(i) * per - 1, 0), 0))
        nxt = pl.BlockSpec((b, SUBLANES, c), lambda i: (0, jnp.minimum((tile_of(i) + 1) * per, last), 0))
        return [cur, prev, nxt]

    fwd_tile = lambda i: i
    bwd_tile = lambda i: nt - 1 - i
    full = lambda a: pl.BlockSpec(a.shape, lambda i: (0,) * a.ndim)
    rows_scr = max(tt, nc)
    return pl.pallas_call(
        functools.partial(_lru_kernel, batch=b, tt=tt),
        grid=(nt,),
        in_specs=[full(lx_ctx)] + specs(fwd_tile) + specs(bwd_tile)
        + [full(cw8), full(cb), full(wgf), full(wgb), full(bgf), full(bgb), full(sp8)],
        out_specs=[pl.BlockSpec((b, tt, c), lambda i: (0, i, 0)), pl.BlockSpec((b, tt, c), lambda i: (0, nt - 1 - i, 0))],
        out_shape=[jax.ShapeDtypeStruct((b, n, c), F32)] * 2,
        scratch_shapes=[pltpu.VMEM((2, b, rows_scr, c), F32), pltpu.VMEM((2, b, rows_scr, c), F32),
                        pltpu.VMEM((2, b, SUBLANES, c), F32)],
        compiler_params=_cparams(("arbitrary",)),
        name="rglru",
    )(lx_ctx, lx_lat, lx_lat, lx_lat, lx_lat, lx_lat, lx_lat, cw8, cb, wgf, wgb, bgf, bgb, sp8)


def _odd_out_kernel(x_ref, m_ref, yc_ref, z_ref, x0_ref, hb_ref, hf_ref, hbk_ref, lg_ref, wh_ref, wr_ref, out_ref):
    z = z_ref[...]
    hy = x0_ref[...] * (yc_ref[...] + z * hb_ref[...])
    r = (hf_ref[...] + hbk_ref[...]) * _gelu_tanh(lg_ref[...])
    mix = _dot(hy.astype(BF16), wh_ref[...]) + _dot(r.astype(BF16), wr_ref[...])
    out_ref[...] = x_ref[...] + m_ref[...][5:6] * mix


def _odd_out(tok, m, yc, z, x0, hy_bias, hf, hb, lg, w_out, *, n_rows, tm, mod_row):
    d = tok.shape[1]
    c = yc.shape[1]
    wh, wr = w_out[:c].astype(BF16), w_out[c:].astype(BF16)
    hbias = hy_bias.reshape(1, c).astype(F32)
    full = lambda a: pl.BlockSpec(a.shape, lambda i: (0,) * a.ndim)
    row = lambda w: pl.BlockSpec((tm, w), lambda i: (i, 0))
    return pl.pallas_call(
        _odd_out_kernel,
        grid=(n_rows // tm,),
        in_specs=[row(d), pl.BlockSpec((None, N_MOD, d), lambda i: (mod_row(i), 0, 0)),
                  row(c), row(c), row(c), full(hbias), row(c), row(c), row(c), full(wh), full(wr)],
        out_specs=row(d),
        out_shape=jax.ShapeDtypeStruct((n_rows, d), F32),
        compiler_params=_cparams(("parallel",)),
        name="odd_out",
    )(tok, m, yc, z, x0, hbias, hf, hb, lg, wh, wr)


def kernel(x, c, ctx, c_ctx, mod_w, mod_b, norm_ffn1, norm_mix, norm_ffn2, ffn1_w_gate, ffn1_w_up, ffn1_w_down, ffn2_w_gate, ffn2_w_up, ffn2_w_down, ev_w_in, mla_q_norm, mla_w_uq, mla_kv_norm, mla_w_ukv, s5_lambda_re, s5_lambda_im, s5_log_step, s5_b_re, s5_b_im, s5_c_re, s5_c_im, s5_d, s5_w_glu, s5_b_glu, ev_w_out, od_w_in, hy_conv_w, hy_conv_b, hy_filt_w1, hy_filt_b1, hy_filt_w2, hy_filt_b2, hy_filt_w3, hy_sin_freq, hy_bias, lru_conv_w, lru_conv_b, lru_w_a, lru_b_a, lru_w_x, lru_b_x, lru_lambda, od_w_out, final_norm):
    b, n, d = x.shape
    nc = ctx.shape[1]
    depth = mod_w.shape[0]
    assert depth == 2, "the trunk is laid out for one even and one odd layer"
    n_lat_rows = b * n
    t = n_lat_rows + b * nc
    tm = _row_tile(n, b * nc)
    tpb = n // tm
    mod_row = lambda i: jnp.minimum(i // tpb, b)

    cond8 = jnp.concatenate([c, c_ctx[None, :], jnp.zeros((SUBLANES - b - 1, d), F32)], axis=0)
    m = _adaln(cond8, mod_w, mod_b)
    bf = lambda w: w.astype(BF16)

    tok = _ffn(x.reshape(n_lat_rows, d), m[0], 0, norm_ffn1[0], bf(ffn1_w_gate[0]), bf(ffn1_w_up[0]),
               bf(ffn1_w_down[0]), n_rows=t, tm=tm, mod_row=mod_row, tok_tail=ctx.reshape(b * nc, d))
    ev_w = _even_in_weights(ev_w_in[0], mla_w_uq[0], mla_w_ukv[0])
    qt, k, vt, s_chunks = _even_in(tok, m[0], norm_mix[0], ev_w, mla_q_norm[0], mla_kv_norm[0],
                                   n=n, tm=tm, mod_row=mod_row, n_lat_rows=n_lat_rows)
    o = _attention(qt, k, vt, b=b, n=n, nc=nc)
    s5_ops = _s5_operators(s5_lambda_re[0], s5_lambda_im[0], s5_log_step[0], s5_b_re[0], s5_b_im[0],
                           s5_c_re[0], s5_c_im[0], s5_d[0])
    ys = _s5(s_chunks, s5_ops, b=b, n=n, nc=nc)
    tok = _even_out(tok, m[0], o, ys, s5_w_glu[0], s5_b_glu[0], ev_w_out[0], tm=tm, mod_row=mod_row)
    tok = _ffn(tok, m[0], 2, norm_ffn2[0], bf(ffn2_w_gate[0]), bf(ffn2_w_up[0]), bf(ffn2_w_down[0]),
               n_rows=t, tm=tm, mod_row=mod_row)

    tok = _ffn(tok, m[1], 0, norm_ffn1[1], bf(ffn1_w_gate[1]), bf(ffn1_w_up[1]), bf(ffn1_w_down[1]),
               n_rows=t, tm=tm, mod_row=mod_row)
    hy_w = hy_conv_w.shape[2]
    lru_w = lru_conv_w.shape[2]
    hy, lx, lg, lx_ctx = _odd_in(tok, m[1], norm_mix[1], od_w_in[0], hy_w=hy_w, lru_w=lru_w, tm=tm,
                                 mod_row=mod_row, n_lat_rows=n_lat_rows)
    tt = _row_tile(n)
    z, x0 = _hyena_prep(hy.reshape(b, n, hy_w), hy_conv_w[0], hy_conv_b[0], tt=tt)
    dft = _Dft(n)
    spec = _hyena_filter_spectrum(n, hy_filt_w1[0], hy_filt_b1[0], hy_filt_w2[0], hy_filt_b2[0], hy_filt_w3[0],
                                  hy_sin_freq[0], dft)
    yc = _hyena_conv(z, spec, dft)
    hf, hb = _lru(lx.reshape(b, n, lru_w), lx_ctx.reshape(b, nc, lru_w), lru_conv_w[0],
                  lru_conv_b[0], lru_w_a[0], lru_b_a[0], lru_w_x[0], lru_b_x[0], lru_lambda[0], tt=min(tt, 256))
    flat = lambda a: a.reshape(n_lat_rows, a.shape[2])
    lat = _odd_out(tok, m[1], flat(yc), flat(z), flat(x0), hy_bias[0], flat(hf), flat(hb), lg, od_w_out[0],
                   n_rows=n_lat_rows, tm=tm, mod_row=mod_row)
    lat = _ffn(lat, m[1], 2, norm_ffn2[1], bf(ffn2_w_gate[1]), bf(ffn2_w_up[1]), bf(ffn2_w_down[1]),
               n_rows=n_lat_rows, tm=tm, mod_row=mod_row, final_g=final_norm)
    return lat.reshape(b, n, d)
```

```python
import functools
import math

import numpy as np
import jax
import jax.numpy as jnp
from jax import lax
from jax.experimental import pallas as pl
from jax.experimental.pallas import tpu as pltpu

F32 = jnp.float32
BF16 = jnp.bfloat16

GRID_W = 64
N_MOD = 9
NORM_EPS = 1e-6
ROPE_BASE = 10000.0
MLA_HEADS = 8
QK_NOPE = 64
QK_ROPE = 32
V_DIM = 64
Q_RANK = 384
KV_RANK = 256
S5_GROUP = 16
S5_STATE = 64
HY_BANDS = 16
HY_DECAY_TARGET = 1e-2
HY_MIN_DECAY = math.log(HY_DECAY_TARGET) / 1.5
HY_MAX_DECAY = math.log(HY_DECAY_TARGET) / 0.3
LRU_BLOCKS = 8
LRU_C = 8.0

LANES = 128
SUBLANES = 8
VMEM_BYTES_V7X = 64 * 1024 * 1024
VMEM_LIMIT = VMEM_BYTES_V7X - 8 * 1024 * 1024

HEAD_LANES = LANES
ONES_LANE = V_DIM
ATT_V_ROWS = V_DIM + 16
ATT_Q_TILE = 2048
ATT_Q_LANES = 256
S5_CHUNK = 8
FFT_N2 = 128
DFT_UNROLL = 8
FFN_CHUNK = 256
NEG_BIG = -1e30


def _cparams(sem):
    return pltpu.CompilerParams(dimension_semantics=sem, vmem_limit_bytes=VMEM_LIMIT)


def _dot(a, b):
    return jnp.dot(a, b, preferred_element_type=F32)


def _split(x):
    hi = x.astype(BF16)
    lo = (x - hi.astype(F32)).astype(BF16)
    return hi, lo


def _dot3(a_hi, a_lo, b_hi, b_lo):
    return _dot(a_hi, b_hi) + _dot(a_lo, b_hi) + _dot(a_hi, b_lo)


def _sigmoid(x):
    return 1.0 / (1.0 + jnp.exp(-x))


def _gelu_tanh(x):
    return 0.5 * x * (1.0 + jnp.tanh(math.sqrt(2.0 / math.pi) * (x + 0.044715 * (x * x * x))))


def _rms(x, g):
    return x * lax.rsqrt(jnp.mean(x * x, axis=-1, keepdims=True) + NORM_EPS) * g


def _row_tile(*lengths):
    for t in (512, 256, 128, 64, 32, 16, 8):
        if all(l % t == 0 for l in lengths):
            return t
    raise ValueError(f"no row tile divides {lengths}")


def _adaln_kernel(c_ref, w_ref, b_ref, o_ref):
    c = c_ref[...]
    s_hi, s_lo = _split(c * _sigmoid(c))
    w_hi, w_lo = _split(w_ref[...])
    o_ref[...] = _dot3(s_hi, s_lo, w_hi, w_lo) + b_ref[...]


def _adaln(cond8, mod_w, mod_b):
    depth, d, nd = mod_w.shape
    tn = d
    out = pl.pallas_call(
        _adaln_kernel,
        grid=(depth, nd // tn),
        in_specs=[
            pl.BlockSpec((SUBLANES, d), lambda l, j: (0, 0)),
            pl.BlockSpec((None, d, tn), lambda l, j: (l, 0, j)),
            pl.BlockSpec((None, 1, tn), lambda l, j: (l, 0, j)),
        ],
        out_specs=pl.BlockSpec((None, SUBLANES, tn), lambda l, j: (l, 0, j)),
        out_shape=jax.ShapeDtypeStruct((depth, SUBLANES, nd), F32),
        compiler_params=_cparams(("arbitrary", "arbitrary")),
        name="adaln",
    )(cond8, mod_w, mod_b.reshape(depth, 1, nd))
    return out.reshape(depth, SUBLANES, N_MOD, d)


def _ffn_half_step(x, m, g_ref, wg_ref, wu_ref, wd_ref, k):
    shift, scale, gate = m[3 * k:3 * k + 1], m[3 * k + 1:3 * k + 2], m[3 * k + 2:3 * k + 3]
    hb = (_rms(x, g_ref[...]) * (1.0 + scale) + shift).astype(BF16)
    f_all = wg_ref.shape[1]
    tf = FFN_CHUNK if f_all % FFN_CHUNK == 0 else LANES
    acc = jnp.zeros(x.shape, F32)
    for f in range(f_all // tf):
        sl = slice(f * tf, (f + 1) * tf)
        gt = _dot(hb, wg_ref[:, sl])
        up = _dot(hb, wu_ref[:, sl])
        acc = acc + _dot((gt * _sigmoid(gt) * up).astype(BF16), wd_ref[sl, :])
    return x + (0.5 * gate) * acc


def _even_in_kernel(x_ref, tail_ref, m_ref, g1_ref, wg_ref, wu_ref, wd_ref,
                    g_ref, w1_ref, qn_ref, kvn_ref, wq_ref, wqp_ref, wk_ref, wv_ref,
                    c_ref, s_ref, tok_out, qt_out, k_out, vt_out, u_out, s_scr, *, tiles_first):
    m = m_ref[...]
    x = jnp.where(pl.program_id(0) < tiles_first, x_ref[...], tail_ref[...])
    x = _ffn_half_step(x, m, g1_ref, wg_ref, wu_ref, wd_ref, 0)
    tok_out[...] = x
    hb = (_rms(x, g_ref[...]) * (1.0 + m[4:5]) + m[3:4]).astype(BF16)
    p = _dot(hb, w1_ref[...])
    o = 0
    cq = p[:, o:o + Q_RANK]; o += Q_RANK
    ckv = p[:, o:o + KV_RANK]; o += KV_RANK
    kra = p[:, o:o + HEAD_LANES]; o += HEAD_LANES
    krb = p[:, o:o + HEAD_LANES]; o += HEAD_LANES
    pw = 2 * S5_GROUP
    per_tile = LANES // pw
    chunks = s_scr.shape[1] // S5_CHUNK
    for lt in range(s_scr.shape[0]):
        s_scr[lt] = p[:, o + lt * LANES:o + (lt + 1) * LANES]
        by_tok = [s_scr[lt, pl.ds(t, chunks, stride=S5_CHUNK), :] for t in range(S5_CHUNK)]
        for l in range(per_tile):
            u_out[lt * per_tile + l] = jnp.concatenate([r[:, l * pw:(l + 1) * pw] for r in by_tok], axis=-1).astype(BF16)
    cos, sin = c_ref[...], s_ref[...]
    cqn = _rms(cq, qn_ref[...]).astype(BF16)
    qa = _dot(cqn, wq_ref[...])
    qb = _dot(cqn, wqp_ref[...])
    ckvn = _rms(ckv, kvn_ref[...]).astype(BF16)
    kn = _dot(ckvn, wk_ref[...])
    vv = _dot(ckvn, wv_ref[...])
    kr = kra * cos + krb * sin
    ones = (lax.broadcasted_iota(jnp.int32, (1, HEAD_LANES), 1) == ONES_LANE).astype(F32)
    scale = (QK_NOPE + QK_ROPE) ** -0.5 * math.log2(math.e)
    for h in range(MLA_HEADS):
        sl = slice(h * HEAD_LANES, (h + 1) * HEAD_LANES)
        qt_out[h] = ((qa[:, sl] * cos + qb[:, sl] * sin) * scale).T.astype(BF16)
        k_out[h] = (kn[:, sl] + kr).astype(BF16)
        vt_out[h] = (vv[:, sl] + ones).T.astype(BF16)


def _rope_partner(w):
    half = w.shape[-1] // 2
    return jnp.concatenate([-w[..., half:], w[..., :half]], axis=-1)


def _even_in_weights(w_in, w_uq, w_ukv):
    d = w_in.shape[0]
    o = 0
    w_cq = w_in[:, o:o + Q_RANK]; o += Q_RANK
    w_ckv = w_in[:, o:o + KV_RANK]; o += KV_RANK
    w_kr = w_in[:, o:o + QK_ROPE]; o += QK_ROPE
    w_s = w_in[:, o:]
    pad_l = jnp.zeros((d, QK_NOPE), F32)
    pad_r = jnp.zeros((d, HEAD_LANES - QK_NOPE - QK_ROPE), F32)
    kr_blk = jnp.concatenate([pad_l, w_kr, pad_r], axis=1)
    kr_blk_p = jnp.concatenate([pad_l, _rope_partner(w_kr), pad_r], axis=1)
    w1 = jnp.concatenate([w_cq, w_ckv, kr_blk, kr_blk_p, w_s], axis=1).astype(BF16)

    dk = QK_NOPE + QK_ROPE
    wq = w_uq.reshape(Q_RANK, MLA_HEADS, dk)
    zq = jnp.zeros((Q_RANK, MLA_HEADS, HEAD_LANES - dk), F32)
    wq_pad = jnp.concatenate([wq, zq], axis=-1)
    wq_par = jnp.concatenate([jnp.zeros((Q_RANK, MLA_HEADS, QK_NOPE), F32), _rope_partner(wq[..., QK_NOPE:]), zq], axis=-1)
    wkv = w_ukv.reshape(KV_RANK, MLA_HEADS, QK_NOPE + V_DIM)
    wk = jnp.concatenate([wkv[..., :QK_NOPE], jnp.zeros((KV_RANK, MLA_HEADS, HEAD_LANES - QK_NOPE), F32)], axis=-1)
    wv = jnp.concatenate([wkv[..., QK_NOPE:], jnp.zeros((KV_RANK, MLA_HEADS, HEAD_LANES - V_DIM), F32)], axis=-1)
    flat = lambda w: w.reshape(w.shape[0], MLA_HEADS * HEAD_LANES).astype(BF16)
    return w1, flat(wq_pad), flat(wq_par), flat(wk), flat(wv)


def _rope_tables(n, n_ident):
    rows = n // GRID_W
    row = np.repeat(np.arange(rows, dtype=np.float32), GRID_W)
    col = np.tile(np.arange(GRID_W, dtype=np.float32), rows)
    n_freq = QK_ROPE // 4
    inv_freq = (np.float32(ROPE_BASE) ** (-np.arange(n_freq, dtype=np.float32) / n_freq)).astype(np.float32)
    ang = np.concatenate([row[:, None] * inv_freq, col[:, None] * inv_freq], axis=-1).astype(np.float64)
    ang = np.concatenate([ang, np.zeros((n_ident, QK_ROPE // 2))], axis=0)
    tot = n + n_ident
    pad = HEAD_LANES - QK_NOPE - QK_ROPE
    cos = np.concatenate([np.ones((tot, QK_NOPE)), np.cos(ang), np.cos(ang), np.ones((tot, pad))], axis=1)
    sin = np.concatenate([np.zeros((tot, QK_NOPE)), np.sin(ang), np.sin(ang), np.zeros((tot, pad))], axis=1)
    return jnp.asarray(cos, F32), jnp.asarray(sin, F32)


def _even_in(lat, tail, m, ffn1, g, weights, q_norm, kv_norm, *, n, tm, mod_row):
    n_lat_rows, d = lat.shape
    t = n_lat_rows + tail.shape[0]
    g1, wg1, wu1, wd1 = ffn1
    gg1 = g1.reshape(1, d)
    w1, wq, wqp, wk, wv = weights
    cos, sin = _rope_tables(n, tm)
    tiles_lat = n_lat_rows // tm
    tpb = n // tm
    tab_row = lambda i: jnp.where(i < tiles_lat, i % tpb, tpb)
    full = lambda a: pl.BlockSpec(a.shape, lambda i: (0,) * a.ndim, pipeline_mode=pl.Buffered(1))
    hl = MLA_HEADS * HEAD_LANES
    s_ch = w1.shape[1] - Q_RANK - KV_RANK - 2 * HEAD_LANES
    qn, kvn = q_norm.reshape(1, Q_RANK), kv_norm.reshape(1, KV_RANK)
    gg = g.reshape(1, d)
    head_spec = pl.BlockSpec((MLA_HEADS, tm, HEAD_LANES), lambda i: (0, i, 0))
    head_shape = jax.ShapeDtypeStruct((MLA_HEADS, t, HEAD_LANES), BF16)
    head_t_spec = pl.BlockSpec((MLA_HEADS, HEAD_LANES, tm), lambda i: (0, 0, i))
    head_t_shape = jax.ShapeDtypeStruct((MLA_HEADS, HEAD_LANES, t), BF16)
    pairs = s_ch // (2 * S5_GROUP)
    u_w = 2 * S5_GROUP * S5_CHUNK
    return pl.pallas_call(
        functools.partial(_even_in_kernel, tiles_first=tiles_lat),
        grid=(t // tm,),
        in_specs=[
            pl.BlockSpec((tm, d), lambda i: (jnp.minimum(i, tiles_lat - 1), 0)),
            pl.BlockSpec((tm, d), lambda i: (jnp.maximum(i - tiles_lat, 0), 0)),
            pl.BlockSpec((None, N_MOD, d), lambda i: (mod_row(i), 0, 0)),
            full(gg1), full(wg1), full(wu1), full(wd1),
            full(gg), full(w1), full(qn), full(kvn), full(wq), full(wqp), full(wk), full(wv),
            pl.BlockSpec((tm, HEAD_LANES), lambda i: (tab_row(i), 0)),
            pl.BlockSpec((tm, HEAD_LANES), lambda i: (tab_row(i), 0)),
        ],
        out_specs=[pl.BlockSpec((tm, d), lambda i: (i, 0)), head_t_spec, head_spec, head_t_spec,
                   pl.BlockSpec((pairs, tm // S5_CHUNK, u_w), lambda i: (0, i, 0))],
        out_shape=[jax.ShapeDtypeStruct((t, d), F32), head_t_shape, head_shape, head_t_shape,
                   jax.ShapeDtypeStruct((pairs, t // S5_CHUNK, u_w), BF16)],
        scratch_shapes=[pltpu.VMEM((s_ch // LANES, tm, LANES), F32)],
        compiler_params=_cparams(("parallel",)),
        name="ffn_even_in",
    )(lat, tail, m, gg1, wg1, wu1, wd1, gg, w1, qn, kvn, wq, wqp, wk, wv, cos, sin)


def _attn_kernel(qt_ref, kc_ref, vtc_ref, *rest, tk, lat):
    if lat:
        kl_ref, vtl_ref, o_ref = rest
    else:
        _, o_ref = rest
    tq = qt_ref.shape[1]
    lanes = min(ATT_Q_LANES, tq)
    subs = [slice(j * lanes, (j + 1) * lanes) for j in range(tq // lanes)]

    def step(k, vt, carry):
        scores = [_dot(k, qt_ref[:, sl]) for sl in subs]
        out = []
        for s, (m, acc) in zip(scores, carry):
            m_new = jnp.maximum(m, jnp.max(s, axis=0, keepdims=True))
            p = jnp.exp2(s - m_new).astype(BF16)
            out.append((m_new, jnp.exp2(m - m_new) * acc + _dot(vt, p)))
        return tuple(out)

    carry = tuple((jnp.full((1, lanes), NEG_BIG, F32), jnp.zeros((ATT_V_ROWS, lanes), F32)) for _ in subs)
    carry = step(kc_ref[...], vtc_ref[:ATT_V_ROWS, :], carry)
    if lat:
        def body(c, carry):
            start = pl.multiple_of(c * tk, tk)
            return step(kl_ref[pl.ds(start, tk), :], vtl_ref[:ATT_V_ROWS, pl.ds(start, tk)], carry)

        n_chunks = kl_ref.shape[0] // tk
        carry = lax.fori_loop(0, n_chunks, body, carry, unroll=2 if n_chunks % 2 == 0 else 1)
    for sl, (_, acc) in zip(subs, carry):
        o = acc / acc[ONES_LANE:ONES_LANE + 1]
        o = jnp.concatenate([o, jnp.zeros((HEAD_LANES - ATT_V_ROWS, lanes), F32)], axis=0)
        o_ref[sl, :] = o.T.astype(BF16)


def _attention(qt, k, vt, *, b, n, nc):
    h, t, _ = k.shape
    tq = min(ATT_Q_TILE, n)
    n_q = n // tq
    tk = _row_tile(n)
    lat_blocks = (b * n) // nc
    out_shape = jax.ShapeDtypeStruct((h, t, HEAD_LANES), BF16)
    o = pl.pallas_call(
        functools.partial(_attn_kernel, tk=tk, lat=True),
        grid=(b, h, n_q),
        in_specs=[
            pl.BlockSpec((None, HEAD_LANES, tq), lambda bi, hi, i: (hi, 0, bi * n_q + i)),
            pl.BlockSpec((None, nc, HEAD_LANES), lambda bi, hi, i: (hi, lat_blocks + bi, 0)),
            pl.BlockSpec((None, HEAD_LANES, nc), lambda bi, hi, i: (hi, 0, lat_blocks + bi)),
            pl.BlockSpec((None, n, HEAD_LANES), lambda bi, hi, i: (hi, bi, 0)),
            pl.BlockSpec((None, HEAD_LANES, n), lambda bi, hi, i: (hi, 0, bi)),
        ],
        out_specs=pl.BlockSpec((None, tq, HEAD_LANES), lambda bi, hi, i: (hi, bi * n_q + i, 0)),
        out_shape=out_shape,
        compiler_params=_cparams(("parallel", "parallel", "arbitrary")),
        name="attention",
    )(qt, k, vt, k, vt)
    return pl.pallas_call(
        functools.partial(_attn_kernel, tk=tk, lat=False),
        grid=(b, h),
        in_specs=[
            pl.BlockSpec((None, HEAD_LANES, nc), lambda bi, hi: (hi, 0, lat_blocks + bi)),
            pl.BlockSpec((None, nc, HEAD_LANES), lambda bi, hi: (hi, lat_blocks + bi, 0)),
            pl.BlockSpec((None, HEAD_LANES, nc), lambda bi, hi: (hi, 0, lat_blocks + bi)),
            pl.BlockSpec(memory_space=pl.ANY),
        ],
        out_specs=pl.BlockSpec((None, nc, HEAD_LANES), lambda bi, hi: (hi, lat_blocks + bi, 0)),
        out_shape=out_shape,
        input_output_aliases={3: 0},
        compiler_params=_cparams(("parallel", "parallel")),
        name="attention_ctx",
    )(qt, k, vt, o)


def _s5_operators(lam_re, lam_im, log_step, b_re, b_im, c_re, c_im, d_skip):
    tc = S5_CHUNK
    g_n, p_n = lam_re.shape[1], lam_re.shape[2]
    lam = lax.complex(lam_re.astype(F32), lam_im.astype(F32))
    step = jnp.exp(log_step.astype(F32))[..., None]
    la = lam * step
    a_bar = jnp.exp(la)
    bb = ((a_bar - 1.0) / lam)[..., None] * lax.complex(b_re.astype(F32), b_im.astype(F32))
    cm = lax.complex(c_re.astype(F32), c_im.astype(F32))
    kk = jnp.arange(tc + 1, dtype=F32)
    apow = jnp.exp(la[..., None] * kk)

    win_f = jnp.einsum('gps,gpc->gscp', apow[0][..., tc - 1::-1][..., :tc], bb[0])
    win_b = jnp.einsum('gps,gpc->gscp', apow[1][..., :tc], bb[1])
    wout_f = jnp.einsum('gcp,gpt->gptc', cm[0], apow[0][..., 1:])
    wout_b = jnp.einsum('gcp,gpt->gptc', cm[1], apow[1][..., tc:0:-1])
    kf = jnp.real(jnp.einsum('gop,gpk,gpi->gkoi', cm[0], apow[0][..., :tc], bb[0]))
    kb = jnp.real(jnp.einsum('gop,gpk,gpi->gkoi', cm[1], apow[1][..., :tc], bb[1]))
    s_idx = jnp.arange(tc)[:, None]
    t_idx = jnp.arange(tc)[None, :]
    lag = t_idx - s_idx
    kf_st = jnp.where((lag >= 0)[None, :, :, None, None], kf[:, jnp.clip(lag, 0, tc - 1)], 0.0)
    kb_st = jnp.where((lag <= 0)[None, :, :, None, None], kb[:, jnp.clip(-lag, 0, tc - 1)], 0.0)
    eye_t = jnp.eye(tc, dtype=F32)[None, :, :, None, None]
    eye_c = jnp.eye(S5_GROUP, dtype=F32)[None, None, None]
    dsk = d_skip.astype(F32).reshape(g_n, 1, 1, S5_GROUP, 1)
    loc = kf_st + kb_st + eye_t * eye_c * dsk
    loc = jnp.transpose(loc, (0, 1, 4, 2, 3))

    gp = g_n // 2
    gc2 = 2 * tc * S5_GROUP
    eye2 = jnp.eye(2, dtype=F32)
    pair = lambda x: x.reshape((gp, 2) + x.shape[1:])
    w_in_part = lambda w: jnp.einsum('qgscp,gh->qsgchp', pair(w), eye2).reshape(gp, gc2, 2 * p_n)
    w_out_part = lambda w: jnp.einsum('qgptc,gh->qgpthc', pair(w), eye2).reshape(gp, 2 * p_n, gc2)
    w_in = jnp.concatenate([w_in_part(jnp.real(win_f)), w_in_part(jnp.imag(win_f)),
                            w_in_part(jnp.real(win_b)), w_in_part(jnp.imag(win_b))], axis=2)
    w_out = jnp.concatenate([w_out_part(jnp.real(wout_f)), w_out_part(-jnp.imag(wout_f)),
                             w_out_part(jnp.real(wout_b)), w_out_part(-jnp.imag(wout_b))], axis=1)
    loc = jnp.einsum('qgsito,gh->qsgitho', pair(loc), eye2).reshape(gp, gc2, gc2)
    a_tc = apow[..., tc].reshape(2, gp, 2 * p_n)
    a_rows = jnp.stack([jnp.real(a_tc[0]), jnp.imag(a_tc[0]), jnp.real(a_tc[1]), jnp.imag(a_tc[1])], axis=1)
    a_rows = jnp.concatenate([a_rows, jnp.zeros((gp, SUBLANES - 4, 2 * p_n), F32)], axis=1)
    return w_in.astype(BF16), w_out.astype(BF16), loc.astype(BF16), a_rows


def _s5_kernel(u_ref, win_ref, wout_ref, loc_ref, a_ref, y_ref, s_scr, h_scr, *, batch, lat_chunks, ctx_chunks):
    u = u_ref[...]
    w = a_ref.shape[1]
    n_parts = s_scr.shape[0]
    s_all = _dot(u, win_ref[...])
    for k in range(n_parts):
        s_scr[k] = s_all[:, k * w:(k + 1) * w]
    a = a_ref[...]
    bc = lambda r: jnp.broadcast_to(a[r:r + 1], (batch, w))
    arf, aif, arb, aib = bc(0), bc(1), bc(2), bc(3)

    def chunk(rows, carry, part, ar, ai):
        hr, hi = carry
        sr = s_scr[part, rows, :]
        si = s_scr[part + 1, rows, :]
        h_scr[part, rows, :] = hr
        h_scr[part + 1, rows, :] = hi
        return ar * hr - ai * hi + sr, ar * hi + ai * hr + si

    def segment(base, per):
        def body(j, carry):
            cf, cb = carry
            cf = chunk(pl.ds(base + j, batch, stride=per), cf, 0, arf, aif)
            cb = chunk(pl.ds(base + per - 1 - j, batch, stride=per), cb, 2, arb, aib)
            return cf, cb
        return body

    zero = (jnp.zeros((batch, w), F32), jnp.zeros((batch, w), F32))
    carry = lax.fori_loop(0, ctx_chunks, segment(batch * lat_chunks, ctx_chunks), (zero, zero))
    lax.fori_loop(0, lat_chunks, segment(0, lat_chunks), carry)
    h_all = jnp.concatenate([h_scr[k] for k in range(n_parts)], axis=-1).astype(BF16)
    y_ref[...] = _dot(u, loc_ref[...]) + _dot(h_all, wout_ref[...])


def _s5(u, ops, *, b, n, nc):
    w_in, w_out, loc, a_rows = ops
    gp, r, wid = u.shape
    sw = w_in.shape[2]
    return pl.pallas_call(
        functools.partial(_s5_kernel, batch=b, lat_chunks=n // S5_CHUNK, ctx_chunks=nc // S5_CHUNK),
        grid=(gp,),
        in_specs=[
            pl.BlockSpec((None, r, wid), lambda g: (g, 0, 0)),
            pl.BlockSpec((None, wid, sw), lambda g: (g, 0, 0)),
            pl.BlockSpec((None, sw, wid), lambda g: (g, 0, 0)),
            pl.BlockSpec((None, wid, wid), lambda g: (g, 0, 0)),
            pl.BlockSpec((None, SUBLANES, sw // 4), lambda g: (g, 0, 0)),
        ],
        out_specs=pl.BlockSpec((None, r, wid), lambda g: (g, 0, 0)),
        out_shape=jax.ShapeDtypeStruct((gp, r, wid), F32),
        scratch_shapes=[pltpu.VMEM((4, r, sw // 4), F32), pltpu.VMEM((4, r, sw // 4), F32)],
        compiler_params=_cparams(("parallel",)),
        name="s5",
    )(u, w_in, w_out, loc, a_rows)


def _even_out_kernel(x_ref, m_ref, o_ref, yc_ref, wglu_ref, bglu_ref, woo_ref, wos_ref,
                     g2_ref, wg_ref, wu_ref, wd_ref, out_ref, ys_scr):
    pw = 2 * S5_GROUP
    per_tile = LANES // pw
    chunks = yc_ref.shape[1]
    for lt in range(ys_scr.shape[0]):
        by_pair = [yc_ref[lt * per_tile + l] for l in range(per_tile)]
        for t in range(S5_CHUNK):
            ys_scr[lt, pl.ds(t, chunks, stride=S5_CHUNK), :] = jnp.concatenate(
                [y[:, t * pw:(t + 1) * pw] for y in by_pair], axis=-1)
    y = _gelu_tanh(jnp.concatenate([ys_scr[lt] for lt in range(ys_scr.shape[0])], axis=-1))
    y = y * _sigmoid(_dot(y.astype(BF16), wglu_ref[...]) + bglu_ref[...])
    oc = jnp.concatenate([o_ref[h] for h in range(MLA_HEADS)], axis=-1)
    mix = _dot(oc, woo_ref[...]) + _dot(y.astype(BF16), wos_ref[...])
    m = m_ref[...]
    out_ref[...] = _ffn_half_step(x_ref[...] + m[5:6] * mix, m, g2_ref, wg_ref, wu_ref, wd_ref, 2)


def _even_out(tok, m, o, yc, w_glu, b_glu, w_out, g2, wg2, wu2, wd2, *, tm, mod_row):
    t, d = tok.shape
    pairs, _, wid = yc.shape
    sc = pairs * 2 * S5_GROUP
    wo = w_out[:MLA_HEADS * V_DIM].reshape(MLA_HEADS, V_DIM, d)
    woo = jnp.concatenate([wo, jnp.zeros((MLA_HEADS, HEAD_LANES - V_DIM, d), F32)], axis=1)
    woo = woo.reshape(MLA_HEADS * HEAD_LANES, d).astype(BF16)
    wos = w_out[MLA_HEADS * V_DIM:].astype(BF16)
    wg = w_glu.astype(BF16)
    bg = b_glu.reshape(1, sc).astype(F32)
    gg2 = g2.reshape(1, d)
    full = lambda a: pl.BlockSpec(a.shape, lambda i: (0,) * a.ndim, pipeline_mode=pl.Buffered(1))
    return pl.pallas_call(
        _even_out_kernel,
        grid=(t // tm,),
        in_specs=[
            pl.BlockSpec((tm, d), lambda i: (i, 0)),
            pl.BlockSpec((None, N_MOD, d), lambda i: (mod_row(i), 0, 0)),
            pl.BlockSpec((MLA_HEADS, tm, HEAD_LANES), lambda i: (0, i, 0)),
            pl.BlockSpec((pairs, tm // S5_CHUNK, wid), lambda i: (0, i, 0)),
            full(wg), full(bg), full(woo), full(wos), full(gg2), full(wg2), full(wu2), full(wd2),
        ],
        out_specs=pl.BlockSpec((tm, d), lambda i: (i, 0)),
        out_shape=jax.ShapeDtypeStruct((t, d), F32),
        scratch_shapes=[pltpu.VMEM((sc // LANES, tm, LANES), F32)],
        compiler_params=_cparams(("parallel",)),
        name="even_out_ffn",
    )(tok, m, o, yc, wg, bg, woo, wos, gg2, wg2, wu2, wd2)


def _odd_in_kernel(x_ref, m_ref, g1_ref, wg_ref, wu_ref, wd_ref, g_ref, w_ref,
                   tok_out, hy_out, lx_out, lg_out, lxc_out, *, tiles_lat):
    m = m_ref[...]
    x = _ffn_half_step(x_ref[...], m, g1_ref, wg_ref, wu_ref, wd_ref, 0)
    hb = (_rms(x, g_ref[...]) * (1.0 + m[4:5]) + m[3:4]).astype(BF16)
    n_hy, n_lx = hy_out.shape[1], lx_out.shape[1]
    is_lat = pl.program_id(0) < tiles_lat

    @pl.when(is_lat)
    def _():
        tok_out[...] = x
        p = _dot(hb, w_ref[...])
        hy_out[...] = p[:, :n_hy]
        lx_out[...] = p[:, n_hy:n_hy + n_lx]
        lg_out[...] = p[:, n_hy + n_lx:]

    @pl.when(jnp.logical_not(is_lat))
    def _():
        lxc_out[...] = _dot(hb, w_ref[:, n_hy:n_hy + n_lx])


def _odd_in(tok, m, ffn1, g, w_in, *, hy_w, lru_w, tm, mod_row, n_lat_rows):
    t, d = tok.shape
    g1, wg1, wu1, wd1 = ffn1
    gg1 = g1.reshape(1, d)
    w = w_in.astype(BF16)
    gg = g.reshape(1, d)
    tiles_lat = n_lat_rows // tm
    full = lambda a: pl.BlockSpec(a.shape, lambda i: (0,) * a.ndim, pipeline_mode=pl.Buffered(1))
    lat_row = lambda c: pl.BlockSpec((tm, c), lambda i: (jnp.minimum(i, tiles_lat - 1), 0))
    ctx_row = lambda c: pl.BlockSpec((tm, c), lambda i: (jnp.maximum(i - tiles_lat, 0), 0))
    return pl.pallas_call(
        functools.partial(_odd_in_kernel, tiles_lat=tiles_lat),
        grid=(t // tm,),
        in_specs=[pl.BlockSpec((tm, d), lambda i: (i, 0)),
                  pl.BlockSpec((None, N_MOD, d), lambda i: (mod_row(i), 0, 0)),
                  full(gg1), full(wg1), full(wu1), full(wd1), full(gg), full(w)],
        out_specs=[lat_row(d), lat_row(hy_w), lat_row(lru_w), lat_row(lru_w), ctx_row(lru_w)],
        out_shape=[jax.ShapeDtypeStruct((n_lat_rows, c), F32) for c in (d, hy_w, lru_w, lru_w)]
        + [jax.ShapeDtypeStruct((t - n_lat_rows, lru_w), F32)],
        compiler_params=_cparams(("arbitrary",)),
        name="ffn_odd_in",
    )(tok, m, gg1, wg1, wu1, wd1, gg, w)


def _shift_down(cur, prev8, k):
    r = pltpu.roll(cur, k, 0)
    row = lax.broadcasted_iota(jnp.int32, prev8.shape, 0)
    head = jnp.where(row < k, pltpu.roll(prev8, k, 0), r[:SUBLANES])
    return jnp.concatenate([head, r[SUBLANES:]], axis=0)


def _shift_up(cur, next8, k):
    rows = cur.shape[0]
    r = pltpu.roll(cur, rows - k, 0)
    row = lax.broadcasted_iota(jnp.int32, next8.shape, 0)
    tail = jnp.where(row >= SUBLANES - k, pltpu.roll(next8, SUBLANES - k, 0), r[rows - SUBLANES:])
    return jnp.concatenate([r[:rows - SUBLANES], tail], axis=0)


def _halo_specs(tt, width, n_tiles, tile_of):
    per = tt // SUBLANES
    last = n_tiles * per - 1
    cur = pl.BlockSpec((None, tt, width), lambda b, i: (b, tile_of(i), 0))
    prev = pl.BlockSpec((None, SUBLANES, width), lambda b, i: (b, jnp.maximum(tile_of(i) * per - 1, 0), 0))
    nxt = pl.BlockSpec((None, SUBLANES, width), lambda b, i: (b, jnp.minimum((tile_of(i) + 1) * per, last), 0))
    return cur, prev, nxt


def _hyena_prep_kernel(cur_ref, prev_ref, next_ref, w_ref, b_ref, z_out, x0_out):
    i = pl.program_id(1)
    cur = cur_ref[...]
    prev8 = prev_ref[...] * (i > 0).astype(F32)
    next8 = next_ref[...] * (i < pl.num_programs(1) - 1).astype(F32)
    w = w_ref[...]
    u = w[0:1] * _shift_down(cur, prev8, 1) + w[1:2] * cur + w[2:3] * _shift_up(cur, next8, 1) + b_ref[...]
    c = z_out.shape[1]
    x0_out[...] = u[:, :c]
    z_out[...] = u[:, 2 * c:] * u[:, c:2 * c]


def _hyena_prep(hy, conv_w, conv_b, *, tt):
    b, n, c3 = hy.shape
    c = c3 // 3
    cur, prev, nxt = _halo_specs(tt, c3, n // tt, lambda i: i)
    w8 = jnp.concatenate([conv_w.astype(F32), jnp.zeros((SUBLANES - conv_w.shape[0], c3), F32)], axis=0)
    out_spec = pl.BlockSpec((None, tt, c), lambda bi, i: (bi, i, 0))
    return pl.pallas_call(
        _hyena_prep_kernel,
        grid=(b, n // tt),
        in_specs=[cur, prev, nxt, pl.BlockSpec((SUBLANES, c3), lambda bi, i: (0, 0)),
                  pl.BlockSpec((1, c3), lambda bi, i: (0, 0))],
        out_specs=[out_spec, out_spec],
        out_shape=[jax.ShapeDtypeStruct((b, n, c), F32)] * 2,
        compiler_params=_cparams(("parallel", "parallel")),
        name="hyena_prep",
    )(hy, hy, hy, w8, conv_b.reshape(1, c3).astype(F32))


class _Dft:
    def __init__(self, n):
        self.n = n
        self.N = 2 * n
        self.N2 = FFT_N2
        self.N1 = self.N // self.N2
        self.A = n // self.N2
        self.K1 = self.N1 // 2 + 1
        self.K1p = -(-self.K1 // SUBLANES) * SUBLANES
        N, N1, N2, A, K1, K1p = self.N, self.N1, self.N2, self.A, self.K1, self.K1p
        b = np.arange(N2)[:, None, None]
        k1 = np.arange(K1p)[None, :, None]
        a = np.arange(A)[None, None, :]
        phi = 2.0 * np.pi * ((a * k1 % N1) / N1 + (b * k1 % N) / N)
        live = (k1 < K1).astype(np.float64)
        f1 = np.concatenate([np.cos(phi) * live, -np.sin(phi) * live], axis=1)
        wgt = np.where((k1 == 0) | (k1 == N1 // 2), 1.0, 2.0) * live / N
        g1 = np.concatenate([np.cos(phi) * wgt, -np.sin(phi) * wgt], axis=1)
        g1 = np.transpose(g1, (0, 2, 1))
        ang = 2.0 * np.pi * (np.arange(N2)[:, None] * np.arange(N2)[None, :] % N2) / N2
        c, s = np.cos(ang), np.sin(ang)
        f2 = np.block([[c, s], [-s, c]])
        g2 = np.block([[c, -s], [s, c]])
        self.f1, self.f2, self.g2, self.g1 = (jnp.asarray(x, F32).astype(BF16) for x in (f1, f2, g2, g1))


def _dft_forward(src_ref, f1_ref, f2_ref, y_scr, put, dft):
    n2, a_n, k1p2 = dft.N2, dft.A, 2 * dft.K1p

    def stage1(b, _):
        xb = src_ref[pl.ds(b, a_n, stride=n2), :]
        r0 = pl.multiple_of(b * k1p2, SUBLANES)
        y_scr[pl.ds(r0, k1p2), :] = _dot(f1_ref[b], xb.astype(BF16))
        return 0

    lax.fori_loop(0, n2, stage1, 0, unroll=DFT_UNROLL)

    def stage2(k1, _):
        yr = y_scr[pl.ds(k1, n2, stride=k1p2), :]
        yi = y_scr[pl.ds(dft.K1p + k1, n2, stride=k1p2), :]
        put(k1, _dot(f2_ref[...], jnp.concatenate([yr, yi], axis=0).astype(BF16)))
        return 0

    even_planes = dft.K1 - dft.K1 % DFT_UNROLL
    lax.fori_loop(0, even_planes, stage2, 0, unroll=DFT_UNROLL)
    for k1 in range(even_planes, dft.K1):
        stage2(k1, 0)


def _hyena_filter_kernel(feat_ref, w1_ref, b1_ref, w2_ref, b2_ref, w3f_ref, w3b_ref, freq_ref, delta_ref,
                         f1_ref, f2_ref, h_out, filt_scr, y_scr, hid_scr, *, dft):
    n = dft.n
    rb = _row_tile(n)
    blocks = n // rb
    rows_of = lambda r: pl.ds(pl.multiple_of(r * rb, rb), rb)

    @pl.when(pl.program_id(0) == 0)
    def _():
        freq = freq_ref[...]
        w1_hi, w1_lo = _split(w1_ref[...])
        w2_hi, w2_lo = _split(w2_ref[...])

        def hidden(r, _):
            f_hi, f_lo = _split(feat_ref[rows_of(r), :])
            h = jnp.sin(freq[0:1] * (_dot3(f_hi, f_lo, w1_hi, w1_lo) + b1_ref[...]))
            h_hi, h_lo = _split(h)
            hid_scr[rows_of(r), :] = jnp.sin(freq[1:2] * (_dot3(h_hi, h_lo, w2_hi, w2_lo) + b2_ref[...]))
            return 0

        lax.fori_loop(0, blocks, hidden, 0)

    def fill_filter(w3_ref):
        w_hi, w_lo = _split(w3_ref[...])

        def blk(r, _):
            h_hi, h_lo = _split(hid_scr[rows_of(r), :])
            decay = jnp.exp(-feat_ref[rows_of(r), 0:1] * delta_ref[...])
            filt_scr[rows_of(r), :] = _dot3(h_hi, h_lo, w_hi, w_lo) * decay
            return 0

        lax.fori_loop(0, blocks, blk, 0)

    n2 = dft.N2
    fill_filter(w3f_ref)

    def put_fwd(k1, x):
        h_out[k1] = x

    _dft_forward(filt_scr, f1_ref, f2_ref, y_scr, put_fwd, dft)

    fill_filter(w3b_ref)
    first = lax.broadcasted_iota(jnp.int32, (SUBLANES, LANES), 0) == 0
    filt_scr[:SUBLANES, :] = jnp.where(first, 0.0, filt_scr[:SUBLANES, :])
    sign = jnp.where(lax.broadcasted_iota(jnp.int32, (2 * n2, LANES), 0) < n2, 1.0, -1.0)

    def put_bwd(k1, x):
        h_out[k1] = h_out[k1] + sign * x

    _dft_forward(filt_scr, f1_ref, f2_ref, y_scr, put_bwd, dft)


def _hyena_features(n):
    t = np.linspace(0.0, 1.0, n, dtype=np.float32).astype(np.float64)[:, None]
    w = (2.0 * np.pi * np.arange(n, dtype=np.float64)[:, None] / n).astype(np.float32).astype(np.float64)
    bands = np.linspace(1e-4, HY_BANDS - 1, HY_BANDS, dtype=np.float32).astype(np.float64)[None, :]
    bw = (bands * w).astype(np.float32).astype(np.float64)
    z = np.concatenate([t, np.cos(bw), -np.sin(bw)], axis=-1)
    z = np.concatenate([z, np.zeros((n, LANES - z.shape[1]))], axis=-1)
    return jnp.asarray(z, F32)


def _hyena_filter_spectrum(n, w1, b1, w2, b2, w3, sin_freq, dft):
    ch = w3.shape[1] // 2
    hid = w1.shape[1]
    feat = _hyena_features(n)
    w1p = jnp.concatenate([w1.astype(F32), jnp.zeros((LANES - w1.shape[0], hid), F32)], axis=0)
    deltas = jnp.asarray(np.abs(np.linspace(HY_MIN_DECAY, HY_MAX_DECAY, ch, dtype=np.float32)), F32).reshape(1, ch)
    freq8 = jnp.concatenate([sin_freq.astype(F32), jnp.zeros((SUBLANES - 2, hid), F32)], axis=0)
    full = lambda a: pl.BlockSpec(a.shape, lambda c: (0,) * a.ndim, pipeline_mode=pl.Buffered(1))
    tiles = ch // LANES
    args = (feat, w1p, b1.reshape(1, hid).astype(F32), w2.astype(F32), b2.reshape(1, hid).astype(F32))
    return pl.pallas_call(
        functools.partial(_hyena_filter_kernel, dft=dft),
        grid=(tiles,),
        in_specs=[full(a) for a in args] + [
            pl.BlockSpec((hid, LANES), lambda c: (0, c)),
            pl.BlockSpec((hid, LANES), lambda c: (0, tiles + c)),
            full(freq8),
            pl.BlockSpec((1, LANES), lambda c: (0, c)),
            full(dft.f1), full(dft.f2),
        ],
        out_specs=pl.BlockSpec((dft.K1, 2 * dft.N2, LANES), lambda c: (0, 0, c)),
        out_shape=jax.ShapeDtypeStruct((dft.K1, 2 * dft.N2, ch), F32),
        scratch_shapes=[pltpu.VMEM((n, LANES), F32), pltpu.VMEM((dft.N2 * 2 * dft.K1p, LANES), F32),
                        pltpu.VMEM((n, hid), F32)],
        compiler_params=_cparams(("arbitrary",)),
        name="hyena_filter",
    )(*args, w3.astype(F32), w3.astype(F32), freq8, deltas, dft.f1, dft.f2)


def _hyena_conv_kernel(z_ref, h_ref, f1_ref, f2_ref, g2_ref, g1_ref, y_out, y_scr, x_scr, *, dft):
    n2, a_n, k1p = dft.N2, dft.A, dft.K1p

    def put(k1, x):
        hk = h_ref[k1]
        xr, xi, hr, hi = x[:n2], x[n2:], hk[:n2], hk[n2:]
        prod = jnp.concatenate([xr * hr - xi * hi, xr * hi + xi * hr], axis=0).astype(BF16)
        r0 = pl.multiple_of(k1 * 2 * n2, SUBLANES)
        x_scr[pl.ds(r0, 2 * n2), :] = _dot(g2_ref[...], prod)

    if k1p > dft.K1:
        x_scr[dft.K1 * 2 * n2:, :] = jnp.zeros(((k1p - dft.K1) * 2 * n2, LANES), F32)
    _dft_forward(z_ref, f1_ref, f2_ref, y_scr, put, dft)

    def last(b, _):
        yr = x_scr[pl.ds(b, k1p, stride=2 * n2), :]
        yi = x_scr[pl.ds(n2 + b, k1p, stride=2 * n2), :]
        y_out[pl.ds(b, a_n, stride=n2), :] = _dot(g1_ref[b], jnp.concatenate([yr, yi], axis=0).astype(BF16))
        return 0

    lax.fori_loop(0, n2, last, 0, unroll=DFT_UNROLL)


def _hyena_conv(z, spec, dft):
    b, n, ch = z.shape
    once = pl.Buffered(1)
    full = lambda a: pl.BlockSpec(a.shape, lambda c, bi: (0,) * a.ndim, pipeline_mode=once)
    io_spec = pl.BlockSpec((None, n, LANES), lambda c, bi: (bi, 0, c))
    return pl.pallas_call(
        functools.partial(_hyena_conv_kernel, dft=dft),
        grid=(ch // LANES, b),
        in_specs=[io_spec, pl.BlockSpec((dft.K1, 2 * dft.N2, LANES), lambda c, bi: (0, 0, c), pipeline_mode=once),
                  full(dft.f1), full(dft.f2), full(dft.g2), full(dft.g1)],
        out_specs=io_spec,
        out_shape=jax.ShapeDtypeStruct((b, n, ch), F32),
        scratch_shapes=[pltpu.VMEM((dft.N2 * 2 * dft.K1p, LANES), F32), pltpu.VMEM((dft.K1p * 2 * dft.N2, LANES), F32)],
        compiler_params=_cparams(("parallel", "arbitrary")),
        name="hyena_conv",
    )(z, spec, dft.f1, dft.f2, dft.g2, dft.g1)


def _lru_tile_terms(cur, prev8, next8, conv_w, conv_b, wg, bg, sp, reverse):
    rows, c = cur.shape
    xc = (conv_w[0:1] * _shift_down(cur, prev8, 2) + conv_w[1:2] * _shift_down(cur, prev8, 1)
          + conv_w[2:3] * cur + conv_w[3:4] * _shift_up(cur, next8, 1) + conv_b)
    g = _dot(xc.astype(BF16), wg) + bg
    r, ig = _sigmoid(g[:, :c]), _sigmoid(g[:, c:])
    log_a = -LRU_C * r * sp
    a = jnp.exp(log_a)
    th = jnp.tanh(log_a)
    bb = jnp.sqrt(-2.0 * th / (1.0 - th)) * (ig * xc)
    groups = rows // SUBLANES
    a, bb = a.reshape(groups, SUBLANES, c), bb.reshape(groups, SUBLANES, c)
    sub = lax.broadcasted_iota(jnp.int32, (groups, SUBLANES, c), 1)
    for k in (1, 2, 4):
        shift = SUBLANES - k if reverse else k
        ok = sub < SUBLANES - k if reverse else sub >= k
        a_s, b_s = pltpu.roll(a, shift, 1), pltpu.roll(bb, shift, 1)
        bb = jnp.where(ok, bb + a * b_s, bb)
        a = jnp.where(ok, a * a_s, a)
    return a.reshape(rows, c), bb.reshape(rows, c)


def _lru_kernel(ctx_ref, curf_ref, prevf_ref, nextf_ref, curb_ref, prevb_ref, nextb_ref,
                cw_ref, cb_ref, wgf_ref, wgb_ref, bgf_ref, bgb_ref, sp_ref,
                hf_out, hb_out, a_scr, b_scr, carry_scr, *, batch, tt):
    i = pl.program_id(0)
    nt = pl.num_programs(0)
    cw, cb = cw_ref[...], cb_ref[...]
    sp = sp_ref[...]
    c = cw.shape[1]
    params = ((wgf_ref, bgf_ref, sp[0:1]), (wgb_ref, bgb_ref, sp[1:2]))

    def carry_scan(rows, write):
        groups = rows // SUBLANES

        def body(j, carries):
            new = []
            for d in range(2):
                jj = j if d == 0 else groups - 1 - j
                r0 = pl.multiple_of(jj * SUBLANES, SUBLANES)
                for bi in range(batch):
                    h8 = b_scr[d, bi, pl.ds(r0, SUBLANES), :] + a_scr[d, bi, pl.ds(r0, SUBLANES), :] * carries[d * batch + bi]
                    write(d, bi, r0, h8)
                    edge = h8[SUBLANES - 1:SUBLANES] if d == 0 else h8[0:1]
                    new.append(jnp.broadcast_to(edge, (SUBLANES, c)))
            return tuple(new)

        init = tuple(carry_scr[d, bi] for d in range(2) for bi in range(batch))
        out = lax.fori_loop(0, groups, body, init)
        for d in range(2):
            for bi in range(batch):
                carry_scr[d, bi] = out[d * batch + bi]

    @pl.when(i == 0)
    def _():
        carry_scr[...] = jnp.zeros(carry_scr.shape, F32)
        nc = ctx_ref.shape[1]
        zero8 = jnp.zeros((SUBLANES, c), F32)
        for d in range(2):
            wg, bg, spd = params[d]
            for bi in range(batch):
                a, bb = _lru_tile_terms(ctx_ref[bi], zero8, zero8, cw, cb, wg[...], bg[...], spd, d == 1)
                a_scr[d, bi, :nc, :] = a
                b_scr[d, bi, :nc, :] = bb
        carry_scan(nc, lambda d, bi, r0, h8: None)

    tiles = ((curf_ref, prevf_ref, nextf_ref, i), (curb_ref, prevb_ref, nextb_ref, nt - 1 - i))
    for d in range(2):
        cur_ref, prev_ref, next_ref, ti = tiles[d]
        wg, bg, spd = params[d]
        has_prev = (ti > 0).astype(F32)
        has_next = (ti < nt - 1).astype(F32)
        for bi in range(batch):
            a, bb = _lru_tile_terms(cur_ref[bi], prev_ref[bi] * has_prev, next_ref[bi] * has_next,
                                    cw, cb, wg[...], bg[...], spd, d == 1)
            a_scr[d, bi, :tt, :] = a
            b_scr[d, bi, :tt, :] = bb

    def write(d, bi, r0, h8):
        if d == 0:
            hf_out[bi, pl.ds(r0, SUBLANES), :] = h8
        else:
            hb_out[bi, pl.ds(r0, SUBLANES), :] = h8

    carry_scan(tt, write)


def _lru_gate_weights(w_a, w_x):
    def dense(w):
        nb, bs = w.shape[0], w.shape[1]
        eye = jnp.eye(nb, dtype=F32)
        return jnp.einsum('hij,hk->hikj', w.astype(F32), eye).reshape(nb * bs, nb * bs)
    return [jnp.concatenate([dense(w_a[d]), dense(w_x[d])], axis=1).astype(BF16) for d in range(2)]


def _lru(lx_lat, lx_ctx, conv_w, conv_b, w_a, b_a, w_x, b_x, lam, *, tt):
    b, n, c = lx_lat.shape
    nc = lx_ctx.shape[1]
    nt = n // tt
    wgf, wgb = _lru_gate_weights(w_a, w_x)
    bgf = jnp.concatenate([b_a[0], b_x[0]]).reshape(1, 2 * c).astype(F32)
    bgb = jnp.concatenate([b_a[1], b_x[1]]).reshape(1, 2 * c).astype(F32)
    sp = jax.nn.softplus(-lam.astype(F32))
    sp8 = jnp.concatenate([sp, jnp.zeros((SUBLANES - 2, c), F32)], axis=0)
    cw8 = jnp.concatenate([conv_w.astype(F32), jnp.zeros((SUBLANES - conv_w.shape[0], c), F32)], axis=0)
    cb = conv_b.reshape(1, c).astype(F32)
    per = tt // SUBLANES
    last = n // SUBLANES - 1

    def specs(tile_of):
        cur = pl.BlockSpec((b, tt, c), lambda i: (0, tile_of(i), 0))
        prev = pl.BlockSpec((b, SUBLANES, c), lambda i: (0, jnp.maximum(tile_of(i) * per - 1, 0), 0))
        nxt = pl.BlockSpec((b, SUBLANES, c), lambda i: (0, jnp.minimum((tile_of(i) + 1) * per, last), 0))
        return [cur, prev, nxt]

    fwd_tile = lambda i: i
    bwd_tile = lambda i: nt - 1 - i
    full = lambda a: pl.BlockSpec(a.shape, lambda i: (0,) * a.ndim)
    rows_scr = max(tt, nc)
    return pl.pallas_call(
        functools.partial(_lru_kernel, batch=b, tt=tt),
        grid=(nt,),
        in_specs=[full(lx_ctx)] + specs(fwd_tile) + specs(bwd_tile)
        + [full(cw8), full(cb), full(wgf), full(wgb), full(bgf), full(bgb), full(sp8)],
        out_specs=[pl.BlockSpec((b, tt, c), lambda i: (0, i, 0)), pl.BlockSpec((b, tt, c), lambda i: (0, nt - 1 - i, 0))],
        out_shape=[jax.ShapeDtypeStruct((b, n, c), F32)] * 2,
        scratch_shapes=[pltpu.VMEM((2, b, rows_scr, c), F32), pltpu.VMEM((2, b, rows_scr, c), F32),
                        pltpu.VMEM((2, b, SUBLANES, c), F32)],
        compiler_params=_cparams(("arbitrary",)),
        name="rglru",
    )(lx_ctx, lx_lat, lx_lat, lx_lat, lx_lat, lx_lat, lx_lat, cw8, cb, wgf, wgb, bgf, bgb, sp8)


def _odd_out_kernel(x_ref, m_ref, yc_ref, z_ref, x0_ref, hb_ref, hf_ref, hbk_ref, lg_ref, wh_ref, wr_ref,
                    g2_ref, wg_ref, wu_ref, wd_ref, fn_ref, out_ref):
    z = z_ref[...]
    hy = x0_ref[...] * (yc_ref[...] + z * hb_ref[...])
    r = (hf_ref[...] + hbk_ref[...]) * _gelu_tanh(lg_ref[...])
    mix = _dot(hy.astype(BF16), wh_ref[...]) + _dot(r.astype(BF16), wr_ref[...])
    m = m_ref[...]
    y = _ffn_half_step(x_ref[...] + m[5:6] * mix, m, g2_ref, wg_ref, wu_ref, wd_ref, 2)
    out_ref[...] = _rms(y, fn_ref[...])


def _odd_out(tok, m, yc, z, x0, hy_bias, hf, hb, lg, w_out, g2, wg2, wu2, wd2, final_g, *, n_rows, tm, mod_row):
    d = tok.shape[1]
    c = yc.shape[1]
    wh, wr = w_out[:c].astype(BF16), w_out[c:].astype(BF16)
    hbias = hy_bias.reshape(1, c).astype(F32)
    gg2, fg = g2.reshape(1, d), final_g.reshape(1, d)
    full = lambda a: pl.BlockSpec(a.shape, lambda i: (0,) * a.ndim, pipeline_mode=pl.Buffered(1))
    row = lambda w: pl.BlockSpec((tm, w), lambda i: (i, 0))
    return pl.pallas_call(
        _odd_out_kernel,
        grid=(n_rows // tm,),
        in_specs=[row(d), pl.BlockSpec((None, N_MOD, d), lambda i: (mod_row(i), 0, 0)),
                  row(c), row(c), row(c), full(hbias), row(c), row(c), row(c), full(wh), full(wr),
                  full(gg2), full(wg2), full(wu2), full(wd2), full(fg)],
        out_specs=row(d),
        out_shape=jax.ShapeDtypeStruct((n_rows, d), F32),
        compiler_params=_cparams(("parallel",)),
        name="odd_out_ffn",
    )(tok, m, yc, z, x0, hbias, hf, hb, lg, wh, wr, gg2, wg2, wu2, wd2, fg)


def kernel(x, c, ctx, c_ctx, mod_w, mod_b, norm_ffn1, norm_mix, norm_ffn2, ffn1_w_gate, ffn1_w_up, ffn1_w_down, ffn2_w_gate, ffn2_w_up, ffn2_w_down, ev_w_in, mla_q_norm, mla_w_uq, mla_kv_norm, mla_w_ukv, s5_lambda_re, s5_lambda_im, s5_log_step, s5_b_re, s5_b_im, s5_c_re, s5_c_im, s5_d, s5_w_glu, s5_b_glu, ev_w_out, od_w_in, hy_conv_w, hy_conv_b, hy_filt_w1, hy_filt_b1, hy_filt_w2, hy_filt_b2, hy_filt_w3, hy_sin_freq, hy_bias, lru_conv_w, lru_conv_b, lru_w_a, lru_b_a, lru_w_x, lru_b_x, lru_lambda, od_w_out, final_norm):
    b, n, d = x.shape
    nc = ctx.shape[1]
    depth = mod_w.shape[0]
    assert depth == 2, "the trunk is laid out for one even and one odd layer"
    n_lat_rows = b * n
    t = n_lat_rows + b * nc
    tm = _row_tile(n, b * nc)
    tpb = n // tm
    mod_row = lambda i: jnp.minimum(i // tpb, b)

    cond8 = jnp.concatenate([c, c_ctx[None, :], jnp.zeros((SUBLANES - b - 1, d), F32)], axis=0)
    m = _adaln(cond8, mod_w, mod_b)
    bf = lambda w: w.astype(BF16)

    ev_w = _even_in_weights(ev_w_in[0], mla_w_uq[0], mla_w_ukv[0])
    ffn1 = (norm_ffn1[0], bf(ffn1_w_gate[0]), bf(ffn1_w_up[0]), bf(ffn1_w_down[0]))
    tok, qt, k, vt, s_chunks = _even_in(x.reshape(n_lat_rows, d), ctx.reshape(b * nc, d), m[0], ffn1, norm_mix[0],
                                        ev_w, mla_q_norm[0], mla_kv_norm[0], n=n, tm=tm, mod_row=mod_row)
    o = _attention(qt, k, vt, b=b, n=n, nc=nc)
    s5_ops = _s5_operators(s5_lambda_re[0], s5_lambda_im[0], s5_log_step[0], s5_b_re[0], s5_b_im[0],
                           s5_c_re[0], s5_c_im[0], s5_d[0])
    ys = _s5(s_chunks, s5_ops, b=b, n=n, nc=nc)
    tok = _even_out(tok, m[0], o, ys, s5_w_glu[0], s5_b_glu[0], ev_w_out[0], norm_ffn2[0], bf(ffn2_w_gate[0]),
                    bf(ffn2_w_up[0]), bf(ffn2_w_down[0]), tm=tm, mod_row=mod_row)

    hy_w = hy_conv_w.shape[2]
    lru_w = lru_conv_w.shape[2]
    ffn1 = (norm_ffn1[1], bf(ffn1_w_gate[1]), bf(ffn1_w_up[1]), bf(ffn1_w_down[1]))
    tok, hy, lx, lg, lx_ctx = _odd_in(tok, m[1], ffn1, norm_mix[1], od_w_in[0], hy_w=hy_w, lru_w=lru_w, tm=tm,
                                      mod_row=mod_row, n_lat_rows=n_lat_rows)
    tt = _row_tile(n)
    z, x0 = _hyena_prep(hy.reshape(b, n, hy_w), hy_conv_w[0], hy_conv_b[0], tt=tt)
    dft = _Dft(n)
    spec = _hyena_filter_spectrum(n, hy_filt_w1[0], hy_filt_b1[0], hy_filt_w2[0], hy_filt_b2[0], hy_filt_w3[0],
                                  hy_sin_freq[0], dft)
    yc = _hyena_conv(z, spec, dft)
    hf, hb = _lru(lx.reshape(b, n, lru_w), lx_ctx.reshape(b, nc, lru_w), lru_conv_w[0],
                  lru_conv_b[0], lru_w_a[0], lru_b_a[0], lru_w_x[0], lru_b_x[0], lru_lambda[0], tt=min(tt, 256))
    flat = lambda a: a.reshape(n_lat_rows, a.shape[2])
    lat = _odd_out(tok, m[1], flat(yc), flat(z), flat(x0), hy_bias[0], flat(hf), flat(hb), lg, od_w_out[0],
                   norm_ffn2[1], bf(ffn2_w_gate[1]), bf(ffn2_w_up[1]), bf(ffn2_w_down[1]), final_norm,
                   n_rows=n_lat_rows, tm=tm, mod_row=mod_row)
    return lat.reshape(b, n, d)
```

```python
import functools
import math

import numpy as np
import jax
import jax.numpy as jnp
from jax import lax
from jax.experimental import pallas as pl
from jax.experimental.pallas import tpu as pltpu

F32 = jnp.float32
BF16 = jnp.bfloat16

GRID_W = 64
N_MOD = 9
NORM_EPS = 1e-6
ROPE_BASE = 10000.0
MLA_HEADS = 8
QK_NOPE = 64
QK_ROPE = 32
V_DIM = 64
Q_RANK = 384
KV_RANK = 256
S5_GROUP = 16
S5_STATE = 64
HY_BANDS = 16
HY_DECAY_TARGET = 1e-2
HY_MIN_DECAY = math.log(HY_DECAY_TARGET) / 1.5
HY_MAX_DECAY = math.log(HY_DECAY_TARGET) / 0.3
LRU_BLOCKS = 8
LRU_C = 8.0

LANES = 128
SUBLANES = 8
VMEM_BYTES_V7X = 64 * 1024 * 1024
VMEM_LIMIT = VMEM_BYTES_V7X - 8 * 1024 * 1024

HEAD_LANES = LANES
ONES_LANE = V_DIM
ATT_V_ROWS = V_DIM + 16
ATT_Q_TILE = 2048
ATT_Q_LANES = 256
S5_CHUNK = 8
S5_SCAN_UNROLL = 8
FFT_N2 = 128
DFT_UNROLL = 8
FFN_CHUNK = 256
NEG_BIG = -1e30


def _cparams(sem):
    return pltpu.CompilerParams(dimension_semantics=sem, vmem_limit_bytes=VMEM_LIMIT)


def _dot(a, b):
    return jnp.dot(a, b, preferred_element_type=F32)


def _split(x):
    hi = x.astype(BF16)
    lo = (x - hi.astype(F32)).astype(BF16)
    return hi, lo


def _dot3(a_hi, a_lo, b_hi, b_lo):
    return _dot(a_hi, b_hi) + _dot(a_lo, b_hi) + _dot(a_hi, b_lo)


def _sigmoid(x):
    return 1.0 / (1.0 + jnp.exp(-x))


def _gelu_tanh(x):
    return 0.5 * x * (1.0 + jnp.tanh(math.sqrt(2.0 / math.pi) * (x + 0.044715 * (x * x * x))))


def _rms(x, g):
    return x * lax.rsqrt(jnp.mean(x * x, axis=-1, keepdims=True) + NORM_EPS) * g


def _row_tile(*lengths):
    for t in (512, 256, 128, 64, 32, 16, 8):
        if all(l % t == 0 for l in lengths):
            return t
    raise ValueError(f"no row tile divides {lengths}")


def _adaln_kernel(c_ref, w_ref, b_ref, o_ref):
    c = c_ref[...]
    s_hi, s_lo = _split(c * _sigmoid(c))
    w_hi, w_lo = _split(w_ref[...])
    o_ref[...] = _dot3(s_hi, s_lo, w_hi, w_lo) + b_ref[...]


def _adaln(cond8, mod_w, mod_b):
    depth, d, nd = mod_w.shape
    tn = d
    out = pl.pallas_call(
        _adaln_kernel,
        grid=(depth, nd // tn),
        in_specs=[
            pl.BlockSpec((SUBLANES, d), lambda l, j: (0, 0)),
            pl.BlockSpec((None, d, tn), lambda l, j: (l, 0, j)),
            pl.BlockSpec((None, 1, tn), lambda l, j: (l, 0, j)),
        ],
        out_specs=pl.BlockSpec((None, SUBLANES, tn), lambda l, j: (l, 0, j)),
        out_shape=jax.ShapeDtypeStruct((depth, SUBLANES, nd), F32),
        compiler_params=_cparams(("arbitrary", "arbitrary")),
        name="adaln",
    )(cond8, mod_w, mod_b.reshape(depth, 1, nd))
    return out.reshape(depth, SUBLANES, N_MOD, d)


def _ffn_half_step(x, m, g_ref, wg_ref, wu_ref, wd_ref, k):
    shift, scale, gate = m[3 * k:3 * k + 1], m[3 * k + 1:3 * k + 2], m[3 * k + 2:3 * k + 3]
    hb = (_rms(x, g_ref[...]) * (1.0 + scale) + shift).astype(BF16)
    f_all = wg_ref.shape[1]
    tf = FFN_CHUNK if f_all % FFN_CHUNK == 0 else LANES
    acc = jnp.zeros(x.shape, F32)
    for f in range(f_all // tf):
        sl = slice(f * tf, (f + 1) * tf)
        gt = _dot(hb, wg_ref[:, sl])
        up = _dot(hb, wu_ref[:, sl])
        acc = acc + _dot((gt * _sigmoid(gt) * up).astype(BF16), wd_ref[sl, :])
    return x + (0.5 * gate) * acc


def _even_in_kernel(x_ref, tail_ref, m_ref, g1_ref, wg_ref, wu_ref, wd_ref,
                    g_ref, w1_ref, qn_ref, kvn_ref, wq_ref, wqp_ref, wk_ref, wv_ref,
                    c_ref, s_ref, tok_out, qt_out, k_out, vt_out, u_out, s_scr, *, tiles_first):
    m = m_ref[...]
    x = jnp.where(pl.program_id(0) < tiles_first, x_ref[...], tail_ref[...])
    x = _ffn_half_step(x, m, g1_ref, wg_ref, wu_ref, wd_ref, 0)
    tok_out[...] = x
    hb = (_rms(x, g_ref[...]) * (1.0 + m[4:5]) + m[3:4]).astype(BF16)
    p = _dot(hb, w1_ref[...])
    o = 0
    cq = p[:, o:o + Q_RANK]; o += Q_RANK
    ckv = p[:, o:o + KV_RANK]; o += KV_RANK
    kra = p[:, o:o + HEAD_LANES]; o += HEAD_LANES
    krb = p[:, o:o + HEAD_LANES]; o += HEAD_LANES
    pw = 2 * S5_GROUP
    per_tile = LANES // pw
    chunks = s_scr.shape[1] // S5_CHUNK
    for lt in range(s_scr.shape[0]):
        s_scr[lt] = p[:, o + lt * LANES:o + (lt + 1) * LANES]
        by_tok = [s_scr[lt, pl.ds(t, chunks, stride=S5_CHUNK), :] for t in range(S5_CHUNK)]
        for l in range(per_tile):
            u_out[lt * per_tile + l] = jnp.concatenate([r[:, l * pw:(l + 1) * pw] for r in by_tok], axis=-1).astype(BF16)
    cos, sin = c_ref[...], s_ref[...]
    cqn = _rms(cq, qn_ref[...]).astype(BF16)
    qa = _dot(cqn, wq_ref[...])
    qb = _dot(cqn, wqp_ref[...])
    ckvn = _rms(ckv, kvn_ref[...]).astype(BF16)
    kn = _dot(ckvn, wk_ref[...])
    vv = _dot(ckvn, wv_ref[...])
    kr = kra * cos + krb * sin
    ones = (lax.broadcasted_iota(jnp.int32, (1, HEAD_LANES), 1) == ONES_LANE).astype(F32)
    scale = (QK_NOPE + QK_ROPE) ** -0.5 * math.log2(math.e)
    for h in range(MLA_HEADS):
        sl = slice(h * HEAD_LANES, (h + 1) * HEAD_LANES)
        qt_out[h] = ((qa[:, sl] * cos + qb[:, sl] * sin) * scale).T.astype(BF16)
        k_out[h] = (kn[:, sl] + kr).astype(BF16)
        vt_out[h] = (vv[:, sl] + ones).T.astype(BF16)


def _rope_partner(w):
    half = w.shape[-1] // 2
    return jnp.concatenate([-w[..., half:], w[..., :half]], axis=-1)


def _even_in_weights(w_in, w_uq, w_ukv):
    d = w_in.shape[0]
    o = 0
    w_cq = w_in[:, o:o + Q_RANK]; o += Q_RANK
    w_ckv = w_in[:, o:o + KV_RANK]; o += KV_RANK
    w_kr = w_in[:, o:o + QK_ROPE]; o += QK_ROPE
    w_s = w_in[:, o:]
    pad_l = jnp.zeros((d, QK_NOPE), F32)
    pad_r = jnp.zeros((d, HEAD_LANES - QK_NOPE - QK_ROPE), F32)
    kr_blk = jnp.concatenate([pad_l, w_kr, pad_r], axis=1)
    kr_blk_p = jnp.concatenate([pad_l, _rope_partner(w_kr), pad_r], axis=1)
    w1 = jnp.concatenate([w_cq, w_ckv, kr_blk, kr_blk_p, w_s], axis=1).astype(BF16)

    dk = QK_NOPE + QK_ROPE
    wq = w_uq.reshape(Q_RANK, MLA_HEADS, dk)
    zq = jnp.zeros((Q_RANK, MLA_HEADS, HEAD_LANES - dk), F32)
    wq_pad = jnp.concatenate([wq, zq], axis=-1)
    wq_par = jnp.concatenate([jnp.zeros((Q_RANK, MLA_HEADS, QK_NOPE), F32), _rope_partner(wq[..., QK_NOPE:]), zq], axis=-1)
    wkv = w_ukv.reshape(KV_RANK, MLA_HEADS, QK_NOPE + V_DIM)
    wk = jnp.concatenate([wkv[..., :QK_NOPE], jnp.zeros((KV_RANK, MLA_HEADS, HEAD_LANES - QK_NOPE), F32)], axis=-1)
    wv = jnp.concatenate([wkv[..., QK_NOPE:], jnp.zeros((KV_RANK, MLA_HEADS, HEAD_LANES - V_DIM), F32)], axis=-1)
    flat = lambda w: w.reshape(w.shape[0], MLA_HEADS * HEAD_LANES).astype(BF16)
    return w1, flat(wq_pad), flat(wq_par), flat(wk), flat(wv)


def _rope_tables(n, n_ident):
    rows = n // GRID_W
    row = np.repeat(np.arange(rows, dtype=np.float32), GRID_W)
    col = np.tile(np.arange(GRID_W, dtype=np.float32), rows)
    n_freq = QK_ROPE // 4
    inv_freq = (np.float32(ROPE_BASE) ** (-np.arange(n_freq, dtype=np.float32) / n_freq)).astype(np.float32)
    ang = np.concatenate([row[:, None] * inv_freq, col[:, None] * inv_freq], axis=-1).astype(np.float64)
    ang = np.concatenate([ang, np.zeros((n_ident, QK_ROPE // 2))], axis=0)
    tot = n + n_ident
    pad = HEAD_LANES - QK_NOPE - QK_ROPE
    cos = np.concatenate([np.ones((tot, QK_NOPE)), np.cos(ang), np.cos(ang), np.ones((tot, pad))], axis=1)
    sin = np.concatenate([np.zeros((tot, QK_NOPE)), np.sin(ang), np.sin(ang), np.zeros((tot, pad))], axis=1)
    return jnp.asarray(cos, F32), jnp.asarray(sin, F32)


def _even_in(lat, tail, m, ffn1, g, weights, q_norm, kv_norm, *, n, tm, mod_row):
    n_lat_rows, d = lat.shape
    t = n_lat_rows + tail.shape[0]
    g1, wg1, wu1, wd1 = ffn1
    gg1 = g1.reshape(1, d)
    w1, wq, wqp, wk, wv = weights
    cos, sin = _rope_tables(n, tm)
    tiles_lat = n_lat_rows // tm
    tpb = n // tm
    tab_row = lambda i: jnp.where(i < tiles_lat, i % tpb, tpb)
    full = lambda a: pl.BlockSpec(a.shape, lambda i: (0,) * a.ndim, pipeline_mode=pl.Buffered(1))
    hl = MLA_HEADS * HEAD_LANES
    s_ch = w1.shape[1] - Q_RANK - KV_RANK - 2 * HEAD_LANES
    qn, kvn = q_norm.reshape(1, Q_RANK), kv_norm.reshape(1, KV_RANK)
    gg = g.reshape(1, d)
    head_spec = pl.BlockSpec((MLA_HEADS, tm, HEAD_LANES), lambda i: (0, i, 0))
    head_shape = jax.ShapeDtypeStruct((MLA_HEADS, t, HEAD_LANES), BF16)
    head_t_spec = pl.BlockSpec((MLA_HEADS, HEAD_LANES, tm), lambda i: (0, 0, i))
    head_t_shape = jax.ShapeDtypeStruct((MLA_HEADS, HEAD_LANES, t), BF16)
    pairs = s_ch // (2 * S5_GROUP)
    u_w = 2 * S5_GROUP * S5_CHUNK
    return pl.pallas_call(
        functools.partial(_even_in_kernel, tiles_first=tiles_lat),
        grid=(t // tm,),
        in_specs=[
            pl.BlockSpec((tm, d), lambda i: (jnp.minimum(i, tiles_lat - 1), 0)),
            pl.BlockSpec((tm, d), lambda i: (jnp.maximum(i - tiles_lat, 0), 0)),
            pl.BlockSpec((None, N_MOD, d), lambda i: (mod_row(i), 0, 0)),
            full(gg1), full(wg1), full(wu1), full(wd1),
            full(gg), full(w1), full(qn), full(kvn), full(wq), full(wqp), full(wk), full(wv),
            pl.BlockSpec((tm, HEAD_LANES), lambda i: (tab_row(i), 0)),
            pl.BlockSpec((tm, HEAD_LANES), lambda i: (tab_row(i), 0)),
        ],
        out_specs=[pl.BlockSpec((tm, d), lambda i: (i, 0)), head_t_spec, head_spec, head_t_spec,
                   pl.BlockSpec((pairs, tm // S5_CHUNK, u_w), lambda i: (0, i, 0))],
        out_shape=[jax.ShapeDtypeStruct((t, d), F32), head_t_shape, head_shape, head_t_shape,
                   jax.ShapeDtypeStruct((pairs, t // S5_CHUNK, u_w), BF16)],
        scratch_shapes=[pltpu.VMEM((s_ch // LANES, tm, LANES), F32)],
        compiler_params=_cparams(("parallel",)),
        name="ffn_even_in",
    )(lat, tail, m, gg1, wg1, wu1, wd1, gg, w1, qn, kvn, wq, wqp, wk, wv, cos, sin)


def _attn_kernel(qt_ref, kc_ref, vtc_ref, *rest, tk, lat):
    if lat:
        kl_ref, vtl_ref, o_ref = rest
    else:
        (o_ref,) = rest
    tq = qt_ref.shape[1]
    lanes = min(ATT_Q_LANES, tq)
    subs = [slice(j * lanes, (j + 1) * lanes) for j in range(tq // lanes)]

    def step(k, vt, carry):
        scores = [_dot(k, qt_ref[:, sl]) for sl in subs]
        out = []
        for s, (m, acc) in zip(scores, carry):
            m_new = jnp.maximum(m, jnp.max(s, axis=0, keepdims=True))
            p = jnp.exp2(s - m_new).astype(BF16)
            out.append((m_new, jnp.exp2(m - m_new) * acc + _dot(vt, p)))
        return tuple(out)

    carry = tuple((jnp.full((1, lanes), NEG_BIG, F32), jnp.zeros((ATT_V_ROWS, lanes), F32)) for _ in subs)
    carry = step(kc_ref[...], vtc_ref[:ATT_V_ROWS, :], carry)
    if lat:
        def body(c, carry):
            start = pl.multiple_of(c * tk, tk)
            return step(kl_ref[pl.ds(start, tk), :], vtl_ref[:ATT_V_ROWS, pl.ds(start, tk)], carry)

        n_chunks = kl_ref.shape[0] // tk
        carry = lax.fori_loop(0, n_chunks, body, carry, unroll=2 if n_chunks % 2 == 0 else 1)
    for sl, (_, acc) in zip(subs, carry):
        o = acc / acc[ONES_LANE:ONES_LANE + 1]
        o = jnp.concatenate([o, jnp.zeros((HEAD_LANES - ATT_V_ROWS, lanes), F32)], axis=0)
        o_ref[sl, :] = o.T.astype(BF16)


def _attention(qt, k, vt, *, b, n, nc):
    h, t, _ = k.shape
    tq = min(ATT_Q_TILE, n)
    n_q = n // tq
    tk = _row_tile(n)
    lat_blocks = (b * n) // nc
    o_lat = pl.pallas_call(
        functools.partial(_attn_kernel, tk=tk, lat=True),
        grid=(b, h, n_q),
        in_specs=[
            pl.BlockSpec((None, HEAD_LANES, tq), lambda bi, hi, i: (hi, 0, bi * n_q + i)),
            pl.BlockSpec((None, nc, HEAD_LANES), lambda bi, hi, i: (hi, lat_blocks + bi, 0)),
            pl.BlockSpec((None, HEAD_LANES, nc), lambda bi, hi, i: (hi, 0, lat_blocks + bi)),
            pl.BlockSpec((None, n, HEAD_LANES), lambda bi, hi, i: (hi, bi, 0)),
            pl.BlockSpec((None, HEAD_LANES, n), lambda bi, hi, i: (hi, 0, bi)),
        ],
        out_specs=pl.BlockSpec((None, tq, HEAD_LANES), lambda bi, hi, i: (hi, bi * n_q + i, 0)),
        out_shape=jax.ShapeDtypeStruct((h, b * n, HEAD_LANES), BF16),
        compiler_params=_cparams(("parallel", "parallel", "arbitrary")),
        name="attention",
    )(qt, k, vt, k, vt)
    o_ctx = pl.pallas_call(
        functools.partial(_attn_kernel, tk=tk, lat=False),
        grid=(b, h),
        in_specs=[
            pl.BlockSpec((None, HEAD_LANES, nc), lambda bi, hi: (hi, 0, lat_blocks + bi)),
            pl.BlockSpec((None, nc, HEAD_LANES), lambda bi, hi: (hi, lat_blocks + bi, 0)),
            pl.BlockSpec((None, HEAD_LANES, nc), lambda bi, hi: (hi, 0, lat_blocks + bi)),
        ],
        out_specs=pl.BlockSpec((None, nc, HEAD_LANES), lambda bi, hi: (hi, bi, 0)),
        out_shape=jax.ShapeDtypeStruct((h, b * nc, HEAD_LANES), BF16),
        compiler_params=_cparams(("parallel", "parallel")),
        name="attention_ctx",
    )(qt, k, vt)
    return o_lat, o_ctx


def _s5_operators(lam_re, lam_im, log_step, b_re, b_im, c_re, c_im, d_skip):
    tc = S5_CHUNK
    g_n, p_n = lam_re.shape[1], lam_re.shape[2]
    lam = lax.complex(lam_re.astype(F32), lam_im.astype(F32))
    step = jnp.exp(log_step.astype(F32))[..., None]
    la = lam * step
    a_bar = jnp.exp(la)
    bb = ((a_bar - 1.0) / lam)[..., None] * lax.complex(b_re.astype(F32), b_im.astype(F32))
    cm = lax.complex(c_re.astype(F32), c_im.astype(F32))
    kk = jnp.arange(tc + 1, dtype=F32)
    apow = jnp.exp(la[..., None] * kk)

    win_f = jnp.einsum('gps,gpc->gscp', apow[0][..., tc - 1::-1][..., :tc], bb[0])
    win_b = jnp.einsum('gps,gpc->gscp', apow[1][..., :tc], bb[1])
    wout_f = jnp.einsum('gcp,gpt->gptc', cm[0], apow[0][..., 1:])
    wout_b = jnp.einsum('gcp,gpt->gptc', cm[1], apow[1][..., tc:0:-1])
    kf = jnp.real(jnp.einsum('gop,gpk,gpi->gkoi', cm[0], apow[0][..., :tc], bb[0]))
    kb = jnp.real(jnp.einsum('gop,gpk,gpi->gkoi', cm[1], apow[1][..., :tc], bb[1]))
    s_idx = jnp.arange(tc)[:, None]
    t_idx = jnp.arange(tc)[None, :]
    lag = t_idx - s_idx
    kf_st = jnp.where((lag >= 0)[None, :, :, None, None], kf[:, jnp.clip(lag, 0, tc - 1)], 0.0)
    kb_st = jnp.where((lag <= 0)[None, :, :, None, None], kb[:, jnp.clip(-lag, 0, tc - 1)], 0.0)
    eye_t = jnp.eye(tc, dtype=F32)[None, :, :, None, None]
    eye_c = jnp.eye(S5_GROUP, dtype=F32)[None, None, None]
    dsk = d_skip.astype(F32).reshape(g_n, 1, 1, S5_GROUP, 1)
    loc = kf_st + kb_st + eye_t * eye_c * dsk
    loc = jnp.transpose(loc, (0, 1, 4, 2, 3))

    gp = g_n // 2
    gc2 = 2 * tc * S5_GROUP
    eye2 = jnp.eye(2, dtype=F32)
    pair = lambda x: x.reshape((gp, 2) + x.shape[1:])
    w_in_part = lambda w: jnp.einsum('qgscp,gh->qsgchp', pair(w), eye2).reshape(gp, gc2, 2 * p_n)
    w_out_part = lambda w: jnp.einsum('qgptc,gh->qgpthc', pair(w), eye2).reshape(gp, 2 * p_n, gc2)
    w_in = jnp.concatenate([w_in_part(jnp.real(win_f)), w_in_part(jnp.imag(win_f)),
                            w_in_part(jnp.real(win_b)), w_in_part(jnp.imag(win_b))], axis=2)
    w_out = jnp.concatenate([w_out_part(jnp.real(wout_f)), w_out_part(-jnp.imag(wout_f)),
                             w_out_part(jnp.real(wout_b)), w_out_part(-jnp.imag(wout_b))], axis=1)
    loc = jnp.einsum('qgsito,gh->qsgitho', pair(loc), eye2).reshape(gp, gc2, gc2)
    a_tc = apow[..., tc].reshape(2, gp, 2 * p_n)
    a_rows = jnp.stack([jnp.real(a_tc[0]), jnp.imag(a_tc[0]), jnp.real(a_tc[1]), jnp.imag(a_tc[1])], axis=1)
    a_rows = jnp.concatenate([a_rows, jnp.zeros((gp, SUBLANES - 4, 2 * p_n), F32)], axis=1)
    return w_in.astype(BF16), w_out.astype(BF16), loc.astype(BF16), a_rows


def _s5_kernel(u_ref, win_ref, wout_ref, loc_ref, a_ref, y_ref, s_scr, h_scr, *, batch, lat_chunks, ctx_chunks):
    u = u_ref[...]
    w = a_ref.shape[1]
    n_parts = s_scr.shape[0]
    s_all = _dot(u, win_ref[...])
    for k in range(n_parts):
        s_scr[k] = s_all[:, k * w:(k + 1) * w]
    a = a_ref[...]
    bc = lambda r: jnp.broadcast_to(a[r:r + 1], (batch, w))
    arf, aif, arb, aib = bc(0), bc(1), bc(2), bc(3)

    def chunk(rows, carry, part, ar, ai):
        hr, hi = carry
        sr = s_scr[part, rows, :]
        si = s_scr[part + 1, rows, :]
        h_scr[part, rows, :] = hr
        h_scr[part + 1, rows, :] = hi
        return ar * hr - ai * hi + sr, ar * hi + ai * hr + si

    def segment(base, per):
        def body(j, carry):
            cf, cb = carry
            cf = chunk(pl.ds(base + j, batch, stride=per), cf, 0, arf, aif)
            cb = chunk(pl.ds(base + per - 1 - j, batch, stride=per), cb, 2, arb, aib)
            return cf, cb
        return body

    zero = (jnp.zeros((batch, w), F32), jnp.zeros((batch, w), F32))
    unroll = lambda trips: S5_SCAN_UNROLL if trips % S5_SCAN_UNROLL == 0 else 1
    carry = lax.fori_loop(0, ctx_chunks, segment(batch * lat_chunks, ctx_chunks), (zero, zero),
                          unroll=unroll(ctx_chunks))
    lax.fori_loop(0, lat_chunks, segment(0, lat_chunks), carry, unroll=unroll(lat_chunks))
    h_all = jnp.concatenate([h_scr[k] for k in range(n_parts)], axis=-1).astype(BF16)
    y_ref[...] = _dot(u, loc_ref[...]) + _dot(h_all, wout_ref[...])


def _s5(u, ops, *, b, n, nc):
    w_in, w_out, loc, a_rows = ops
    gp, r, wid = u.shape
    sw = w_in.shape[2]
    return pl.pallas_call(
        functools.partial(_s5_kernel, batch=b, lat_chunks=n // S5_CHUNK, ctx_chunks=nc // S5_CHUNK),
        grid=(gp,),
        in_specs=[
            pl.BlockSpec((None, r, wid), lambda g: (g, 0, 0)),
            pl.BlockSpec((None, wid, sw), lambda g: (g, 0, 0)),
            pl.BlockSpec((None, sw, wid), lambda g: (g, 0, 0)),
            pl.BlockSpec((None, wid, wid), lambda g: (g, 0, 0)),
            pl.BlockSpec((None, SUBLANES, sw // 4), lambda g: (g, 0, 0)),
        ],
        out_specs=pl.BlockSpec((None, r, wid), lambda g: (g, 0, 0)),
        out_shape=jax.ShapeDtypeStruct((gp, r, wid), F32),
        scratch_shapes=[pltpu.VMEM((4, r, sw // 4), F32), pltpu.VMEM((4, r, sw // 4), F32)],
        compiler_params=_cparams(("parallel",)),
        name="s5",
    )(u, w_in, w_out, loc, a_rows)


def _even_out_kernel(x_ref, m_ref, ol_ref, oc_ref, yc_ref, wglu_ref, bglu_ref, woo_ref, wos_ref,
                     g2_ref, wg_ref, wu_ref, wd_ref, out_ref, ys_scr, *, tiles_lat):
    pw = 2 * S5_GROUP
    per_tile = LANES // pw
    chunks = yc_ref.shape[1]
    for lt in range(ys_scr.shape[0]):
        by_pair = [yc_ref[lt * per_tile + l] for l in range(per_tile)]
        for t in range(S5_CHUNK):
            ys_scr[lt, pl.ds(t, chunks, stride=S5_CHUNK), :] = jnp.concatenate(
                [y[:, t * pw:(t + 1) * pw] for y in by_pair], axis=-1)
    y = _gelu_tanh(jnp.concatenate([ys_scr[lt] for lt in range(ys_scr.shape[0])], axis=-1))
    y = y * _sigmoid(_dot(y.astype(BF16), wglu_ref[...]) + bglu_ref[...])
    is_lat = pl.program_id(0) < tiles_lat
    oc = jnp.concatenate([jnp.where(is_lat, ol_ref[h], oc_ref[h]) for h in range(MLA_HEADS)], axis=-1)
    mix = _dot(oc, woo_ref[...]) + _dot(y.astype(BF16), wos_ref[...])
    m = m_ref[...]
    out_ref[...] = _ffn_half_step(x_ref[...] + m[5:6] * mix, m, g2_ref, wg_ref, wu_ref, wd_ref, 2)


def _even_out(tok, m, o_lat, o_ctx, yc, w_glu, b_glu, w_out, g2, wg2, wu2, wd2, *, tm, mod_row):
    t, d = tok.shape
    tiles_lat = o_lat.shape[1] // tm
    pairs, _, wid = yc.shape
    sc = pairs * 2 * S5_GROUP
    wo = w_out[:MLA_HEADS * V_DIM].reshape(MLA_HEADS, V_DIM, d)
    woo = jnp.concatenate([wo, jnp.zeros((MLA_HEADS, HEAD_LANES - V_DIM, d), F32)], axis=1)
    woo = woo.reshape(MLA_HEADS * HEAD_LANES, d).astype(BF16)
    wos = w_out[MLA_HEADS * V_DIM:].astype(BF16)
    wg = w_glu.astype(BF16)
    bg = b_glu.reshape(1, sc).astype(F32)
    gg2 = g2.reshape(1, d)
    full = lambda a: pl.BlockSpec(a.shape, lambda i: (0,) * a.ndim, pipeline_mode=pl.Buffered(1))
    return pl.pallas_call(
        functools.partial(_even_out_kernel, tiles_lat=tiles_lat),
        grid=(t // tm,),
        in_specs=[
            pl.BlockSpec((tm, d), lambda i: (i, 0)),
            pl.BlockSpec((None, N_MOD, d), lambda i: (mod_row(i), 0, 0)),
            pl.BlockSpec((MLA_HEADS, tm, HEAD_LANES), lambda i: (0, jnp.minimum(i, tiles_lat - 1), 0)),
            pl.BlockSpec((MLA_HEADS, tm, HEAD_LANES), lambda i: (0, jnp.maximum(i - tiles_lat, 0), 0)),
            pl.BlockSpec((pairs, tm // S5_CHUNK, wid), lambda i: (0, i, 0)),
            full(wg), full(bg), full(woo), full(wos), full(gg2), full(wg2), full(wu2), full(wd2),
        ],
        out_specs=pl.BlockSpec((tm, d), lambda i: (i, 0)),
        out_shape=jax.ShapeDtypeStruct((t, d), F32),
        scratch_shapes=[pltpu.VMEM((sc // LANES, tm, LANES), F32)],
        compiler_params=_cparams(("parallel",)),
        name="even_out_ffn",
    )(tok, m, o_lat, o_ctx, yc, wg, bg, woo, wos, gg2, wg2, wu2, wd2)


def _odd_in_kernel(x_ref, m_ref, g1_ref, wg_ref, wu_ref, wd_ref, g_ref, w_ref,
                   tok_out, hy_out, lx_out, lg_out, lxc_out, *, tiles_lat):
    m = m_ref[...]
    x = _ffn_half_step(x_ref[...], m, g1_ref, wg_ref, wu_ref, wd_ref, 0)
    hb = (_rms(x, g_ref[...]) * (1.0 + m[4:5]) + m[3:4]).astype(BF16)
    n_hy, n_lx = hy_out.shape[1], lx_out.shape[1]
    is_lat = pl.program_id(0) < tiles_lat

    @pl.when(is_lat)
    def _():
        tok_out[...] = x
        p = _dot(hb, w_ref[...])
        hy_out[...] = p[:, :n_hy]
        lx_out[...] = p[:, n_hy:n_hy + n_lx]
        lg_out[...] = p[:, n_hy + n_lx:]

    @pl.when(jnp.logical_not(is_lat))
    def _():
        lxc_out[...] = _dot(hb, w_ref[:, n_hy:n_hy + n_lx])


def _odd_in(tok, m, ffn1, g, w_in, *, hy_w, lru_w, tm, mod_row, n_lat_rows):
    t, d = tok.shape
    g1, wg1, wu1, wd1 = ffn1
    gg1 = g1.reshape(1, d)
    w = w_in.astype(BF16)
    gg = g.reshape(1, d)
    tiles_lat = n_lat_rows // tm
    full = lambda a: pl.BlockSpec(a.shape, lambda i: (0,) * a.ndim, pipeline_mode=pl.Buffered(1))
    lat_row = lambda c: pl.BlockSpec((tm, c), lambda i: (jnp.minimum(i, tiles_lat - 1), 0))
    ctx_row = lambda c: pl.BlockSpec((tm, c), lambda i: (jnp.maximum(i - tiles_lat, 0), 0))
    return pl.pallas_call(
        functools.partial(_odd_in_kernel, tiles_lat=tiles_lat),
        grid=(t // tm,),
        in_specs=[pl.BlockSpec((tm, d), lambda i: (i, 0)),
                  pl.BlockSpec((None, N_MOD, d), lambda i: (mod_row(i), 0, 0)),
                  full(gg1), full(wg1), full(wu1), full(wd1), full(gg), full(w)],
        out_specs=[lat_row(d), lat_row(hy_w), lat_row(lru_w), lat_row(lru_w), ctx_row(lru_w)],
        out_shape=[jax.ShapeDtypeStruct((n_lat_rows, c), F32) for c in (d, hy_w, lru_w, lru_w)]
        + [jax.ShapeDtypeStruct((t - n_lat_rows, lru_w), F32)],
        compiler_params=_cparams(("arbitrary",)),
        name="ffn_odd_in",
    )(tok, m, gg1, wg1, wu1, wd1, gg, w)


def _shift_down(cur, prev8, k):
    r = pltpu.roll(cur, k, 0)
    row = lax.broadcasted_iota(jnp.int32, prev8.shape, 0)
    head = jnp.where(row < k, pltpu.roll(prev8, k, 0), r[:SUBLANES])
    return jnp.concatenate([head, r[SUBLANES:]], axis=0)


def _shift_up(cur, next8, k):
    rows = cur.shape[0]
    r = pltpu.roll(cur, rows - k, 0)
    row = lax.broadcasted_iota(jnp.int32, next8.shape, 0)
    tail = jnp.where(row >= SUBLANES - k, pltpu.roll(next8, SUBLANES - k, 0), r[rows - SUBLANES:])
    return jnp.concatenate([r[:rows - SUBLANES], tail], axis=0)


def _halo_specs(tt, width, n_tiles, tile_of):
    per = tt // SUBLANES
    last = n_tiles * per - 1
    cur = pl.BlockSpec((None, tt, width), lambda b, i: (b, tile_of(i), 0))
    prev = pl.BlockSpec((None, SUBLANES, width), lambda b, i: (b, jnp.maximum(tile_of(i) * per - 1, 0), 0))
    nxt = pl.BlockSpec((None, SUBLANES, width), lambda b, i: (b, jnp.minimum((tile_of(i) + 1) * per, last), 0))
    return cur, prev, nxt


def _hyena_prep_kernel(cur_ref, prev_ref, next_ref, w_ref, b_ref, z_out, x0_out):
    i = pl.program_id(1)
    cur = cur_ref[...]
    prev8 = prev_ref[...] * (i > 0).astype(F32)
    next8 = next_ref[...] * (i < pl.num_programs(1) - 1).astype(F32)
    w = w_ref[...]
    u = w[0:1] * _shift_down(cur, prev8, 1) + w[1:2] * cur + w[2:3] * _shift_up(cur, next8, 1) + b_ref[...]
    c = z_out.shape[1]
    x0_out[...] = u[:, :c]
    z_out[...] = u[:, 2 * c:] * u[:, c:2 * c]


def _hyena_prep(hy, conv_w, conv_b, *, tt):
    b, n, c3 = hy.shape
    c = c3 // 3
    cur, prev, nxt = _halo_specs(tt, c3, n // tt, lambda i: i)
    w8 = jnp.concatenate([conv_w.astype(F32), jnp.zeros((SUBLANES - conv_w.shape[0], c3), F32)], axis=0)
    out_spec = pl.BlockSpec((None, tt, c), lambda bi, i: (bi, i, 0))
    return pl.pallas_call(
        _hyena_prep_kernel,
        grid=(b, n // tt),
        in_specs=[cur, prev, nxt, pl.BlockSpec((SUBLANES, c3), lambda bi, i: (0, 0)),
                  pl.BlockSpec((1, c3), lambda bi, i: (0, 0))],
        out_specs=[out_spec, out_spec],
        out_shape=[jax.ShapeDtypeStruct((b, n, c), F32)] * 2,
        compiler_params=_cparams(("parallel", "parallel")),
        name="hyena_prep",
    )(hy, hy, hy, w8, conv_b.reshape(1, c3).astype(F32))


def _gather_pitch(rows):
    p = -(-rows // SUBLANES)
    return (p + 1 - p % 2) * SUBLANES


class _Dft:
    def __init__(self, n):
        self.n = n
        self.N = 2 * n
        self.N2 = FFT_N2
        self.N1 = self.N // self.N2
        self.A = n // self.N2
        self.K1 = self.N1 // 2 + 1
        self.K1p = -(-self.K1 // SUBLANES) * SUBLANES
        self.y_pitch = _gather_pitch(2 * self.K1p)
        self.x_pitch = _gather_pitch(2 * self.N2)
        N, N1, N2, A, K1, K1p = self.N, self.N1, self.N2, self.A, self.K1, self.K1p
        b = np.arange(N2)[:, None, None]
        k1 = np.arange(K1p)[None, :, None]
        a = np.arange(A)[None, None, :]
        phi = 2.0 * np.pi * ((a * k1 % N1) / N1 + (b * k1 % N) / N)
        live = (k1 < K1).astype(np.float64)
        f1 = np.concatenate([np.cos(phi) * live, -np.sin(phi) * live], axis=1)
        f1 = np.concatenate([f1[0::2], f1[1::2]], axis=2)
        wgt = np.where((k1 == 0) | (k1 == N1 // 2), 1.0, 2.0) * live / N
        g1 = np.concatenate([np.cos(phi) * wgt, -np.sin(phi) * wgt], axis=1)
        g1 = np.transpose(g1, (0, 2, 1))
        ang = 2.0 * np.pi * (np.arange(N2)[:, None] * np.arange(N2)[None, :] % N2) / N2
        c, s = np.cos(ang), np.sin(ang)
        f2 = np.block([[c, s], [-s, c]])
        g2 = np.block([[c, -s], [s, c]])
        self.f1, self.f2, self.g2, self.g1 = (jnp.asarray(x, F32).astype(BF16) for x in (f1, f2, g2, g1))


def _dft_forward(src_ref, f1_ref, f2_ref, y_scr, put, dft):
    n2, a_n, k1p2, yp = dft.N2, dft.A, 2 * dft.K1p, dft.y_pitch

    zero = jnp.zeros((a_n, LANES), BF16)

    def stage1(i, _):
        b = 2 * i
        x0 = src_ref[pl.ds(b, a_n, stride=n2), :].astype(BF16)
        x1 = src_ref[pl.ds(b + 1, a_n, stride=n2), :].astype(BF16)
        rhs = jnp.concatenate([jnp.concatenate([x0, zero], axis=1), jnp.concatenate([zero, x1], axis=1)], axis=0)
        y = _dot(f1_ref[i], rhs)
        for j in range(2):
            r0 = pl.multiple_of((b + j) * yp, SUBLANES)
            y_scr[pl.ds(r0, k1p2), :] = y[:, j * LANES:(j + 1) * LANES]
        return 0

    lax.fori_loop(0, n2 // 2, stage1, 0, unroll=DFT_UNROLL // 2)

    def plane(k1):
        yr = y_scr[pl.ds(k1, n2, stride=yp), :]
        yi = y_scr[pl.ds(dft.K1p + k1, n2, stride=yp), :]
        return jnp.concatenate([yr, yi], axis=0).astype(BF16)

    def stage2_pair(i, _):
        k1 = 2 * i
        put(k1, _dot(f2_ref[...], jnp.concatenate([plane(k1), plane(k1 + 1)], axis=1)), 2)
        return 0

    pairs = (dft.K1 // 2) - (dft.K1 // 2) % (DFT_UNROLL // 2)
    lax.fori_loop(0, pairs, stage2_pair, 0, unroll=DFT_UNROLL // 2)
    for k1 in range(2 * pairs, dft.K1):
        put(k1, _dot(f2_ref[...], plane(k1)), 1)


def _hyena_filter_kernel(feat_ref, w1_ref, b1_ref, w2_ref, b2_ref, w3f_ref, w3b_ref, freq_ref, delta_ref,
                         f1_ref, f2_ref, h_out, filt_scr, y_scr, hid_scr, *, dft):
    n = dft.n
    rb = _row_tile(n)
    blocks = n // rb
    rows_of = lambda r: pl.ds(pl.multiple_of(r * rb, rb), rb)

    @pl.when(pl.program_id(0) == 0)
    def _():
        freq = freq_ref[...]
        w1_hi, w1_lo = _split(w1_ref[...])
        w2_hi, w2_lo = _split(w2_ref[...])

        def hidden(r, _):
            f_hi, f_lo = _split(feat_ref[rows_of(r), :])
            h = jnp.sin(freq[0:1] * (_dot3(f_hi, f_lo, w1_hi, w1_lo) + b1_ref[...]))
            h_hi, h_lo = _split(h)
            hid_scr[rows_of(r), :] = jnp.sin(freq[1:2] * (_dot3(h_hi, h_lo, w2_hi, w2_lo) + b2_ref[...]))
            return 0

        lax.fori_loop(0, blocks, hidden, 0)

    def fill_filter(w3_ref):
        w_hi, w_lo = _split(w3_ref[...])

        def blk(r, _):
            h_hi, h_lo = _split(hid_scr[rows_of(r), :])
            decay = jnp.exp(-feat_ref[rows_of(r), 0:1] * delta_ref[...])
            filt_scr[rows_of(r), :] = _dot3(h_hi, h_lo, w_hi, w_lo) * decay
            return 0

        lax.fori_loop(0, blocks, blk, 0)

    n2 = dft.N2
    fill_filter(w3f_ref)

    def put_fwd(k1, x, w):
        for j in range(w):
            h_out[k1 + j] = x[:, j * LANES:(j + 1) * LANES]

    _dft_forward(filt_scr, f1_ref, f2_ref, y_scr, put_fwd, dft)

    fill_filter(w3b_ref)
    first = lax.broadcasted_iota(jnp.int32, (SUBLANES, LANES), 0) == 0
    filt_scr[:SUBLANES, :] = jnp.where(first, 0.0, filt_scr[:SUBLANES, :])
    sign = jnp.where(lax.broadcasted_iota(jnp.int32, (2 * n2, LANES), 0) < n2, 1.0, -1.0)

    def put_bwd(k1, x, w):
        for j in range(w):
            h_out[k1 + j] = h_out[k1 + j] + sign * x[:, j * LANES:(j + 1) * LANES]

    _dft_forward(filt_scr, f1_ref, f2_ref, y_scr, put_bwd, dft)


def _hyena_features(n):
    t = np.linspace(0.0, 1.0, n, dtype=np.float32).astype(np.float64)[:, None]
    w = (2.0 * np.pi * np.arange(n, dtype=np.float64)[:, None] / n).astype(np.float32).astype(np.float64)
    bands = np.linspace(1e-4, HY_BANDS - 1, HY_BANDS, dtype=np.float32).astype(np.float64)[None, :]
    bw = (bands * w).astype(np.float32).astype(np.float64)
    z = np.concatenate([t, np.cos(bw), -np.sin(bw)], axis=-1)
    z = np.concatenate([z, np.zeros((n, LANES - z.shape[1]))], axis=-1)
    return jnp.asarray(z, F32)


def _hyena_filter_spectrum(n, w1, b1, w2, b2, w3, sin_freq, dft):
    ch = w3.shape[1] // 2
    hid = w1.shape[1]
    feat = _hyena_features(n)
    w1p = jnp.concatenate([w1.astype(F32), jnp.zeros((LANES - w1.shape[0], hid), F32)], axis=0)
    deltas = jnp.asarray(np.abs(np.linspace(HY_MIN_DECAY, HY_MAX_DECAY, ch, dtype=np.float32)), F32).reshape(1, ch)
    freq8 = jnp.concatenate([sin_freq.astype(F32), jnp.zeros((SUBLANES - 2, hid), F32)], axis=0)
    full = lambda a: pl.BlockSpec(a.shape, lambda c: (0,) * a.ndim, pipeline_mode=pl.Buffered(1))
    tiles = ch // LANES
    args = (feat, w1p, b1.reshape(1, hid).astype(F32), w2.astype(F32), b2.reshape(1, hid).astype(F32))
    return pl.pallas_call(
        functools.partial(_hyena_filter_kernel, dft=dft),
        grid=(tiles,),
        in_specs=[full(a) for a in args] + [
            pl.BlockSpec((hid, LANES), lambda c: (0, c)),
            pl.BlockSpec((hid, LANES), lambda c: (0, tiles + c)),
            full(freq8),
            pl.BlockSpec((1, LANES), lambda c: (0, c)),
            full(dft.f1), full(dft.f2),
        ],
        out_specs=pl.BlockSpec((dft.K1, 2 * dft.N2, LANES), lambda c: (0, 0, c)),
        out_shape=jax.ShapeDtypeStruct((dft.K1, 2 * dft.N2, ch), F32),
        scratch_shapes=[pltpu.VMEM((n, LANES), F32), pltpu.VMEM((dft.N2 * dft.y_pitch, LANES), F32),
                        pltpu.VMEM((n, hid), F32)],
        compiler_params=_cparams(("arbitrary",)),
        name="hyena_filter",
    )(*args, w3.astype(F32), w3.astype(F32), freq8, deltas, dft.f1, dft.f2)


def _hyena_conv_kernel(z_ref, h_ref, f1_ref, f2_ref, g2_ref, g1_ref, y_out, y_scr, x_scr, *, dft):
    n2, a_n, k1p, xp = dft.N2, dft.A, dft.K1p, dft.x_pitch

    def put(k1, x, w):
        hk = jnp.concatenate([h_ref[k1 + j] for j in range(w)], axis=1)
        xr, xi, hr, hi = x[:n2], x[n2:], hk[:n2], hk[n2:]
        prod = jnp.concatenate([xr * hr - xi * hi, xr * hi + xi * hr], axis=0).astype(BF16)
        inv = _dot(g2_ref[...], prod)
        for j in range(w):
            r0 = pl.multiple_of((k1 + j) * xp, SUBLANES)
            x_scr[pl.ds(r0, 2 * n2), :] = inv[:, j * LANES:(j + 1) * LANES]

    if k1p > dft.K1:
        x_scr[dft.K1 * xp:, :] = jnp.zeros(((k1p - dft.K1) * xp, LANES), F32)
    _dft_forward(z_ref, f1_ref, f2_ref, y_scr, put, dft)

    def last(b, _):
        yr = x_scr[pl.ds(b, k1p, stride=xp), :]
        yi = x_scr[pl.ds(n2 + b, k1p, stride=xp), :]
        y_out[pl.ds(b, a_n, stride=n2), :] = _dot(g1_ref[b], jnp.concatenate([yr, yi], axis=0).astype(BF16))
        return 0

    lax.fori_loop(0, n2, last, 0, unroll=DFT_UNROLL)


def _hyena_conv(z, spec, dft):
    b, n, ch = z.shape
    once = pl.Buffered(1)
    full = lambda a: pl.BlockSpec(a.shape, lambda c, bi: (0,) * a.ndim, pipeline_mode=once)
    io_spec = pl.BlockSpec((None, n, LANES), lambda c, bi: (bi, 0, c))
    return pl.pallas_call(
        functools.partial(_hyena_conv_kernel, dft=dft),
        grid=(ch // LANES, b),
        in_specs=[io_spec, pl.BlockSpec((dft.K1, 2 * dft.N2, LANES), lambda c, bi: (0, 0, c), pipeline_mode=once),
                  full(dft.f1), full(dft.f2), full(dft.g2), full(dft.g1)],
        out_specs=io_spec,
        out_shape=jax.ShapeDtypeStruct((b, n, ch), F32),
        scratch_shapes=[pltpu.VMEM((dft.N2 * dft.y_pitch, LANES), F32), pltpu.VMEM((dft.K1p * dft.x_pitch, LANES), F32)],
        compiler_params=_cparams(("parallel", "arbitrary")),
        name="hyena_conv",
    )(z, spec, dft.f1, dft.f2, dft.g2, dft.g1)


def _lru_tile_terms(cur, prev8, next8, conv_w, conv_b, wg, bg, sp, reverse):
    rows, c = cur.shape
    xc = (conv_w[0:1] * _shift_down(cur, prev8, 2) + conv_w[1:2] * _shift_down(cur, prev8, 1)
          + conv_w[2:3] * cur + conv_w[3:4] * _shift_up(cur, next8, 1) + conv_b)
    g = _dot(xc.astype(BF16), wg) + bg
    r, ig = _sigmoid(g[:, :c]), _sigmoid(g[:, c:])
    log_a = -LRU_C * r * sp
    a = jnp.exp(log_a)
    th = jnp.tanh(log_a)
    bb = jnp.sqrt(-2.0 * th / (1.0 - th)) * (ig * xc)
    groups = rows // SUBLANES
    a, bb = a.reshape(groups, SUBLANES, c), bb.reshape(groups, SUBLANES, c)
    sub = lax.broadcasted_iota(jnp.int32, (groups, SUBLANES, c), 1)
    for k in (1, 2, 4):
        shift = SUBLANES - k if reverse else k
        ok = sub < SUBLANES - k if reverse else sub >= k
        a_s, b_s = pltpu.roll(a, shift, 1), pltpu.roll(bb, shift, 1)
        bb = jnp.where(ok, bb + a * b_s, bb)
        a = jnp.where(ok, a * a_s, a)
    return a.reshape(rows, c), bb.reshape(rows, c)


def _lru_kernel(ctx_ref, curf_ref, prevf_ref, nextf_ref, curb_ref, prevb_ref, nextb_ref,
                cw_ref, cb_ref, wgf_ref, wgb_ref, bgf_ref, bgb_ref, sp_ref,
                hf_out, hb_out, a_scr, b_scr, carry_scr, *, batch, tt):
    i = pl.program_id(0)
    nt = pl.num_programs(0)
    cw, cb = cw_ref[...], cb_ref[...]
    sp = sp_ref[...]
    c = cw.shape[1]
    params = ((wgf_ref, bgf_ref, sp[0:1]), (wgb_ref, bgb_ref, sp[1:2]))

    def carry_scan(rows, write):
        groups = rows // SUBLANES

        def body(j, carries):
            new = []
            for d in range(2):
                jj = j if d == 0 else groups - 1 - j
                r0 = pl.multiple_of(jj * SUBLANES, SUBLANES)
                for bi in range(batch):
                    h8 = b_scr[d, bi, pl.ds(r0, SUBLANES), :] + a_scr[d, bi, pl.ds(r0, SUBLANES), :] * carries[d * batch + bi]
                    write(d, bi, r0, h8)
                    edge = h8[SUBLANES - 1:SUBLANES] if d == 0 else h8[0:1]
                    new.append(jnp.broadcast_to(edge, (SUBLANES, c)))
            return tuple(new)

        init = tuple(carry_scr[d, bi] for d in range(2) for bi in range(batch))
        out = lax.fori_loop(0, groups, body, init)
        for d in range(2):
            for bi in range(batch):
                carry_scr[d, bi] = out[d * batch + bi]

    @pl.when(i == 0)
    def _():
        carry_scr[...] = jnp.zeros(carry_scr.shape, F32)
        nc = ctx_ref.shape[1]
        zero8 = jnp.zeros((SUBLANES, c), F32)
        for d in range(2):
            wg, bg, spd = params[d]
            for bi in range(batch):
                a, bb = _lru_tile_terms(ctx_ref[bi], zero8, zero8, cw, cb, wg[...], bg[...], spd, d == 1)
                a_scr[d, bi, :nc, :] = a
                b_scr[d, bi, :nc, :] = bb
        carry_scan(nc, lambda d, bi, r0, h8: None)

    tiles = ((curf_ref, prevf_ref, nextf_ref, i), (curb_ref, prevb_ref, nextb_ref, nt - 1 - i))
    for d in range(2):
        cur_ref, prev_ref, next_ref, ti = tiles[d]
        wg, bg, spd = params[d]
        has_prev = (ti > 0).astype(F32)
        has_next = (ti < nt - 1).astype(F32)
        for bi in range(batch):
            a, bb = _lru_tile_terms(cur_ref[bi], prev_ref[bi] * has_prev, next_ref[bi] * has_next,
                                    cw, cb, wg[...], bg[...], spd, d == 1)
            a_scr[d, bi, :tt, :] = a
            b_scr[d, bi, :tt, :] = bb

    def write(d, bi, r0, h8):
        if d == 0:
            hf_out[bi, pl.ds(r0, SUBLANES), :] = h8
        else:
            hb_out[bi, pl.ds(r0, SUBLANES), :] = h8

    carry_scan(tt, write)


def _lru_gate_weights(w_a, w_x):
    def dense(w):
        nb, bs = w.shape[0], w.shape[1]
        eye = jnp.eye(nb, dtype=F32)
        return jnp.einsum('hij,hk->hikj', w.astype(F32), eye).reshape(nb * bs, nb * bs)
    return [jnp.concatenate([dense(w_a[d]), dense(w_x[d])], axis=1).astype(BF16) for d in range(2)]


def _lru(lx_lat, lx_ctx, conv_w, conv_b, w_a, b_a, w_x, b_x, lam, *, tt):
    b, n, c = lx_lat.shape
    nc = lx_ctx.shape[1]
    nt = n // tt
    wgf, wgb = _lru_gate_weights(w_a, w_x)
    bgf = jnp.concatenate([b_a[0], b_x[0]]).reshape(1, 2 * c).astype(F32)
    bgb = jnp.concatenate([b_a[1], b_x[1]]).reshape(1, 2 * c).astype(F32)
    sp = jax.nn.softplus(-lam.astype(F32))
    sp8 = jnp.concatenate([sp, jnp.zeros((SUBLANES - 2, c), F32)], axis=0)
    cw8 = jnp.concatenate([conv_w.astype(F32), jnp.zeros((SUBLANES - conv_w.shape[0], c), F32)], axis=0)
    cb = conv_b.reshape(1, c).astype(F32)
    per = tt // SUBLANES
    last = n // SUBLANES - 1

    def specs(tile_of):
        cur = pl.BlockSpec((b, tt, c), lambda i: (0, tile_of(i), 0))
        prev = pl.BlockSpec((b, SUBLANES, c), lambda i: (0, jnp.maximum(tile_of(i) * per - 1, 0), 0))
        nxt = pl.BlockSpec((b, SUBLANES, c), lambda i: (0, jnp.minimum((tile_of(i) + 1) * per, last), 0))
        return [cur, prev, nxt]

    fwd_tile = lambda i: i
    bwd_tile = lambda i: nt - 1 - i
    full = lambda a: pl.BlockSpec(a.shape, lambda i: (0,) * a.ndim)
    rows_scr = max(tt, nc)
    return pl.pallas_call(
        functools.partial(_lru_kernel, batch=b, tt=tt),
        grid=(nt,),
        in_specs=[full(lx_ctx)] + specs(fwd_tile) + specs(bwd_tile)
        + [full(cw8), full(cb), full(wgf), full(wgb), full(bgf), full(bgb), full(sp8)],
        out_specs=[pl.BlockSpec((b, tt, c), lambda i: (0, i, 0)), pl.BlockSpec((b, tt, c), lambda i: (0, nt - 1 - i, 0))],
        out_shape=[jax.ShapeDtypeStruct((b, n, c), F32)] * 2,
        scratch_shapes=[pltpu.VMEM((2, b, rows_scr, c), F32), pltpu.VMEM((2, b, rows_scr, c), F32),
                        pltpu.VMEM((2, b, SUBLANES, c), F32)],
        compiler_params=_cparams(("arbitrary",)),
        name="rglru",
    )(lx_ctx, lx_lat, lx_lat, lx_lat, lx_lat, lx_lat, lx_lat, cw8, cb, wgf, wgb, bgf, bgb, sp8)


def _odd_out_kernel(x_ref, m_ref, yc_ref, z_ref, x0_ref, hb_ref, hf_ref, hbk_ref, lg_ref, wh_ref, wr_ref,
                    g2_ref, wg_ref, wu_ref, wd_ref, fn_ref, out_ref):
    z = z_ref[...]
    hy = x0_ref[...] * (yc_ref[...] + z * hb_ref[...])
    r = (hf_ref[...] + hbk_ref[...]) * _gelu_tanh(lg_ref[...])
    mix = _dot(hy.astype(BF16), wh_ref[...]) + _dot(r.astype(BF16), wr_ref[...])
    m = m_ref[...]
    y = _ffn_half_step(x_ref[...] + m[5:6] * mix, m, g2_ref, wg_ref, wu_ref, wd_ref, 2)
    out_ref[...] = _rms(y, fn_ref[...])


def _odd_out(tok, m, yc, z, x0, hy_bias, hf, hb, lg, w_out, g2, wg2, wu2, wd2, final_g, *, n_rows, tm, mod_row):
    d = tok.shape[1]
    c = yc.shape[1]
    wh, wr = w_out[:c].astype(BF16), w_out[c:].astype(BF16)
    hbias = hy_bias.reshape(1, c).astype(F32)
    gg2, fg = g2.reshape(1, d), final_g.reshape(1, d)
    full = lambda a: pl.BlockSpec(a.shape, lambda i: (0,) * a.ndim, pipeline_mode=pl.Buffered(1))
    row = lambda w: pl.BlockSpec((tm, w), lambda i: (i, 0))
    return pl.pallas_call(
        _odd_out_kernel,
        grid=(n_rows // tm,),
        in_specs=[row(d), pl.BlockSpec((None, N_MOD, d), lambda i: (mod_row(i), 0, 0)),
                  row(c), row(c), row(c), full(hbias), row(c), row(c), row(c), full(wh), full(wr),
                  full(gg2), full(wg2), full(wu2), full(wd2), full(fg)],
        out_specs=row(d),
        out_shape=jax.ShapeDtypeStruct((n_rows, d), F32),
        compiler_params=_cparams(("parallel",)),
        name="odd_out_ffn",
    )(tok, m, yc, z, x0, hbias, hf, hb, lg, wh, wr, gg2, wg2, wu2, wd2, fg)


def kernel(x, c, ctx, c_ctx, mod_w, mod_b, norm_ffn1, norm_mix, norm_ffn2, ffn1_w_gate, ffn1_w_up, ffn1_w_down, ffn2_w_gate, ffn2_w_up, ffn2_w_down, ev_w_in, mla_q_norm, mla_w_uq, mla_kv_norm, mla_w_ukv, s5_lambda_re, s5_lambda_im, s5_log_step, s5_b_re, s5_b_im, s5_c_re, s5_c_im, s5_d, s5_w_glu, s5_b_glu, ev_w_out, od_w_in, hy_conv_w, hy_conv_b, hy_filt_w1, hy_filt_b1, hy_filt_w2, hy_filt_b2, hy_filt_w3, hy_sin_freq, hy_bias, lru_conv_w, lru_conv_b, lru_w_a, lru_b_a, lru_w_x, lru_b_x, lru_lambda, od_w_out, final_norm):
    b, n, d = x.shape
    nc = ctx.shape[1]
    depth = mod_w.shape[0]
    assert depth == 2, "the trunk is laid out for one even and one odd layer"
    n_lat_rows = b * n
    t = n_lat_rows + b * nc
    tm = _row_tile(n, b * nc)
    tpb = n // tm
    mod_row = lambda i: jnp.minimum(i // tpb, b)

    cond8 = jnp.concatenate([c, c_ctx[None, :], jnp.zeros((SUBLANES - b - 1, d), F32)], axis=0)
    m = _adaln(cond8, mod_w, mod_b)
    bf = lambda w: w.astype(BF16)

    ev_w = _even_in_weights(ev_w_in[0], mla_w_uq[0], mla_w_ukv[0])
    ffn1 = (norm_ffn1[0], bf(ffn1_w_gate[0]), bf(ffn1_w_up[0]), bf(ffn1_w_down[0]))
    tok, qt, k, vt, s_chunks = _even_in(x.reshape(n_lat_rows, d), ctx.reshape(b * nc, d), m[0], ffn1, norm_mix[0],
                                        ev_w, mla_q_norm[0], mla_kv_norm[0], n=n, tm=tm, mod_row=mod_row)
    o_lat, o_ctx = _attention(qt, k, vt, b=b, n=n, nc=nc)
    s5_ops = _s5_operators(s5_lambda_re[0], s5_lambda_im[0], s5_log_step[0], s5_b_re[0], s5_b_im[0],
                           s5_c_re[0], s5_c_im[0], s5_d[0])
    ys = _s5(s_chunks, s5_ops, b=b, n=n, nc=nc)
    tok = _even_out(tok, m[0], o_lat, o_ctx, ys, s5_w_glu[0], s5_b_glu[0], ev_w_out[0], norm_ffn2[0], bf(ffn2_w_gate[0]),
                    bf(ffn2_w_up[0]), bf(ffn2_w_down[0]), tm=tm, mod_row=mod_row)

    hy_w = hy_conv_w.shape[2]
    lru_w = lru_conv_w.shape[2]
    ffn1 = (norm_ffn1[1], bf(ffn1_w_gate[1]), bf(ffn1_w_up[1]), bf(ffn1_w_down[1]))
    tok, hy, lx, lg, lx_ctx = _odd_in(tok, m[1], ffn1, norm_mix[1], od_w_in[0], hy_w=hy_w, lru_w=lru_w, tm=tm,
                                      mod_row=mod_row, n_lat_rows=n_lat_rows)
    tt = _row_tile(n)
    z, x0 = _hyena_prep(hy.reshape(b, n, hy_w), hy_conv_w[0], hy_conv_b[0], tt=tt)
    dft = _Dft(n)
    spec = _hyena_filter_spectrum(n, hy_filt_w1[0], hy_filt_b1[0], hy_filt_w2[0], hy_filt_b2[0], hy_filt_w3[0],
                                  hy_sin_freq[0], dft)
    yc = _hyena_conv(z, spec, dft)
    hf, hb = _lru(lx.reshape(b, n, lru_w), lx_ctx.reshape(b, nc, lru_w), lru_conv_w[0],
                  lru_conv_b[0], lru_w_a[0], lru_b_a[0], lru_w_x[0], lru_b_x[0], lru_lambda[0], tt=min(tt, 256))
    flat = lambda a: a.reshape(n_lat_rows, a.shape[2])
    lat = _odd_out(tok, m[1], flat(yc), flat(z), flat(x0), hy_bias[0], flat(hf), flat(hb), lg, od_w_out[0],
                   norm_ffn2[1], bf(ffn2_w_gate[1]), bf(ffn2_w_up[1]), bf(ffn2_w_down[1]), final_norm,
                   n_rows=n_lat_rows, tm=tm, mod_row=mod_row)
    return lat.reshape(b, n, d)
```

```python
import functools
import math

import numpy as np
import jax
import jax.numpy as jnp
from jax import lax
from jax.experimental import pallas as pl
from jax.experimental.pallas import tpu as pltpu

F32 = jnp.float32
BF16 = jnp.bfloat16

GRID_W = 64
N_MOD = 9
NORM_EPS = 1e-6
ROPE_BASE = 10000.0
MLA_HEADS = 8
QK_NOPE = 64
QK_ROPE = 32
V_DIM = 64
Q_RANK = 384
KV_RANK = 256
S5_GROUP = 16
S5_STATE = 64
HY_BANDS = 16
HY_DECAY_TARGET = 1e-2
HY_MIN_DECAY = math.log(HY_DECAY_TARGET) / 1.5
HY_MAX_DECAY = math.log(HY_DECAY_TARGET) / 0.3
LRU_BLOCKS = 8
LRU_C = 8.0

LANES = 128
SUBLANES = 8
VMEM_BYTES_V7X = 64 * 1024 * 1024
VMEM_LIMIT = VMEM_BYTES_V7X - 8 * 1024 * 1024

HEAD_LANES = LANES
ONES_LANE = V_DIM
ATT_V_ROWS = V_DIM + 16
ATT_Q_TILE = 2048
ATT_Q_LANES = 256
S5_CHUNK = 8
S5_SCAN_UNROLL = 8
FFT_N2 = 128
DFT_UNROLL = 8
FFN_CHUNK = 256
NEG_BIG = -1e30


def _cparams(sem):
    return pltpu.CompilerParams(dimension_semantics=sem, vmem_limit_bytes=VMEM_LIMIT)


def _dot(a, b):
    return jnp.dot(a, b, preferred_element_type=F32)


def _split(x):
    hi = x.astype(BF16)
    lo = (x - hi.astype(F32)).astype(BF16)
    return hi, lo


def _dot3(a_hi, a_lo, b_hi, b_lo):
    return _dot(a_hi, b_hi) + _dot(a_lo, b_hi) + _dot(a_hi, b_lo)


def _sigmoid(x):
    return 1.0 / (1.0 + jnp.exp(-x))


def _gelu_tanh(x):
    return 0.5 * x * (1.0 + jnp.tanh(math.sqrt(2.0 / math.pi) * (x + 0.044715 * (x * x * x))))


def _rms(x, g):
    return x * lax.rsqrt(jnp.mean(x * x, axis=-1, keepdims=True) + NORM_EPS) * g


def _row_tile(*lengths):
    for t in (512, 256, 128, 64, 32, 16, 8):
        if all(l % t == 0 for l in lengths):
            return t
    raise ValueError(f"no row tile divides {lengths}")


def _adaln_kernel(c_ref, w_ref, b_ref, o_ref):
    c = c_ref[...]
    s_hi, s_lo = _split(c * _sigmoid(c))
    w_hi, w_lo = _split(w_ref[...])
    o_ref[...] = _dot3(s_hi, s_lo, w_hi, w_lo) + b_ref[...]


def _adaln(cond8, mod_w, mod_b):
    depth, d, nd = mod_w.shape
    tn = d
    out = pl.pallas_call(
        _adaln_kernel,
        grid=(depth, nd // tn),
        in_specs=[
            pl.BlockSpec((SUBLANES, d), lambda l, j: (0, 0)),
            pl.BlockSpec((None, d, tn), lambda l, j: (l, 0, j)),
            pl.BlockSpec((None, 1, tn), lambda l, j: (l, 0, j)),
        ],
        out_specs=pl.BlockSpec((None, SUBLANES, tn), lambda l, j: (l, 0, j)),
        out_shape=jax.ShapeDtypeStruct((depth, SUBLANES, nd), F32),
        compiler_params=_cparams(("arbitrary", "arbitrary")),
        name="adaln",
    )(cond8, mod_w, mod_b.reshape(depth, 1, nd))
    return out.reshape(depth, SUBLANES, N_MOD, d)


def _ffn_half_step(x, m, g_ref, wg_ref, wu_ref, wd_ref, k):
    shift, scale, gate = m[3 * k:3 * k + 1], m[3 * k + 1:3 * k + 2], m[3 * k + 2:3 * k + 3]
    hb = (_rms(x, g_ref[...]) * (1.0 + scale) + shift).astype(BF16)
    f_all = wg_ref.shape[1]
    tf = FFN_CHUNK if f_all % FFN_CHUNK == 0 else LANES
    acc = jnp.zeros(x.shape, F32)
    for f in range(f_all // tf):
        sl = slice(f * tf, (f + 1) * tf)
        gt = _dot(hb, wg_ref[:, sl])
        up = _dot(hb, wu_ref[:, sl])
        acc = acc + _dot((gt * _sigmoid(gt) * up).astype(BF16), wd_ref[sl, :])
    return x + (0.5 * gate) * acc


def _even_in_kernel(x_ref, tail_ref, m_ref, g1_ref, wg_ref, wu_ref, wd_ref,
                    g_ref, w1_ref, qn_ref, kvn_ref, wq_ref, wqp_ref, wk_ref, wv_ref,
                    c_ref, s_ref, tok_out, qt_out, k_out, vt_out, u_out, s_scr, *, tiles_first):
    m = m_ref[...]
    x = jnp.where(pl.program_id(0) < tiles_first, x_ref[...], tail_ref[...])
    x = _ffn_half_step(x, m, g1_ref, wg_ref, wu_ref, wd_ref, 0)
    tok_out[...] = x
    hb = (_rms(x, g_ref[...]) * (1.0 + m[4:5]) + m[3:4]).astype(BF16)
    p = _dot(hb, w1_ref[...])
    o = 0
    cq = p[:, o:o + Q_RANK]; o += Q_RANK
    ckv = p[:, o:o + KV_RANK]; o += KV_RANK
    kra = p[:, o:o + HEAD_LANES]; o += HEAD_LANES
    krb = p[:, o:o + HEAD_LANES]; o += HEAD_LANES
    pw = 2 * S5_GROUP
    per_tile = LANES // pw
    chunks = s_scr.shape[1] // S5_CHUNK
    for lt in range(s_scr.shape[0]):
        s_scr[lt] = p[:, o + lt * LANES:o + (lt + 1) * LANES]
        by_tok = [s_scr[lt, pl.ds(t, chunks, stride=S5_CHUNK), :] for t in range(S5_CHUNK)]
        for l in range(per_tile):
            u_out[lt * per_tile + l] = jnp.concatenate([r[:, l * pw:(l + 1) * pw] for r in by_tok], axis=-1).astype(BF16)
    cos, sin = c_ref[...], s_ref[...]
    cqn = _rms(cq, qn_ref[...]).astype(BF16)
    qa = _dot(cqn, wq_ref[...])
    qb = _dot(cqn, wqp_ref[...])
    ckvn = _rms(ckv, kvn_ref[...]).astype(BF16)
    kn = _dot(ckvn, wk_ref[...])
    vv = _dot(ckvn, wv_ref[...])
    kr = kra * cos + krb * sin
    ones = (lax.broadcasted_iota(jnp.int32, (1, HEAD_LANES), 1) == ONES_LANE).astype(F32)
    scale = (QK_NOPE + QK_ROPE) ** -0.5 * math.log2(math.e)
    for h in range(MLA_HEADS):
        sl = slice(h * HEAD_LANES, (h + 1) * HEAD_LANES)
        qt_out[h] = ((qa[:, sl] * cos + qb[:, sl] * sin) * scale).T.astype(BF16)
        k_out[h] = (kn[:, sl] + kr).astype(BF16)
        vt_out[h] = (vv[:, sl] + ones).T.astype(BF16)


def _rope_partner(w):
    half = w.shape[-1] // 2
    return jnp.concatenate([-w[..., half:], w[..., :half]], axis=-1)


def _even_in_weights(w_in, w_uq, w_ukv):
    d = w_in.shape[0]
    o = 0
    w_cq = w_in[:, o:o + Q_RANK]; o += Q_RANK
    w_ckv = w_in[:, o:o + KV_RANK]; o += KV_RANK
    w_kr = w_in[:, o:o + QK_ROPE]; o += QK_ROPE
    w_s = w_in[:, o:]
    pad_l = jnp.zeros((d, QK_NOPE), F32)
    pad_r = jnp.zeros((d, HEAD_LANES - QK_NOPE - QK_ROPE), F32)
    kr_blk = jnp.concatenate([pad_l, w_kr, pad_r], axis=1)
    kr_blk_p = jnp.concatenate([pad_l, _rope_partner(w_kr), pad_r], axis=1)
    w1 = jnp.concatenate([w_cq, w_ckv, kr_blk, kr_blk_p, w_s], axis=1).astype(BF16)

    dk = QK_NOPE + QK_ROPE
    wq = w_uq.reshape(Q_RANK, MLA_HEADS, dk)
    zq = jnp.zeros((Q_RANK, MLA_HEADS, HEAD_LANES - dk), F32)
    wq_pad = jnp.concatenate([wq, zq], axis=-1)
    wq_par = jnp.concatenate([jnp.zeros((Q_RANK, MLA_HEADS, QK_NOPE), F32), _rope_partner(wq[..., QK_NOPE:]), zq], axis=-1)
    wkv = w_ukv.reshape(KV_RANK, MLA_HEADS, QK_NOPE + V_DIM)
    wk = jnp.concatenate([wkv[..., :QK_NOPE], jnp.zeros((KV_RANK, MLA_HEADS, HEAD_LANES - QK_NOPE), F32)], axis=-1)
    wv = jnp.concatenate([wkv[..., QK_NOPE:], jnp.zeros((KV_RANK, MLA_HEADS, HEAD_LANES - V_DIM), F32)], axis=-1)
    flat = lambda w: w.reshape(w.shape[0], MLA_HEADS * HEAD_LANES).astype(BF16)
    return w1, flat(wq_pad), flat(wq_par), flat(wk), flat(wv)


def _rope_tables(n, n_ident):
    rows = n // GRID_W
    row = np.repeat(np.arange(rows, dtype=np.float32), GRID_W)
    col = np.tile(np.arange(GRID_W, dtype=np.float32), rows)
    n_freq = QK_ROPE // 4
    inv_freq = (np.float32(ROPE_BASE) ** (-np.arange(n_freq, dtype=np.float32) / n_freq)).astype(np.float32)
    ang = np.concatenate([row[:, None] * inv_freq, col[:, None] * inv_freq], axis=-1).astype(np.float64)
    ang = np.concatenate([ang, np.zeros((n_ident, QK_ROPE // 2))], axis=0)
    tot = n + n_ident
    pad = HEAD_LANES - QK_NOPE - QK_ROPE
    cos = np.concatenate([np.ones((tot, QK_NOPE)), np.cos(ang), np.cos(ang), np.ones((tot, pad))], axis=1)
    sin = np.concatenate([np.zeros((tot, QK_NOPE)), np.sin(ang), np.sin(ang), np.zeros((tot, pad))], axis=1)
    return jnp.asarray(cos, F32), jnp.asarray(sin, F32)


def _even_in(lat, tail, m, ffn1, g, weights, q_norm, kv_norm, *, n, tm, mod_row):
    n_lat_rows, d = lat.shape
    t = n_lat_rows + tail.shape[0]
    g1, wg1, wu1, wd1 = ffn1
    gg1 = g1.reshape(1, d)
    w1, wq, wqp, wk, wv = weights
    cos, sin = _rope_tables(n, tm)
    tiles_lat = n_lat_rows // tm
    tpb = n // tm
    tab_row = lambda i: jnp.where(i < tiles_lat, i % tpb, tpb)
    full = lambda a: pl.BlockSpec(a.shape, lambda i: (0,) * a.ndim, pipeline_mode=pl.Buffered(1))
    hl = MLA_HEADS * HEAD_LANES
    s_ch = w1.shape[1] - Q_RANK - KV_RANK - 2 * HEAD_LANES
    qn, kvn = q_norm.reshape(1, Q_RANK), kv_norm.reshape(1, KV_RANK)
    gg = g.reshape(1, d)
    head_spec = pl.BlockSpec((MLA_HEADS, tm, HEAD_LANES), lambda i: (0, i, 0))
    head_shape = jax.ShapeDtypeStruct((MLA_HEADS, t, HEAD_LANES), BF16)
    head_t_spec = pl.BlockSpec((MLA_HEADS, HEAD_LANES, tm), lambda i: (0, 0, i))
    head_t_shape = jax.ShapeDtypeStruct((MLA_HEADS, HEAD_LANES, t), BF16)
    pairs = s_ch // (2 * S5_GROUP)
    u_w = 2 * S5_GROUP * S5_CHUNK
    return pl.pallas_call(
        functools.partial(_even_in_kernel, tiles_first=tiles_lat),
        grid=(t // tm,),
        in_specs=[
            pl.BlockSpec((tm, d), lambda i: (jnp.minimum(i, tiles_lat - 1), 0)),
            pl.BlockSpec((tm, d), lambda i: (jnp.maximum(i - tiles_lat, 0), 0)),
            pl.BlockSpec((None, N_MOD, d), lambda i: (mod_row(i), 0, 0)),
            full(gg1), full(wg1), full(wu1), full(wd1),
            full(gg), full(w1), full(qn), full(kvn), full(wq), full(wqp), full(wk), full(wv),
            pl.BlockSpec((tm, HEAD_LANES), lambda i: (tab_row(i), 0)),
            pl.BlockSpec((tm, HEAD_LANES), lambda i: (tab_row(i), 0)),
        ],
        out_specs=[pl.BlockSpec((tm, d), lambda i: (i, 0)), head_t_spec, head_spec, head_t_spec,
                   pl.BlockSpec((pairs, tm // S5_CHUNK, u_w), lambda i: (0, i, 0))],
        out_shape=[jax.ShapeDtypeStruct((t, d), F32), head_t_shape, head_shape, head_t_shape,
                   jax.ShapeDtypeStruct((pairs, t // S5_CHUNK, u_w), BF16)],
        scratch_shapes=[pltpu.VMEM((s_ch // LANES, tm, LANES), F32)],
        compiler_params=_cparams(("parallel",)),
        name="ffn_even_in",
    )(lat, tail, m, gg1, wg1, wu1, wd1, gg, w1, qn, kvn, wq, wqp, wk, wv, cos, sin)


def _attn_kernel(qt_ref, kc_ref, vtc_ref, *rest, tk, lat):
    if lat:
        kl_ref, vtl_ref, o_ref = rest
    else:
        (o_ref,) = rest
    tq = qt_ref.shape[1]
    lanes = min(ATT_Q_LANES, tq)
    subs = [slice(j * lanes, (j + 1) * lanes) for j in range(tq // lanes)]

    def step(k, vt, carry):
        scores = [_dot(k, qt_ref[:, sl]) for sl in subs]
        out = []
        for s, (m, acc) in zip(scores, carry):
            m_new = jnp.maximum(m, jnp.max(s, axis=0, keepdims=True))
            p = jnp.exp2(s - m_new).astype(BF16)
            out.append((m_new, jnp.exp2(m - m_new) * acc + _dot(vt, p)))
        return tuple(out)

    carry = tuple((jnp.full((1, lanes), NEG_BIG, F32), jnp.zeros((ATT_V_ROWS, lanes), F32)) for _ in subs)
    carry = step(kc_ref[...], vtc_ref[:ATT_V_ROWS, :], carry)
    if lat:
        def body(c, carry):
            start = pl.multiple_of(c * tk, tk)
            return step(kl_ref[pl.ds(start, tk), :], vtl_ref[:ATT_V_ROWS, pl.ds(start, tk)], carry)

        n_chunks = kl_ref.shape[0] // tk
        unroll = max(u for u in (4, 2, 1) if n_chunks % u == 0)
        carry = lax.fori_loop(0, n_chunks, body, carry, unroll=unroll)
    for sl, (_, acc) in zip(subs, carry):
        o = acc / acc[ONES_LANE:ONES_LANE + 1]
        o = jnp.concatenate([o, jnp.zeros((HEAD_LANES - ATT_V_ROWS, lanes), F32)], axis=0)
        o_ref[sl, :] = o.T.astype(BF16)


def _attention(qt, k, vt, *, b, n, nc):
    h, t, _ = k.shape
    tq = min(ATT_Q_TILE, n)
    n_q = n // tq
    tk = _row_tile(n)
    lat_blocks = (b * n) // nc
    o_lat = pl.pallas_call(
        functools.partial(_attn_kernel, tk=tk, lat=True),
        grid=(b, h, n_q),
        in_specs=[
            pl.BlockSpec((None, HEAD_LANES, tq), lambda bi, hi, i: (hi, 0, bi * n_q + i)),
            pl.BlockSpec((None, nc, HEAD_LANES), lambda bi, hi, i: (hi, lat_blocks + bi, 0)),
            pl.BlockSpec((None, HEAD_LANES, nc), lambda bi, hi, i: (hi, 0, lat_blocks + bi)),
            pl.BlockSpec((None, n, HEAD_LANES), lambda bi, hi, i: (hi, bi, 0)),
            pl.BlockSpec((None, HEAD_LANES, n), lambda bi, hi, i: (hi, 0, bi)),
        ],
        out_specs=pl.BlockSpec((None, tq, HEAD_LANES), lambda bi, hi, i: (hi, bi * n_q + i, 0)),
        out_shape=jax.ShapeDtypeStruct((h, b * n, HEAD_LANES), BF16),
        compiler_params=_cparams(("parallel", "parallel", "arbitrary")),
        name="attention",
    )(qt, k, vt, k, vt)
    o_ctx = pl.pallas_call(
        functools.partial(_attn_kernel, tk=tk, lat=False),
        grid=(b, h),
        in_specs=[
            pl.BlockSpec((None, HEAD_LANES, nc), lambda bi, hi: (hi, 0, lat_blocks + bi)),
            pl.BlockSpec((None, nc, HEAD_LANES), lambda bi, hi: (hi, lat_blocks + bi, 0)),
            pl.BlockSpec((None, HEAD_LANES, nc), lambda bi, hi: (hi, 0, lat_blocks + bi)),
        ],
        out_specs=pl.BlockSpec((None, nc, HEAD_LANES), lambda bi, hi: (hi, bi, 0)),
        out_shape=jax.ShapeDtypeStruct((h, b * nc, HEAD_LANES), BF16),
        compiler_params=_cparams(("parallel", "parallel")),
        name="attention_ctx",
    )(qt, k, vt)
    return o_lat, o_ctx


def _s5_operators(lam_re, lam_im, log_step, b_re, b_im, c_re, c_im, d_skip):
    tc = S5_CHUNK
    g_n, p_n = lam_re.shape[1], lam_re.shape[2]
    lam = lax.complex(lam_re.astype(F32), lam_im.astype(F32))
    step = jnp.exp(log_step.astype(F32))[..., None]
    la = lam * step
    a_bar = jnp.exp(la)
    bb = ((a_bar - 1.0) / lam)[..., None] * lax.complex(b_re.astype(F32), b_im.astype(F32))
    cm = lax.complex(c_re.astype(F32), c_im.astype(F32))
    kk = jnp.arange(tc + 1, dtype=F32)
    apow = jnp.exp(la[..., None] * kk)

    win_f = jnp.einsum('gps,gpc->gscp', apow[0][..., tc - 1::-1][..., :tc], bb[0])
    win_b = jnp.einsum('gps,gpc->gscp', apow[1][..., :tc], bb[1])
    wout_f = jnp.einsum('gcp,gpt->gptc', cm[0], apow[0][..., 1:])
    wout_b = jnp.einsum('gcp,gpt->gptc', cm[1], apow[1][..., tc:0:-1])
    kf = jnp.real(jnp.einsum('gop,gpk,gpi->gkoi', cm[0], apow[0][..., :tc], bb[0]))
    kb = jnp.real(jnp.einsum('gop,gpk,gpi->gkoi', cm[1], apow[1][..., :tc], bb[1]))
    lag0 = kf[:, :1] + kb[:, :1] + jnp.eye(S5_GROUP, dtype=F32) * d_skip.astype(F32).reshape(g_n, 1, S5_GROUP, 1)
    by_lag = jnp.concatenate([kb[:, tc - 1:0:-1], lag0, kf[:, 1:]], axis=1)
    own = jax.nn.one_hot(jnp.arange(g_n) % 2, 2, dtype=F32)
    lag_rows = jnp.einsum('gkoi,gh->gikho', by_lag, own).reshape(g_n, S5_GROUP, (2 * tc - 1) * 2 * S5_GROUP)
    lane_pad = -lag_rows.shape[2] % LANES
    lag_rows = jnp.concatenate([lag_rows, jnp.zeros((g_n, S5_GROUP, lane_pad), F32)], axis=2)
    loc = _s5_local_operator(lag_rows, tc)

    gp = g_n // 2
    gc2 = 2 * tc * S5_GROUP
    eye2 = jnp.eye(2, dtype=F32)
    pair = lambda x: x.reshape((gp, 2) + x.shape[1:])
    w_in_part = lambda w: jnp.einsum('qgscp,gh->qsgchp', pair(w), eye2).reshape(gp, gc2, 2 * p_n)
    w_out_part = lambda w: jnp.einsum('qgptc,gh->qgpthc', pair(w), eye2).reshape(gp, 2 * p_n, gc2)
    w_in = jnp.concatenate([w_in_part(jnp.real(win_f)), w_in_part(jnp.imag(win_f)),
                            w_in_part(jnp.real(win_b)), w_in_part(jnp.imag(win_b))], axis=2)
    w_out = jnp.concatenate([w_out_part(jnp.real(wout_f)), w_out_part(-jnp.imag(wout_f)),
                             w_out_part(jnp.real(wout_b)), w_out_part(-jnp.imag(wout_b))], axis=1)
    a_tc = apow[..., tc].reshape(2, gp, 2 * p_n)
    a_rows = jnp.stack([jnp.real(a_tc[0]), jnp.imag(a_tc[0]), jnp.real(a_tc[1]), jnp.imag(a_tc[1])], axis=1)
    a_rows = jnp.concatenate([a_rows, jnp.zeros((gp, SUBLANES - 4, 2 * p_n), F32)], axis=1)
    return w_in.astype(BF16), w_out.astype(BF16), loc, a_rows


def _s5_local_kernel(rows_ref, loc_ref, *, tc):
    pw = 2 * S5_GROUP
    for s in range(tc):
        for g in range(2):
            r0 = (s * 2 + g) * S5_GROUP
            loc_ref[r0:r0 + S5_GROUP, :] = rows_ref[g][:, (tc - 1 - s) * pw:(2 * tc - 1 - s) * pw].astype(BF16)


def _s5_local_operator(lag_rows, tc):
    g_n, ch, lanes = lag_rows.shape
    wid = 2 * tc * S5_GROUP
    return pl.pallas_call(
        functools.partial(_s5_local_kernel, tc=tc),
        grid=(g_n // 2,),
        in_specs=[pl.BlockSpec((2, ch, lanes), lambda q: (q, 0, 0))],
        out_specs=pl.BlockSpec((None, wid, wid), lambda q: (q, 0, 0)),
        out_shape=jax.ShapeDtypeStruct((g_n // 2, wid, wid), BF16),
        compiler_params=_cparams(("parallel",)),
        name="s5_local_operator",
    )(lag_rows)


def _s5_kernel(u_ref, win_ref, wout_ref, loc_ref, a_ref, y_ref, s_scr, h_scr, *, batch, lat_chunks, ctx_chunks):
    u = u_ref[...]
    w = a_ref.shape[1]
    n_parts = s_scr.shape[0]
    s_all = _dot(u, win_ref[...])
    for k in range(n_parts):
        s_scr[k] = s_all[:, k * w:(k + 1) * w]
    a = a_ref[...]
    bc = lambda r: jnp.broadcast_to(a[r:r + 1], (batch, w))
    arf, aif, arb, aib = bc(0), bc(1), bc(2), bc(3)

    def chunk(rows, carry, part, ar, ai):
        hr, hi = carry
        sr = s_scr[part, rows, :]
        si = s_scr[part + 1, rows, :]
        h_scr[part, rows, :] = hr
        h_scr[part + 1, rows, :] = hi
        return ar * hr - ai * hi + sr, ar * hi + ai * hr + si

    def segment(base, per):
        def body(j, carry):
            cf, cb = carry
            cf = chunk(pl.ds(base + j, batch, stride=per), cf, 0, arf, aif)
            cb = chunk(pl.ds(base + per - 1 - j, batch, stride=per), cb, 2, arb, aib)
            return cf, cb
        return body

    zero = (jnp.zeros((batch, w), F32), jnp.zeros((batch, w), F32))
    unroll = lambda trips: S5_SCAN_UNROLL if trips % S5_SCAN_UNROLL == 0 else 1
    carry = lax.fori_loop(0, ctx_chunks, segment(batch * lat_chunks, ctx_chunks), (zero, zero),
                          unroll=unroll(ctx_chunks))
    lax.fori_loop(0, lat_chunks, segment(0, lat_chunks), carry, unroll=unroll(lat_chunks))
    h_all = jnp.concatenate([h_scr[k] for k in range(n_parts)], axis=-1).astype(BF16)
    y_ref[...] = _dot(u, loc_ref[...]) + _dot(h_all, wout_ref[...])


def _s5(u, ops, *, b, n, nc):
    w_in, w_out, loc, a_rows = ops
    gp, r, wid = u.shape
    sw = w_in.shape[2]
    return pl.pallas_call(
        functools.partial(_s5_kernel, batch=b, lat_chunks=n // S5_CHUNK, ctx_chunks=nc // S5_CHUNK),
        grid=(gp,),
        in_specs=[
            pl.BlockSpec((None, r, wid), lambda g: (g, 0, 0)),
            pl.BlockSpec((None, wid, sw), lambda g: (g, 0, 0)),
            pl.BlockSpec((None, sw, wid), lambda g: (g, 0, 0)),
            pl.BlockSpec((None, wid, wid), lambda g: (g, 0, 0)),
            pl.BlockSpec((None, SUBLANES, sw // 4), lambda g: (g, 0, 0)),
        ],
        out_specs=pl.BlockSpec((None, r, wid), lambda g: (g, 0, 0)),
        out_shape=jax.ShapeDtypeStruct((gp, r, wid), F32),
        scratch_shapes=[pltpu.VMEM((4, r, sw // 4), F32), pltpu.VMEM((4, r, sw // 4), F32)],
        compiler_params=_cparams(("parallel",)),
        name="s5",
    )(u, w_in, w_out, loc, a_rows)


def _even_out_kernel(x_ref, m_ref, ol_ref, oc_ref, yc_ref, wglu_ref, bglu_ref, woo_ref, wos_ref,
                     g2_ref, wg_ref, wu_ref, wd_ref, out_ref, ys_scr, *, tiles_lat):
    pw = 2 * S5_GROUP
    per_tile = LANES // pw
    chunks = yc_ref.shape[1]
    for lt in range(ys_scr.shape[0]):
        by_pair = [yc_ref[lt * per_tile + l] for l in range(per_tile)]
        for t in range(S5_CHUNK):
            ys_scr[lt, pl.ds(t, chunks, stride=S5_CHUNK), :] = jnp.concatenate(
                [y[:, t * pw:(t + 1) * pw] for y in by_pair], axis=-1)
    y = _gelu_tanh(jnp.concatenate([ys_scr[lt] for lt in range(ys_scr.shape[0])], axis=-1))
    y = y * _sigmoid(_dot(y.astype(BF16), wglu_ref[...]) + bglu_ref[...])
    is_lat = pl.program_id(0) < tiles_lat
    oc = jnp.concatenate([jnp.where(is_lat, ol_ref[h], oc_ref[h]) for h in range(MLA_HEADS)], axis=-1)
    mix = _dot(oc, woo_ref[...]) + _dot(y.astype(BF16), wos_ref[...])
    m = m_ref[...]
    out_ref[...] = _ffn_half_step(x_ref[...] + m[5:6] * mix, m, g2_ref, wg_ref, wu_ref, wd_ref, 2)


def _even_out(tok, m, o_lat, o_ctx, yc, w_glu, b_glu, w_out, g2, wg2, wu2, wd2, *, tm, mod_row):
    t, d = tok.shape
    tiles_lat = o_lat.shape[1] // tm
    pairs, _, wid = yc.shape
    sc = pairs * 2 * S5_GROUP
    wo = w_out[:MLA_HEADS * V_DIM].reshape(MLA_HEADS, V_DIM, d)
    woo = jnp.concatenate([wo, jnp.zeros((MLA_HEADS, HEAD_LANES - V_DIM, d), F32)], axis=1)
    woo = woo.reshape(MLA_HEADS * HEAD_LANES, d).astype(BF16)
    wos = w_out[MLA_HEADS * V_DIM:].astype(BF16)
    wg = w_glu.astype(BF16)
    bg = b_glu.reshape(1, sc).astype(F32)
    gg2 = g2.reshape(1, d)
    full = lambda a: pl.BlockSpec(a.shape, lambda i: (0,) * a.ndim, pipeline_mode=pl.Buffered(1))
    return pl.pallas_call(
        functools.partial(_even_out_kernel, tiles_lat=tiles_lat),
        grid=(t // tm,),
        in_specs=[
            pl.BlockSpec((tm, d), lambda i: (i, 0)),
            pl.BlockSpec((None, N_MOD, d), lambda i: (mod_row(i), 0, 0)),
            pl.BlockSpec((MLA_HEADS, tm, HEAD_LANES), lambda i: (0, jnp.minimum(i, tiles_lat - 1), 0)),
            pl.BlockSpec((MLA_HEADS, tm, HEAD_LANES), lambda i: (0, jnp.maximum(i - tiles_lat, 0), 0)),
            pl.BlockSpec((pairs, tm // S5_CHUNK, wid), lambda i: (0, i, 0)),
            full(wg), full(bg), full(woo), full(wos), full(gg2), full(wg2), full(wu2), full(wd2),
        ],
        out_specs=pl.BlockSpec((tm, d), lambda i: (i, 0)),
        out_shape=jax.ShapeDtypeStruct((t, d), F32),
        scratch_shapes=[pltpu.VMEM((sc // LANES, tm, LANES), F32)],
        compiler_params=_cparams(("parallel",)),
        name="even_out_ffn",
    )(tok, m, o_lat, o_ctx, yc, wg, bg, woo, wos, gg2, wg2, wu2, wd2)


def _odd_in_kernel(x_ref, m_ref, g1_ref, wg_ref, wu_ref, wd_ref, g_ref, w_ref,
                   tok_out, hy_out, lx_out, lg_out, lxc_out, *, tiles_lat):
    m = m_ref[...]
    x = _ffn_half_step(x_ref[...], m, g1_ref, wg_ref, wu_ref, wd_ref, 0)
    hb = (_rms(x, g_ref[...]) * (1.0 + m[4:5]) + m[3:4]).astype(BF16)
    n_hy, n_lx = hy_out.shape[1], lx_out.shape[1]
    is_lat = pl.program_id(0) < tiles_lat

    @pl.when(is_lat)
    def _():
        tok_out[...] = x
        p = _dot(hb, w_ref[...])
        hy_out[...] = p[:, :n_hy]
        lx_out[...] = p[:, n_hy:n_hy + n_lx]
        lg_out[...] = p[:, n_hy + n_lx:]

    @pl.when(jnp.logical_not(is_lat))
    def _():
        lxc_out[...] = _dot(hb, w_ref[:, n_hy:n_hy + n_lx])


def _odd_in(tok, m, ffn1, g, w_in, *, hy_w, lru_w, tm, mod_row, n_lat_rows):
    t, d = tok.shape
    g1, wg1, wu1, wd1 = ffn1
    gg1 = g1.reshape(1, d)
    w = w_in.astype(BF16)
    gg = g.reshape(1, d)
    tiles_lat = n_lat_rows // tm
    full = lambda a: pl.BlockSpec(a.shape, lambda i: (0,) * a.ndim, pipeline_mode=pl.Buffered(1))
    lat_row = lambda c: pl.BlockSpec((tm, c), lambda i: (jnp.minimum(i, tiles_lat - 1), 0))
    ctx_row = lambda c: pl.BlockSpec((tm, c), lambda i: (jnp.maximum(i - tiles_lat, 0), 0))
    return pl.pallas_call(
        functools.partial(_odd_in_kernel, tiles_lat=tiles_lat),
        grid=(t // tm,),
        in_specs=[pl.BlockSpec((tm, d), lambda i: (i, 0)),
                  pl.BlockSpec((None, N_MOD, d), lambda i: (mod_row(i), 0, 0)),
                  full(gg1), full(wg1), full(wu1), full(wd1), full(gg), full(w)],
        out_specs=[lat_row(d), lat_row(hy_w), lat_row(lru_w), lat_row(lru_w), ctx_row(lru_w)],
        out_shape=[jax.ShapeDtypeStruct((n_lat_rows, c), F32) for c in (d, hy_w, lru_w, lru_w)]
        + [jax.ShapeDtypeStruct((t - n_lat_rows, lru_w), F32)],
        compiler_params=_cparams(("arbitrary",)),
        name="ffn_odd_in",
    )(tok, m, gg1, wg1, wu1, wd1, gg, w)


def _shift_down(cur, prev8, k):
    r = pltpu.roll(cur, k, 0)
    row = lax.broadcasted_iota(jnp.int32, prev8.shape, 0)
    head = jnp.where(row < k, pltpu.roll(prev8, k, 0), r[:SUBLANES])
    return jnp.concatenate([head, r[SUBLANES:]], axis=0)


def _shift_up(cur, next8, k):
    rows = cur.shape[0]
    r = pltpu.roll(cur, rows - k, 0)
    row = lax.broadcasted_iota(jnp.int32, next8.shape, 0)
    tail = jnp.where(row >= SUBLANES - k, pltpu.roll(next8, SUBLANES - k, 0), r[rows - SUBLANES:])
    return jnp.concatenate([r[:rows - SUBLANES], tail], axis=0)


def _halo_specs(tt, width, n_tiles, tile_of):
    per = tt // SUBLANES
    last = n_tiles * per - 1
    cur = pl.BlockSpec((None, tt, width), lambda b, i: (b, tile_of(i), 0))
    prev = pl.BlockSpec((None, SUBLANES, width), lambda b, i: (b, jnp.maximum(tile_of(i) * per - 1, 0), 0))
    nxt = pl.BlockSpec((None, SUBLANES, width), lambda b, i: (b, jnp.minimum((tile_of(i) + 1) * per, last), 0))
    return cur, prev, nxt


def _hyena_prep_kernel(cur_ref, prev_ref, next_ref, w_ref, b_ref, z_out, x0_out):
    i = pl.program_id(1)
    cur = cur_ref[...]
    prev8 = prev_ref[...] * (i > 0).astype(F32)
    next8 = next_ref[...] * (i < pl.num_programs(1) - 1).astype(F32)
    w = w_ref[...]
    u = w[0:1] * _shift_down(cur, prev8, 1) + w[1:2] * cur + w[2:3] * _shift_up(cur, next8, 1) + b_ref[...]
    c = z_out.shape[1]
    x0_out[...] = u[:, :c]
    z_out[...] = u[:, 2 * c:] * u[:, c:2 * c]


def _hyena_prep(hy, conv_w, conv_b, *, tt):
    b, n, c3 = hy.shape
    c = c3 // 3
    cur, prev, nxt = _halo_specs(tt, c3, n // tt, lambda i: i)
    w8 = jnp.concatenate([conv_w.astype(F32), jnp.zeros((SUBLANES - conv_w.shape[0], c3), F32)], axis=0)
    out_spec = pl.BlockSpec((None, tt, c), lambda bi, i: (bi, i, 0))
    return pl.pallas_call(
        _hyena_prep_kernel,
        grid=(b, n // tt),
        in_specs=[cur, prev, nxt, pl.BlockSpec((SUBLANES, c3), lambda bi, i: (0, 0)),
                  pl.BlockSpec((1, c3), lambda bi, i: (0, 0))],
        out_specs=[out_spec, out_spec],
        out_shape=[jax.ShapeDtypeStruct((b, n, c), F32)] * 2,
        compiler_params=_cparams(("parallel", "parallel")),
        name="hyena_prep",
    )(hy, hy, hy, w8, conv_b.reshape(1, c3).astype(F32))


def _gather_pitch(rows):
    p = -(-rows // SUBLANES)
    return (p + 1 - p % 2) * SUBLANES


class _Dft:
    def __init__(self, n):
        self.n = n
        self.N = 2 * n
        self.N2 = FFT_N2
        self.N1 = self.N // self.N2
        self.A = n // self.N2
        self.K1 = self.N1 // 2 + 1
        self.K1p = -(-self.K1 // SUBLANES) * SUBLANES
        self.y_pitch = _gather_pitch(2 * self.K1p)
        self.x_pitch = _gather_pitch(2 * self.N2)
        N, N1, N2, A, K1, K1p = self.N, self.N1, self.N2, self.A, self.K1, self.K1p
        b = np.arange(N2)[:, None, None]
        k1 = np.arange(K1p)[None, :, None]
        a = np.arange(A)[None, None, :]
        phi = 2.0 * np.pi * ((a * k1 % N1) / N1 + (b * k1 % N) / N)
        live = (k1 < K1).astype(np.float64)
        f1 = np.concatenate([np.cos(phi) * live, -np.sin(phi) * live], axis=1)
        f1 = np.concatenate([f1[0::2], f1[1::2]], axis=2)
        wgt = np.where((k1 == 0) | (k1 == N1 // 2), 1.0, 2.0) * live / N
        g1 = np.concatenate([np.cos(phi) * wgt, -np.sin(phi) * wgt], axis=1)
        g1 = np.transpose(g1, (0, 2, 1))
        ang = 2.0 * np.pi * (np.arange(N2)[:, None] * np.arange(N2)[None, :] % N2) / N2
        c, s = np.cos(ang), np.sin(ang)
        f2 = np.block([[c, s], [-s, c]])
        g2 = np.block([[c, -s], [s, c]])
        self.f1, self.f2, self.g2, self.g1 = (jnp.asarray(x, F32).astype(BF16) for x in (f1, f2, g2, g1))


def _dft_forward(src_ref, f1_ref, f2_ref, y_scr, put, dft):
    n2, a_n, k1p2, yp = dft.N2, dft.A, 2 * dft.K1p, dft.y_pitch

    zero = jnp.zeros((a_n, LANES), BF16)

    def stage1(i, _):
        b = 2 * i
        x0 = src_ref[pl.ds(b, a_n, stride=n2), :].astype(BF16)
        x1 = src_ref[pl.ds(b + 1, a_n, stride=n2), :].astype(BF16)
        rhs = jnp.concatenate([jnp.concatenate([x0, zero], axis=1), jnp.concatenate([zero, x1], axis=1)], axis=0)
        y = _dot(f1_ref[i], rhs)
        for j in range(2):
            r0 = pl.multiple_of((b + j) * yp, SUBLANES)
            y_scr[pl.ds(r0, k1p2), :] = y[:, j * LANES:(j + 1) * LANES]
        return 0

    lax.fori_loop(0, n2 // 2, stage1, 0, unroll=DFT_UNROLL // 2)

    def plane(k1):
        yr = y_scr[pl.ds(k1, n2, stride=yp), :]
        yi = y_scr[pl.ds(dft.K1p + k1, n2, stride=yp), :]
        return jnp.concatenate([yr, yi], axis=0).astype(BF16)

    def stage2_pair(i, _):
        k1 = 2 * i
        put(k1, _dot(f2_ref[...], jnp.concatenate([plane(k1), plane(k1 + 1)], axis=1)), 2)
        return 0

    pairs = (dft.K1 // 2) - (dft.K1 // 2) % (DFT_UNROLL // 2)
    lax.fori_loop(0, pairs, stage2_pair, 0, unroll=DFT_UNROLL // 2)
    for k1 in range(2 * pairs, dft.K1):
        put(k1, _dot(f2_ref[...], plane(k1)), 1)


def _hyena_filter_kernel(feat_ref, w1_ref, b1_ref, w2_ref, b2_ref, w3f_ref, w3b_ref, freq_ref, delta_ref,
                         f1_ref, f2_ref, h_out, filt_scr, y_scr, hid_scr, *, dft):
    n = dft.n
    rb = _row_tile(n)
    blocks = n // rb
    rows_of = lambda r: pl.ds(pl.multiple_of(r * rb, rb), rb)

    @pl.when(pl.program_id(0) == 0)
    def _():
        freq = freq_ref[...]
        w1_hi, w1_lo = _split(w1_ref[...])
        w2_hi, w2_lo = _split(w2_ref[...])

        def hidden(r, _):
            f_hi, f_lo = _split(feat_ref[rows_of(r), :])
            h = jnp.sin(freq[0:1] * (_dot3(f_hi, f_lo, w1_hi, w1_lo) + b1_ref[...]))
            h_hi, h_lo = _split(h)
            hid_scr[rows_of(r), :] = jnp.sin(freq[1:2] * (_dot3(h_hi, h_lo, w2_hi, w2_lo) + b2_ref[...]))
            return 0

        lax.fori_loop(0, blocks, hidden, 0)

    def fill_filter(w3_ref):
        w_hi, w_lo = _split(w3_ref[...])

        def blk(r, _):
            h_hi, h_lo = _split(hid_scr[rows_of(r), :])
            decay = jnp.exp(-feat_ref[rows_of(r), 0:1] * delta_ref[...])
            filt_scr[rows_of(r), :] = _dot3(h_hi, h_lo, w_hi, w_lo) * decay
            return 0

        lax.fori_loop(0, blocks, blk, 0)

    n2 = dft.N2
    fill_filter(w3f_ref)

    def put_fwd(k1, x, w):
        for j in range(w):
            h_out[k1 + j] = x[:, j * LANES:(j + 1) * LANES]

    _dft_forward(filt_scr, f1_ref, f2_ref, y_scr, put_fwd, dft)

    fill_filter(w3b_ref)
    first = lax.broadcasted_iota(jnp.int32, (SUBLANES, LANES), 0) == 0
    filt_scr[:SUBLANES, :] = jnp.where(first, 0.0, filt_scr[:SUBLANES, :])
    sign = jnp.where(lax.broadcasted_iota(jnp.int32, (2 * n2, LANES), 0) < n2, 1.0, -1.0)

    def put_bwd(k1, x, w):
        for j in range(w):
            h_out[k1 + j] = h_out[k1 + j] + sign * x[:, j * LANES:(j + 1) * LANES]

    _dft_forward(filt_scr, f1_ref, f2_ref, y_scr, put_bwd, dft)


def _hyena_features(n):
    t = np.linspace(0.0, 1.0, n, dtype=np.float32).astype(np.float64)[:, None]
    w = (2.0 * np.pi * np.arange(n, dtype=np.float64)[:, None] / n).astype(np.float32).astype(np.float64)
    bands = np.linspace(1e-4, HY_BANDS - 1, HY_BANDS, dtype=np.float32).astype(np.float64)[None, :]
    bw = (bands * w).astype(np.float32).astype(np.float64)
    z = np.concatenate([t, np.cos(bw), -np.sin(bw)], axis=-1)
    z = np.concatenate([z, np.zeros((n, LANES - z.shape[1]))], axis=-1)
    return jnp.asarray(z, F32)


def _hyena_filter_spectrum(n, w1, b1, w2, b2, w3, sin_freq, dft):
    ch = w3.shape[1] // 2
    hid = w1.shape[1]
    feat = _hyena_features(n)
    w1p = jnp.concatenate([w1.astype(F32), jnp.zeros((LANES - w1.shape[0], hid), F32)], axis=0)
    deltas = jnp.asarray(np.abs(np.linspace(HY_MIN_DECAY, HY_MAX_DECAY, ch, dtype=np.float32)), F32).reshape(1, ch)
    freq8 = jnp.concatenate([sin_freq.astype(F32), jnp.zeros((SUBLANES - 2, hid), F32)], axis=0)
    full = lambda a: pl.BlockSpec(a.shape, lambda c: (0,) * a.ndim, pipeline_mode=pl.Buffered(1))
    tiles = ch // LANES
    args = (feat, w1p, b1.reshape(1, hid).astype(F32), w2.astype(F32), b2.reshape(1, hid).astype(F32))
    return pl.pallas_call(
        functools.partial(_hyena_filter_kernel, dft=dft),
        grid=(tiles,),
        in_specs=[full(a) for a in args] + [
            pl.BlockSpec((hid, LANES), lambda c: (0, c)),
            pl.BlockSpec((hid, LANES), lambda c: (0, tiles + c)),
            full(freq8),
            pl.BlockSpec((1, LANES), lambda c: (0, c)),
            full(dft.f1), full(dft.f2),
        ],
        out_specs=pl.BlockSpec((dft.K1, 2 * dft.N2, LANES), lambda c: (0, 0, c)),
        out_shape=jax.ShapeDtypeStruct((dft.K1, 2 * dft.N2, ch), F32),
        scratch_shapes=[pltpu.VMEM((n, LANES), F32), pltpu.VMEM((dft.N2 * dft.y_pitch, LANES), F32),
                        pltpu.VMEM((n, hid), F32)],
        compiler_params=_cparams(("arbitrary",)),
        name="hyena_filter",
    )(*args, w3.astype(F32), w3.astype(F32), freq8, deltas, dft.f1, dft.f2)


def _hyena_conv_kernel(z_ref, h_ref, f1_ref, f2_ref, g2_ref, g1_ref, y_out, y_scr, x_scr, *, dft):
    n2, a_n, k1p, xp = dft.N2, dft.A, dft.K1p, dft.x_pitch

    def put(k1, x, w):
        hk = jnp.concatenate([h_ref[k1 + j] for j in range(w)], axis=1)
        xr, xi, hr, hi = x[:n2], x[n2:], hk[:n2], hk[n2:]
        prod = jnp.concatenate([xr * hr - xi * hi, xr * hi + xi * hr], axis=0).astype(BF16)
        inv = _dot(g2_ref[...], prod)
        for j in range(w):
            r0 = pl.multiple_of((k1 + j) * xp, SUBLANES)
            x_scr[pl.ds(r0, 2 * n2), :] = inv[:, j * LANES:(j + 1) * LANES]

    if k1p > dft.K1:
        x_scr[dft.K1 * xp:, :] = jnp.zeros(((k1p - dft.K1) * xp, LANES), F32)
    _dft_forward(z_ref, f1_ref, f2_ref, y_scr, put, dft)

    def last(b, _):
        yr = x_scr[pl.ds(b, k1p, stride=xp), :]
        yi = x_scr[pl.ds(n2 + b, k1p, stride=xp), :]
        y_out[pl.ds(b, a_n, stride=n2), :] = _dot(g1_ref[b], jnp.concatenate([yr, yi], axis=0).astype(BF16))
        return 0

    lax.fori_loop(0, n2, last, 0, unroll=DFT_UNROLL)


def _hyena_conv(z, spec, dft):
    b, n, ch = z.shape
    once = pl.Buffered(1)
    full = lambda a: pl.BlockSpec(a.shape, lambda c, bi: (0,) * a.ndim, pipeline_mode=once)
    io_spec = pl.BlockSpec((None, n, LANES), lambda c, bi: (bi, 0, c))
    return pl.pallas_call(
        functools.partial(_hyena_conv_kernel, dft=dft),
        grid=(ch // LANES, b),
        in_specs=[io_spec, pl.BlockSpec((dft.K1, 2 * dft.N2, LANES), lambda c, bi: (0, 0, c), pipeline_mode=once),
                  full(dft.f1), full(dft.f2), full(dft.g2), full(dft.g1)],
        out_specs=io_spec,
        out_shape=jax.ShapeDtypeStruct((b, n, ch), F32),
        scratch_shapes=[pltpu.VMEM((dft.N2 * dft.y_pitch, LANES), F32), pltpu.VMEM((dft.K1p * dft.x_pitch, LANES), F32)],
        compiler_params=_cparams(("parallel", "arbitrary")),
        name="hyena_conv",
    )(z, spec, dft.f1, dft.f2, dft.g2, dft.g1)


def _lru_tile_terms(cur, prev8, next8, conv_w, conv_b, wg, bg, sp, reverse):
    rows, c = cur.shape
    xc = (conv_w[0:1] * _shift_down(cur, prev8, 2) + conv_w[1:2] * _shift_down(cur, prev8, 1)
          + conv_w[2:3] * cur + conv_w[3:4] * _shift_up(cur, next8, 1) + conv_b)
    g = _dot(xc.astype(BF16), wg) + bg
    gate = 0.5 + 0.5 * jnp.tanh(0.5 * g)
    r, ig = gate[:, :c], gate[:, c:]
    log_a = -LRU_C * r * sp
    a = jnp.exp(log_a)
    th = jnp.tanh(log_a)
    bb = jnp.sqrt(-2.0 * th / (1.0 - th)) * (ig * xc)
    groups = rows // SUBLANES
    a, bb = a.reshape(groups, SUBLANES, c), bb.reshape(groups, SUBLANES, c)
    sub = lax.broadcasted_iota(jnp.int32, (groups, SUBLANES, c), 1)
    for k in (1, 2, 4):
        shift = SUBLANES - k if reverse else k
        ok = sub < SUBLANES - k if reverse else sub >= k
        a_s, b_s = pltpu.roll(a, shift, 1), pltpu.roll(bb, shift, 1)
        bb = jnp.where(ok, bb + a * b_s, bb)
        a = jnp.where(ok, a * a_s, a)
    return a.reshape(rows, c), bb.reshape(rows, c)


def _lru_kernel(ctx_ref, curf_ref, prevf_ref, nextf_ref, curb_ref, prevb_ref, nextb_ref,
                cw_ref, cb_ref, wgf_ref, wgb_ref, bgf_ref, bgb_ref, sp_ref,
                hf_out, hb_out, a_scr, b_scr, carry_scr, *, batch, tt):
    i = pl.program_id(0)
    nt = pl.num_programs(0)
    cw, cb = cw_ref[...], cb_ref[...]
    sp = sp_ref[...]
    c = cw.shape[1]
    params = ((wgf_ref, bgf_ref, sp[0:1]), (wgb_ref, bgb_ref, sp[1:2]))

    def carry_scan(rows, write):
        groups = rows // SUBLANES

        def body(j, carries):
            new = []
            for d in range(2):
                jj = j if d == 0 else groups - 1 - j
                r0 = pl.multiple_of(jj * SUBLANES, SUBLANES)
                for bi in range(batch):
                    h8 = b_scr[d, bi, pl.ds(r0, SUBLANES), :] + a_scr[d, bi, pl.ds(r0, SUBLANES), :] * carries[d * batch + bi]
                    write(d, bi, r0, h8)
                    edge = h8[SUBLANES - 1:SUBLANES] if d == 0 else h8[0:1]
                    new.append(jnp.broadcast_to(edge, (SUBLANES, c)))
            return tuple(new)

        init = tuple(carry_scr[d, bi] for d in range(2) for bi in range(batch))
        out = lax.fori_loop(0, groups, body, init)
        for d in range(2):
            for bi in range(batch):
                carry_scr[d, bi] = out[d * batch + bi]

    @pl.when(i == 0)
    def _():
        carry_scr[...] = jnp.zeros(carry_scr.shape, F32)
        nc = ctx_ref.shape[1]
        zero8 = jnp.zeros((SUBLANES, c), F32)
        for d in range(2):
            wg, bg, spd = params[d]
            for bi in range(batch):
                a, bb = _lru_tile_terms(ctx_ref[bi], zero8, zero8, cw, cb, wg[...], bg[...], spd, d == 1)
                a_scr[d, bi, :nc, :] = a
                b_scr[d, bi, :nc, :] = bb
        carry_scan(nc, lambda d, bi, r0, h8: None)

    tiles = ((curf_ref, prevf_ref, nextf_ref, i), (curb_ref, prevb_ref, nextb_ref, nt - 1 - i))
    for d in range(2):
        cur_ref, prev_ref, next_ref, ti = tiles[d]
        wg, bg, spd = params[d]
        has_prev = (ti > 0).astype(F32)
        has_next = (ti < nt - 1).astype(F32)
        for bi in range(batch):
            a, bb = _lru_tile_terms(cur_ref[bi], prev_ref[bi] * has_prev, next_ref[bi] * has_next,
                                    cw, cb, wg[...], bg[...], spd, d == 1)
            a_scr[d, bi, :tt, :] = a
            b_scr[d, bi, :tt, :] = bb

    def write(d, bi, r0, h8):
        if d == 0:
            hf_out[bi, pl.ds(r0, SUBLANES), :] = h8
        else:
            hb_out[bi, pl.ds(r0, SUBLANES), :] = h8

    carry_scan(tt, write)


def _lru_gate_weights(w_a, w_x):
    def dense(w):
        nb, bs = w.shape[0], w.shape[1]
        eye = jnp.eye(nb, dtype=F32)
        return jnp.einsum('hij,hk->hikj', w.astype(F32), eye).reshape(nb * bs, nb * bs)
    return [jnp.concatenate([dense(w_a[d]), dense(w_x[d])], axis=1).astype(BF16) for d in range(2)]


def _lru(lx_lat, lx_ctx, conv_w, conv_b, w_a, b_a, w_x, b_x, lam, *, tt):
    b, n, c = lx_lat.shape
    nc = lx_ctx.shape[1]
    nt = n // tt
    wgf, wgb = _lru_gate_weights(w_a, w_x)
    bgf = jnp.concatenate([b_a[0], b_x[0]]).reshape(1, 2 * c).astype(F32)
    bgb = jnp.concatenate([b_a[1], b_x[1]]).reshape(1, 2 * c).astype(F32)
    sp = jax.nn.softplus(-lam.astype(F32))
    sp8 = jnp.concatenate([sp, jnp.zeros((SUBLANES - 2, c), F32)], axis=0)
    cw8 = jnp.concatenate([conv_w.astype(F32), jnp.zeros((SUBLANES - conv_w.shape[0], c), F32)], axis=0)
    cb = conv_b.reshape(1, c).astype(F32)
    per = tt // SUBLANES
    last = n // SUBLANES - 1

    def specs(tile_of):
        cur = pl.BlockSpec((b, tt, c), lambda i: (0, tile_of(i), 0))
        prev = pl.BlockSpec((b, SUBLANES, c), lambda i: (0, jnp.maximum(tile_of(i) * per - 1, 0), 0))
        nxt = pl.BlockSpec((b, SUBLANES, c), lambda i: (0, jnp.minimum((tile_of(i) + 1) * per, last), 0))
        return [cur, prev, nxt]

    fwd_tile = lambda i: i
    bwd_tile = lambda i: nt - 1 - i
    full = lambda a: pl.BlockSpec(a.shape, lambda i: (0,) * a.ndim)
    rows_scr = max(tt, nc)
    return pl.pallas_call(
        functools.partial(_lru_kernel, batch=b, tt=tt),
        grid=(nt,),
        in_specs=[full(lx_ctx)] + specs(fwd_tile) + specs(bwd_tile)
        + [full(cw8), full(cb), full(wgf), full(wgb), full(bgf), full(bgb), full(sp8)],
        out_specs=[pl.BlockSpec((b, tt, c), lambda i: (0, i, 0)), pl.BlockSpec((b, tt, c), lambda i: (0, nt - 1 - i, 0))],
        out_shape=[jax.ShapeDtypeStruct((b, n, c), F32)] * 2,
        scratch_shapes=[pltpu.VMEM((2, b, rows_scr, c), F32), pltpu.VMEM((2, b, rows_scr, c), F32),
                        pltpu.VMEM((2, b, SUBLANES, c), F32)],
        compiler_params=_cparams(("arbitrary",)),
        name="rglru",
    )(lx_ctx, lx_lat, lx_lat, lx_lat, lx_lat, lx_lat, lx_lat, cw8, cb, wgf, wgb, bgf, bgb, sp8)


def _odd_out_kernel(x_ref, m_ref, yc_ref, z_ref, x0_ref, hb_ref, hf_ref, hbk_ref, lg_ref, wh_ref, wr_ref,
                    g2_ref, wg_ref, wu_ref, wd_ref, fn_ref, out_ref):
    z = z_ref[...]
    hy = x0_ref[...] * (yc_ref[...] + z * hb_ref[...])
    r = (hf_ref[...] + hbk_ref[...]) * _gelu_tanh(lg_ref[...])
    mix = _dot(hy.astype(BF16), wh_ref[...]) + _dot(r.astype(BF16), wr_ref[...])
    m = m_ref[...]
    y = _ffn_half_step(x_ref[...] + m[5:6] * mix, m, g2_ref, wg_ref, wu_ref, wd_ref, 2)
    out_ref[...] = _rms(y, fn_ref[...])


def _odd_out(tok, m, yc, z, x0, hy_bias, hf, hb, lg, w_out, g2, wg2, wu2, wd2, final_g, *, n_rows, tm, mod_row):
    d = tok.shape[1]
    c = yc.shape[1]
    wh, wr = w_out[:c].astype(BF16), w_out[c:].astype(BF16)
    hbias = hy_bias.reshape(1, c).astype(F32)
    gg2, fg = g2.reshape(1, d), final_g.reshape(1, d)
    full = lambda a: pl.BlockSpec(a.shape, lambda i: (0,) * a.ndim, pipeline_mode=pl.Buffered(1))
    row = lambda w: pl.BlockSpec((tm, w), lambda i: (i, 0))
    return pl.pallas_call(
        _odd_out_kernel,
        grid=(n_rows // tm,),
        in_specs=[row(d), pl.BlockSpec((None, N_MOD, d), lambda i: (mod_row(i), 0, 0)),
                  row(c), row(c), row(c), full(hbias), row(c), row(c), row(c), full(wh), full(wr),
                  full(gg2), full(wg2), full(wu2), full(wd2), full(fg)],
        out_specs=row(d),
        out_shape=jax.ShapeDtypeStruct((n_rows, d), F32),
        compiler_params=_cparams(("parallel",)),
        name="odd_out_ffn",
    )(tok, m, yc, z, x0, hbias, hf, hb, lg, wh, wr, gg2, wg2, wu2, wd2, fg)


def kernel(x, c, ctx, c_ctx, mod_w, mod_b, norm_ffn1, norm_mix, norm_ffn2, ffn1_w_gate, ffn1_w_up, ffn1_w_down, ffn2_w_gate, ffn2_w_up, ffn2_w_down, ev_w_in, mla_q_norm, mla_w_uq, mla_kv_norm, mla_w_ukv, s5_lambda_re, s5_lambda_im, s5_log_step, s5_b_re, s5_b_im, s5_c_re, s5_c_im, s5_d, s5_w_glu, s5_b_glu, ev_w_out, od_w_in, hy_conv_w, hy_conv_b, hy_filt_w1, hy_filt_b1, hy_filt_w2, hy_filt_b2, hy_filt_w3, hy_sin_freq, hy_bias, lru_conv_w, lru_conv_b, lru_w_a, lru_b_a, lru_w_x, lru_b_x, lru_lambda, od_w_out, final_norm):
    b, n, d = x.shape
    nc = ctx.shape[1]
    depth = mod_w.shape[0]
    assert depth == 2, "the trunk is laid out for one even and one odd layer"
    n_lat_rows = b * n
    t = n_lat_rows + b * nc
    tm = _row_tile(n, b * nc)
    tpb = n // tm
    mod_row = lambda i: jnp.minimum(i // tpb, b)

    cond8 = jnp.concatenate([c, c_ctx[None, :], jnp.zeros((SUBLANES - b - 1, d), F32)], axis=0)
    m = _adaln(cond8, mod_w, mod_b)
    bf = lambda w: w.astype(BF16)

    ev_w = _even_in_weights(ev_w_in[0], mla_w_uq[0], mla_w_ukv[0])
    ffn1 = (norm_ffn1[0], bf(ffn1_w_gate[0]), bf(ffn1_w_up[0]), bf(ffn1_w_down[0]))
    tok, qt, k, vt, s_chunks = _even_in(x.reshape(n_lat_rows, d), ctx.reshape(b * nc, d), m[0], ffn1, norm_mix[0],
                                        ev_w, mla_q_norm[0], mla_kv_norm[0], n=n, tm=tm, mod_row=mod_row)
    o_lat, o_ctx = _attention(qt, k, vt, b=b, n=n, nc=nc)
    s5_ops = _s5_operators(s5_lambda_re[0], s5_lambda_im[0], s5_log_step[0], s5_b_re[0], s5_b_im[0],
                           s5_c_re[0], s5_c_im[0], s5_d[0])
    ys = _s5(s_chunks, s5_ops, b=b, n=n, nc=nc)
    tok = _even_out(tok, m[0], o_lat, o_ctx, ys, s5_w_glu[0], s5_b_glu[0], ev_w_out[0], norm_ffn2[0], bf(ffn2_w_gate[0]),
                    bf(ffn2_w_up[0]), bf(ffn2_w_down[0]), tm=tm, mod_row=mod_row)

    hy_w = hy_conv_w.shape[2]
    lru_w = lru_conv_w.shape[2]
    ffn1 = (norm_ffn1[1], bf(ffn1_w_gate[1]), bf(ffn1_w_up[1]), bf(ffn1_w_down[1]))
    tok, hy, lx, lg, lx_ctx = _odd_in(tok, m[1], ffn1, norm_mix[1], od_w_in[0], hy_w=hy_w, lru_w=lru_w, tm=tm,
                                      mod_row=mod_row, n_lat_rows=n_lat_rows)
    tt = _row_tile(n)
    z, x0 = _hyena_prep(hy.reshape(b, n, hy_w), hy_conv_w[0], hy_conv_b[0], tt=tt)
    dft = _Dft(n)
    spec = _hyena_filter_spectrum(n, hy_filt_w1[0], hy_filt_b1[0], hy_filt_w2[0], hy_filt_b2[0], hy_filt_w3[0],
                                  hy_sin_freq[0], dft)
    yc = _hyena_conv(z, spec, dft)
    hf, hb = _lru(lx.reshape(b, n, lru_w), lx_ctx.reshape(b, nc, lru_w), lru_conv_w[0],
                  lru_conv_b[0], lru_w_a[0], lru_b_a[0], lru_w_x[0], lru_b_x[0], lru_lambda[0], tt=min(tt, 256))
    flat = lambda a: a.reshape(n_lat_rows, a.shape[2])
    lat = _odd_out(tok, m[1], flat(yc), flat(z), flat(x0), hy_bias[0], flat(hf), flat(hb), lg, od_w_out[0],
                   norm_ffn2[1], bf(ffn2_w_gate[1]), bf(ffn2_w_up[1]), bf(ffn2_w_down[1]), final_norm,
                   n_rows=n_lat_rows, tm=tm, mod_row=mod_row)
    return lat.reshape(b, n, d)
```

```python
import functools
import math

import numpy as np
import jax
import jax.numpy as jnp
from jax import lax
from jax.experimental import pallas as pl
from jax.experimental.pallas import tpu as pltpu

F32 = jnp.float32
BF16 = jnp.bfloat16

GRID_W = 64
N_MOD = 9
NORM_EPS = 1e-6
ROPE_BASE = 10000.0
MLA_HEADS = 8
QK_NOPE = 64
QK_ROPE = 32
V_DIM = 64
Q_RANK = 384
KV_RANK = 256
S5_GROUP = 16
HY_BANDS = 16
HY_DECAY_TARGET = 1e-2
HY_MIN_DECAY = math.log(HY_DECAY_TARGET) / 1.5
HY_MAX_DECAY = math.log(HY_DECAY_TARGET) / 0.3
LRU_C = 8.0

LANES = 128
SUBLANES = 8
BF16_TILE_ROWS = 16
MXU_WIDTH_V7X = 256
VMEM_BYTES_V7X = 64 * 1024 * 1024
VMEM_LIMIT = VMEM_BYTES_V7X - 8 * 1024 * 1024

HEAD_LANES = LANES
ONES_LANE = V_DIM
ATT_V_ROWS = V_DIM + BF16_TILE_ROWS
ATT_Q_TILE = 2048
ATT_Q_LANES = MXU_WIDTH_V7X
S5_CHUNK = 8
S5_SCAN_UNROLL = 8
FFT_N2 = 128
DFT_UNROLL = 8
FFN_CHUNK = MXU_WIDTH_V7X
NEG_BIG = -1e30


def _cparams(sem):
    return pltpu.CompilerParams(dimension_semantics=sem, vmem_limit_bytes=VMEM_LIMIT)


def _dot(a, b):
    return jnp.dot(a, b, preferred_element_type=F32)


def _split(x):
    hi = x.astype(BF16)
    lo = (x - hi.astype(F32)).astype(BF16)
    return hi, lo


def _dot3(a_hi, a_lo, b_hi, b_lo):
    return _dot(a_hi, b_hi) + _dot(a_lo, b_hi) + _dot(a_hi, b_lo)


def _sigmoid(x):
    return 1.0 / (1.0 + jnp.exp(-x))


def _gelu_tanh(x):
    return 0.5 * x * (1.0 + jnp.tanh(math.sqrt(2.0 / math.pi) * (x + 0.044715 * (x * x * x))))


def _rms(x, g):
    return x * lax.rsqrt(jnp.mean(x * x, axis=-1, keepdims=True) + NORM_EPS) * g


def _row_tile(*lengths):
    for t in (512, 256, 128, 64, 32, 16, 8):
        if all(l % t == 0 for l in lengths):
            return t
    raise ValueError(f"no row tile divides {lengths}")


def _bf16_kernel(w_ref, o_ref):
    o_ref[...] = w_ref[...].astype(BF16)


def _bf16_layer(w_stack, layer):
    _, r, c = w_stack.shape
    tr = next(t for t in range(512, 0, -BF16_TILE_ROWS) if r % t == 0)
    return pl.pallas_call(
        _bf16_kernel,
        grid=(r // tr,),
        in_specs=[pl.BlockSpec((None, tr, c), lambda i: (layer, i, 0))],
        out_specs=pl.BlockSpec((tr, c), lambda i: (i, 0)),
        out_shape=jax.ShapeDtypeStruct((r, c), BF16),
        compiler_params=_cparams(("parallel",)),
        name="to_bf16",
    )(w_stack)


def _adaln_kernel(c_ref, w_ref, b_ref, o_ref):
    c = c_ref[...]
    s_hi, s_lo = _split(c * _sigmoid(c))
    w_hi, w_lo = _split(w_ref[...])
    o_ref[...] = _dot3(s_hi, s_lo, w_hi, w_lo) + b_ref[...]


def _adaln(cond8, mod_w, mod_b):
    depth, d, nd = mod_w.shape
    tn = d
    out = pl.pallas_call(
        _adaln_kernel,
        grid=(depth, nd // tn),
        in_specs=[
            pl.BlockSpec((SUBLANES, d), lambda l, j: (0, 0)),
            pl.BlockSpec((None, d, tn), lambda l, j: (l, 0, j)),
            pl.BlockSpec((None, 1, tn), lambda l, j: (l, 0, j)),
        ],
        out_specs=pl.BlockSpec((None, SUBLANES, tn), lambda l, j: (l, 0, j)),
        out_shape=jax.ShapeDtypeStruct((depth, SUBLANES, nd), F32),
        compiler_params=_cparams(("arbitrary", "arbitrary")),
        name="adaln",
    )(cond8, mod_w, mod_b.reshape(depth, 1, nd))
    return out.reshape(depth, SUBLANES, N_MOD, d)


def _ffn_half_step(x, m, g_ref, wg_ref, wu_ref, wd_ref, k):
    shift, scale, gate = m[3 * k:3 * k + 1], m[3 * k + 1:3 * k + 2], m[3 * k + 2:3 * k + 3]
    hb = (_rms(x, g_ref[...]) * (1.0 + scale) + shift).astype(BF16)
    f_all = wg_ref.shape[1]
    tf = FFN_CHUNK if f_all % FFN_CHUNK == 0 else LANES
    acc = jnp.zeros(x.shape, F32)
    for f in range(f_all // tf):
        sl = slice(f * tf, (f + 1) * tf)
        gt = _dot(hb, wg_ref[:, sl])
        up = _dot(hb, wu_ref[:, sl])
        acc = acc + _dot((gt * _sigmoid(gt) * up).astype(BF16), wd_ref[sl, :])
    return x + (0.5 * gate) * acc


def _even_in_kernel(x_ref, tail_ref, m_ref, g1_ref, wg_ref, wu_ref, wd_ref,
                    g_ref, w1_ref, qn_ref, kvn_ref, wq_ref, wqp_ref, wk_ref, wv_ref,
                    c_ref, s_ref, tok_out, qt_out, k_out, vt_out, u_out, s_scr, *, tiles_first):
    m = m_ref[...]
    x = jnp.where(pl.program_id(0) < tiles_first, x_ref[...], tail_ref[...])
    x = _ffn_half_step(x, m, g1_ref, wg_ref, wu_ref, wd_ref, 0)
    tok_out[...] = x
    hb = (_rms(x, g_ref[...]) * (1.0 + m[4:5]) + m[3:4]).astype(BF16)
    p = _dot(hb, w1_ref[...])
    o = 0
    cq = p[:, o:o + Q_RANK]; o += Q_RANK
    ckv = p[:, o:o + KV_RANK]; o += KV_RANK
    kra = p[:, o:o + HEAD_LANES]; o += HEAD_LANES
    krb = p[:, o:o + HEAD_LANES]; o += HEAD_LANES
    pw = 2 * S5_GROUP
    per_tile = LANES // pw
    chunks = s_scr.shape[1] // S5_CHUNK
    for lt in range(s_scr.shape[0]):
        s_scr[lt] = p[:, o + lt * LANES:o + (lt + 1) * LANES]
        by_tok = [s_scr[lt, pl.ds(t, chunks, stride=S5_CHUNK), :] for t in range(S5_CHUNK)]
        for l in range(per_tile):
            u_out[lt * per_tile + l] = jnp.concatenate([r[:, l * pw:(l + 1) * pw] for r in by_tok], axis=-1).astype(BF16)
    cos, sin = c_ref[...], s_ref[...]
    cqn = _rms(cq, qn_ref[...]).astype(BF16)
    qa = _dot(cqn, wq_ref[...])
    qb = _dot(cqn, wqp_ref[...])
    ckvn = _rms(ckv, kvn_ref[...]).astype(BF16)
    kn = _dot(ckvn, wk_ref[...])
    vv = _dot(ckvn, wv_ref[...])
    kr = kra * cos + krb * sin
    ones = (lax.broadcasted_iota(jnp.int32, (1, HEAD_LANES), 1) == ONES_LANE).astype(F32)
    scale = (QK_NOPE + QK_ROPE) ** -0.5 * math.log2(math.e)
    for h in range(MLA_HEADS):
        sl = slice(h * HEAD_LANES, (h + 1) * HEAD_LANES)
        qt_out[h] = ((qa[:, sl] * cos + qb[:, sl] * sin) * scale).T.astype(BF16)
        k_out[h] = (kn[:, sl] + kr).astype(BF16)
        vt_out[h] = (vv[:, sl] + ones).T.astype(BF16)


def _rope_partner(w):
    half = w.shape[-1] // 2
    return jnp.concatenate([-w[..., half:], w[..., :half]], axis=-1)


def _even_in_weights(w_in, w_uq, w_ukv):
    d = w_in.shape[0]
    o = 0
    w_cq = w_in[:, o:o + Q_RANK]; o += Q_RANK
    w_ckv = w_in[:, o:o + KV_RANK]; o += KV_RANK
    w_kr = w_in[:, o:o + QK_ROPE]; o += QK_ROPE
    w_s = w_in[:, o:]
    pad_l = jnp.zeros((d, QK_NOPE), F32)
    pad_r = jnp.zeros((d, HEAD_LANES - QK_NOPE - QK_ROPE), F32)
    kr_blk = jnp.concatenate([pad_l, w_kr, pad_r], axis=1)
    kr_blk_p = jnp.concatenate([pad_l, _rope_partner(w_kr), pad_r], axis=1)
    w1 = jnp.concatenate([w_cq, w_ckv, kr_blk, kr_blk_p, w_s], axis=1).astype(BF16)

    dk = QK_NOPE + QK_ROPE
    wq = w_uq.reshape(Q_RANK, MLA_HEADS, dk)
    zq = jnp.zeros((Q_RANK, MLA_HEADS, HEAD_LANES - dk), F32)
    wq_pad = jnp.concatenate([wq, zq], axis=-1)
    wq_par = jnp.concatenate([jnp.zeros((Q_RANK, MLA_HEADS, QK_NOPE), F32), _rope_partner(wq[..., QK_NOPE:]), zq], axis=-1)
    wkv = w_ukv.reshape(KV_RANK, MLA_HEADS, QK_NOPE + V_DIM)
    wk = jnp.concatenate([wkv[..., :QK_NOPE], jnp.zeros((KV_RANK, MLA_HEADS, HEAD_LANES - QK_NOPE), F32)], axis=-1)
    wv = jnp.concatenate([wkv[..., QK_NOPE:], jnp.zeros((KV_RANK, MLA_HEADS, HEAD_LANES - V_DIM), F32)], axis=-1)
    flat = lambda w: w.reshape(w.shape[0], MLA_HEADS * HEAD_LANES).astype(BF16)
    return w1, flat(wq_pad), flat(wq_par), flat(wk), flat(wv)


def _rope_tables(n, n_ident):
    rows = n // GRID_W
    row = np.repeat(np.arange(rows, dtype=np.float32), GRID_W)
    col = np.tile(np.arange(GRID_W, dtype=np.float32), rows)
    n_freq = QK_ROPE // 4
    inv_freq = (np.float32(ROPE_BASE) ** (-np.arange(n_freq, dtype=np.float32) / n_freq)).astype(np.float32)
    ang = np.concatenate([row[:, None] * inv_freq, col[:, None] * inv_freq], axis=-1).astype(np.float64)
    ang = np.concatenate([ang, np.zeros((n_ident, QK_ROPE // 2))], axis=0)
    tot = n + n_ident
    pad = HEAD_LANES - QK_NOPE - QK_ROPE
    cos = np.concatenate([np.ones((tot, QK_NOPE)), np.cos(ang), np.cos(ang), np.ones((tot, pad))], axis=1)
    sin = np.concatenate([np.zeros((tot, QK_NOPE)), np.sin(ang), np.sin(ang), np.zeros((tot, pad))], axis=1)
    return jnp.asarray(cos, F32), jnp.asarray(sin, F32)


def _even_in(lat, tail, m, ffn1, g, weights, q_norm, kv_norm, *, n, tm, mod_row):
    n_lat_rows, d = lat.shape
    t = n_lat_rows + tail.shape[0]
    g1, wg1, wu1, wd1 = ffn1
    gg1 = g1.reshape(1, d)
    w1, wq, wqp, wk, wv = weights
    cos, sin = _rope_tables(n, tm)
    tiles_lat = n_lat_rows // tm
    tpb = n // tm
    tab_row = lambda i: jnp.where(i < tiles_lat, i % tpb, tpb)
    full = lambda a: pl.BlockSpec(a.shape, lambda i: (0,) * a.ndim, pipeline_mode=pl.Buffered(1))
    s_ch = w1.shape[1] - Q_RANK - KV_RANK - 2 * HEAD_LANES
    qn, kvn = q_norm.reshape(1, Q_RANK), kv_norm.reshape(1, KV_RANK)
    gg = g.reshape(1, d)
    head_spec = pl.BlockSpec((MLA_HEADS, tm, HEAD_LANES), lambda i: (0, i, 0))
    head_shape = jax.ShapeDtypeStruct((MLA_HEADS, t, HEAD_LANES), BF16)
    head_t_spec = pl.BlockSpec((MLA_HEADS, HEAD_LANES, tm), lambda i: (0, 0, i))
    head_t_shape = jax.ShapeDtypeStruct((MLA_HEADS, HEAD_LANES, t), BF16)
    pairs = s_ch // (2 * S5_GROUP)
    u_w = 2 * S5_GROUP * S5_CHUNK
    return pl.pallas_call(
        functools.partial(_even_in_kernel, tiles_first=tiles_lat),
        grid=(t // tm,),
        in_specs=[
            pl.BlockSpec((tm, d), lambda i: (jnp.minimum(i, tiles_lat - 1), 0)),
            pl.BlockSpec((tm, d), lambda i: (jnp.maximum(i - tiles_lat, 0), 0)),
            pl.BlockSpec((None, N_MOD, d), lambda i: (mod_row(i), 0, 0)),
            full(gg1), full(wg1), full(wu1), full(wd1),
            full(gg), full(w1), full(qn), full(kvn), full(wq), full(wqp), full(wk), full(wv),
            pl.BlockSpec((tm, HEAD_LANES), lambda i: (tab_row(i), 0)),
            pl.BlockSpec((tm, HEAD_LANES), lambda i: (tab_row(i), 0)),
        ],
        out_specs=[pl.BlockSpec((tm, d), lambda i: (i, 0)), head_t_spec, head_spec, head_t_spec,
                   pl.BlockSpec((pairs, tm // S5_CHUNK, u_w), lambda i: (0, i, 0))],
        out_shape=[jax.ShapeDtypeStruct((t, d), F32), head_t_shape, head_shape, head_t_shape,
                   jax.ShapeDtypeStruct((pairs, t // S5_CHUNK, u_w), BF16)],
        scratch_shapes=[pltpu.VMEM((s_ch // LANES, tm, LANES), F32)],
        compiler_params=_cparams(("parallel",)),
        name="ffn_even_in",
    )(lat, tail, m, gg1, wg1, wu1, wd1, gg, w1, qn, kvn, wq, wqp, wk, wv, cos, sin)


def _attn_kernel(qt_ref, kc_ref, vtc_ref, *rest, tk, lat):
    if lat:
        kl_ref, vtl_ref, o_ref = rest
    else:
        (o_ref,) = rest
    tq = qt_ref.shape[1]
    lanes = min(ATT_Q_LANES, tq)
    subs = [slice(j * lanes, (j + 1) * lanes) for j in range(tq // lanes)]

    def step(k, vt, carry):
        scores = [_dot(k, qt_ref[:, sl]) for sl in subs]
        out = []
        for s, (m, acc) in zip(scores, carry):
            m_new = jnp.maximum(m, jnp.max(s, axis=0, keepdims=True))
            p = jnp.exp2(s - m_new).astype(BF16)
            out.append((m_new, jnp.exp2(m - m_new) * acc + _dot(vt, p)))
        return tuple(out)

    carry = tuple((jnp.full((1, lanes), NEG_BIG, F32), jnp.zeros((ATT_V_ROWS, lanes), F32)) for _ in subs)
    carry = step(kc_ref[...], vtc_ref[:ATT_V_ROWS, :], carry)
    if lat:
        def body(c, carry):
            start = pl.multiple_of(c * tk, tk)
            return step(kl_ref[pl.ds(start, tk), :], vtl_ref[:ATT_V_ROWS, pl.ds(start, tk)], carry)

        n_chunks = kl_ref.shape[0] // tk
        unroll = max(u for u in (4, 2, 1) if n_chunks % u == 0)
        carry = lax.fori_loop(0, n_chunks, body, carry, unroll=unroll)
    for sl, (_, acc) in zip(subs, carry):
        o = acc / acc[ONES_LANE:ONES_LANE + 1]
        o = jnp.concatenate([o, jnp.zeros((HEAD_LANES - ATT_V_ROWS, lanes), F32)], axis=0)
        o_ref[sl, :] = o.T.astype(BF16)


def _attention(qt, k, vt, *, b, n, nc):
    h = k.shape[0]
    tq = min(ATT_Q_TILE, n)
    n_q = n // tq
    tk = _row_tile(n)
    lat_blocks = (b * n) // nc
    o_lat = pl.pallas_call(
        functools.partial(_attn_kernel, tk=tk, lat=True),
        grid=(b, h, n_q),
        in_specs=[
            pl.BlockSpec((None, HEAD_LANES, tq), lambda bi, hi, i: (hi, 0, bi * n_q + i)),
            pl.BlockSpec((None, nc, HEAD_LANES), lambda bi, hi, i: (hi, lat_blocks + bi, 0)),
            pl.BlockSpec((None, HEAD_LANES, nc), lambda bi, hi, i: (hi, 0, lat_blocks + bi)),
            pl.BlockSpec((None, n, HEAD_LANES), lambda bi, hi, i: (hi, bi, 0)),
            pl.BlockSpec((None, HEAD_LANES, n), lambda bi, hi, i: (hi, 0, bi)),
        ],
        out_specs=pl.BlockSpec((None, tq, HEAD_LANES), lambda bi, hi, i: (hi, bi * n_q + i, 0)),
        out_shape=jax.ShapeDtypeStruct((h, b * n, HEAD_LANES), BF16),
        compiler_params=_cparams(("parallel", "parallel", "arbitrary")),
        name="attention",
    )(qt, k, vt, k, vt)
    o_ctx = pl.pallas_call(
        functools.partial(_attn_kernel, tk=tk, lat=False),
        grid=(b, h),
        in_specs=[
            pl.BlockSpec((None, HEAD_LANES, nc), lambda bi, hi: (hi, 0, lat_blocks + bi)),
            pl.BlockSpec((None, nc, HEAD_LANES), lambda bi, hi: (hi, lat_blocks + bi, 0)),
            pl.BlockSpec((None, HEAD_LANES, nc), lambda bi, hi: (hi, 0, lat_blocks + bi)),
        ],
        out_specs=pl.BlockSpec((None, nc, HEAD_LANES), lambda bi, hi: (hi, bi, 0)),
        out_shape=jax.ShapeDtypeStruct((h, b * nc, HEAD_LANES), BF16),
        compiler_params=_cparams(("parallel", "parallel")),
        name="attention_ctx",
    )(qt, k, vt)
    return o_lat, o_ctx


def _s5_operators(lam_re, lam_im, log_step, b_re, b_im, c_re, c_im, d_skip):
    tc = S5_CHUNK
    g_n, p_n = lam_re.shape[1], lam_re.shape[2]
    lam = lax.complex(lam_re.astype(F32), lam_im.astype(F32))
    step = jnp.exp(log_step.astype(F32))[..., None]
    la = lam * step
    a_bar = jnp.exp(la)
    bb = ((a_bar - 1.0) / lam)[..., None] * lax.complex(b_re.astype(F32), b_im.astype(F32))
    cm = lax.complex(c_re.astype(F32), c_im.astype(F32))
    kk = jnp.arange(tc + 1, dtype=F32)
    apow = jnp.exp(la[..., None] * kk)

    win_f = jnp.einsum('gps,gpc->gscp', apow[0][..., tc - 1::-1][..., :tc], bb[0])
    win_b = jnp.einsum('gps,gpc->gscp', apow[1][..., :tc], bb[1])
    wout_f = jnp.einsum('gcp,gpt->gptc', cm[0], apow[0][..., 1:])
    wout_b = jnp.einsum('gcp,gpt->gptc', cm[1], apow[1][..., tc:0:-1])
    kf = jnp.real(jnp.einsum('gop,gpk,gpi->gkoi', cm[0], apow[0][..., :tc], bb[0]))
    kb = jnp.real(jnp.einsum('gop,gpk,gpi->gkoi', cm[1], apow[1][..., :tc], bb[1]))
    lag0 = kf[:, :1] + kb[:, :1] + jnp.eye(S5_GROUP, dtype=F32) * d_skip.astype(F32).reshape(g_n, 1, S5_GROUP, 1)
    by_lag = jnp.concatenate([kb[:, tc - 1:0:-1], lag0, kf[:, 1:]], axis=1)
    own = jax.nn.one_hot(jnp.arange(g_n) % 2, 2, dtype=F32)
    lag_rows = jnp.einsum('gkoi,gh->gikho', by_lag, own).reshape(g_n, S5_GROUP, (2 * tc - 1) * 2 * S5_GROUP)
    lane_pad = -lag_rows.shape[2] % LANES
    lag_rows = jnp.concatenate([lag_rows, jnp.zeros((g_n, S5_GROUP, lane_pad), F32)], axis=2)
    loc = _s5_local_operator(lag_rows, tc)

    gp = g_n // 2
    gc2 = 2 * tc * S5_GROUP
    eye2 = jnp.eye(2, dtype=F32)
    pair = lambda x: x.reshape((gp, 2) + x.shape[1:])
    w_in_part = lambda w: jnp.einsum('qgscp,gh->qsgchp', pair(w), eye2).reshape(gp, gc2, 2 * p_n)
    w_out_part = lambda w: jnp.einsum('qgptc,gh->qgpthc', pair(w), eye2).reshape(gp, 2 * p_n, gc2)
    w_in = jnp.concatenate([w_in_part(jnp.real(win_f)), w_in_part(jnp.imag(win_f)),
                            w_in_part(jnp.real(win_b)), w_in_part(jnp.imag(win_b))], axis=2)
    w_out = jnp.concatenate([w_out_part(jnp.real(wout_f)), w_out_part(-jnp.imag(wout_f)),
                             w_out_part(jnp.real(wout_b)), w_out_part(-jnp.imag(wout_b))], axis=1)
    a_tc = apow[..., tc].reshape(2, gp, 2 * p_n)
    a_rows = jnp.stack([jnp.real(a_tc[0]), jnp.imag(a_tc[0]), jnp.real(a_tc[1]), jnp.imag(a_tc[1])], axis=1)
    a_rows = jnp.concatenate([a_rows, jnp.zeros((gp, SUBLANES - 4, 2 * p_n), F32)], axis=1)
    return w_in.astype(BF16), w_out.astype(BF16), loc, a_rows


def _s5_local_kernel(rows_ref, loc_ref, *, tc):
    pw = 2 * S5_GROUP
    for s in range(tc):
        for g in range(2):
            r0 = (s * 2 + g) * S5_GROUP
            loc_ref[r0:r0 + S5_GROUP, :] = rows_ref[g][:, (tc - 1 - s) * pw:(2 * tc - 1 - s) * pw].astype(BF16)


def _s5_local_operator(lag_rows, tc):
    g_n, ch, lanes = lag_rows.shape
    wid = 2 * tc * S5_GROUP
    return pl.pallas_call(
        functools.partial(_s5_local_kernel, tc=tc),
        grid=(g_n // 2,),
        in_specs=[pl.BlockSpec((2, ch, lanes), lambda q: (q, 0, 0))],
        out_specs=pl.BlockSpec((None, wid, wid), lambda q: (q, 0, 0)),
        out_shape=jax.ShapeDtypeStruct((g_n // 2, wid, wid), BF16),
        compiler_params=_cparams(("parallel",)),
        name="s5_local_operator",
    )(lag_rows)


def _s5_kernel(u_ref, win_ref, wout_ref, loc_ref, a_ref, y_ref, s_scr, h_scr, *, batch, lat_chunks, ctx_chunks):
    u = u_ref[...]
    w = a_ref.shape[1]
    n_parts = s_scr.shape[0]
    s_all = _dot(u, win_ref[...])
    for k in range(n_parts):
        s_scr[k] = s_all[:, k * w:(k + 1) * w]
    a = a_ref[...]
    bc = lambda r: jnp.broadcast_to(a[r:r + 1], (batch, w))
    arf, aif, arb, aib = bc(0), bc(1), bc(2), bc(3)

    def chunk(rows, carry, part, ar, ai):
        hr, hi = carry
        sr = s_scr[part, rows, :]
        si = s_scr[part + 1, rows, :]
        h_scr[part, rows, :] = hr
        h_scr[part + 1, rows, :] = hi
        return ar * hr - ai * hi + sr, ar * hi + ai * hr + si

    def segment(base, per):
        def body(j, carry):
            cf, cb = carry
            cf = chunk(pl.ds(base + j, batch, stride=per), cf, 0, arf, aif)
            cb = chunk(pl.ds(base + per - 1 - j, batch, stride=per), cb, 2, arb, aib)
            return cf, cb
        return body

    zero = (jnp.zeros((batch, w), F32), jnp.zeros((batch, w), F32))
    unroll = lambda trips: S5_SCAN_UNROLL if trips % S5_SCAN_UNROLL == 0 else 1
    carry = lax.fori_loop(0, ctx_chunks, segment(batch * lat_chunks, ctx_chunks), (zero, zero),
                          unroll=unroll(ctx_chunks))
    lax.fori_loop(0, lat_chunks, segment(0, lat_chunks), carry, unroll=unroll(lat_chunks))
    h_all = jnp.concatenate([h_scr[k] for k in range(n_parts)], axis=-1).astype(BF16)
    y_ref[...] = _dot(u, loc_ref[...]) + _dot(h_all, wout_ref[...])


def _s5(u, ops, *, b, n, nc):
    w_in, w_out, loc, a_rows = ops
    gp, r, wid = u.shape
    sw = w_in.shape[2]
    return pl.pallas_call(
        functools.partial(_s5_kernel, batch=b, lat_chunks=n // S5_CHUNK, ctx_chunks=nc // S5_CHUNK),
        grid=(gp,),
        in_specs=[
            pl.BlockSpec((None, r, wid), lambda g: (g, 0, 0)),
            pl.BlockSpec((None, wid, sw), lambda g: (g, 0, 0)),
            pl.BlockSpec((None, sw, wid), lambda g: (g, 0, 0)),
            pl.BlockSpec((None, wid, wid), lambda g: (g, 0, 0)),
            pl.BlockSpec((None, SUBLANES, sw // 4), lambda g: (g, 0, 0)),
        ],
        out_specs=pl.BlockSpec((None, r, wid), lambda g: (g, 0, 0)),
        out_shape=jax.ShapeDtypeStruct((gp, r, wid), F32),
        scratch_shapes=[pltpu.VMEM((4, r, sw // 4), F32), pltpu.VMEM((4, r, sw // 4), F32)],
        compiler_params=_cparams(("parallel",)),
        name="s5",
    )(u, w_in, w_out, loc, a_rows)


def _even_out_kernel(x_ref, m_ref, ol_ref, oc_ref, yc_ref, wglu_ref, bglu_ref, woo_ref, wos_ref,
                     g2_ref, wg_ref, wu_ref, wd_ref, out_ref, ys_scr, *, tiles_lat):
    pw = 2 * S5_GROUP
    per_tile = LANES // pw
    chunks = yc_ref.shape[1]
    for lt in range(ys_scr.shape[0]):
        by_pair = [yc_ref[lt * per_tile + l] for l in range(per_tile)]
        for t in range(S5_CHUNK):
            ys_scr[lt, pl.ds(t, chunks, stride=S5_CHUNK), :] = jnp.concatenate(
                [y[:, t * pw:(t + 1) * pw] for y in by_pair], axis=-1)
    y = _gelu_tanh(jnp.concatenate([ys_scr[lt] for lt in range(ys_scr.shape[0])], axis=-1))
    y = y * _sigmoid(_dot(y.astype(BF16), wglu_ref[...]) + bglu_ref[...])
    is_lat = pl.program_id(0) < tiles_lat
    oc = jnp.concatenate([jnp.where(is_lat, ol_ref[h], oc_ref[h]) for h in range(MLA_HEADS)], axis=-1)
    mix = _dot(oc, woo_ref[...]) + _dot(y.astype(BF16), wos_ref[...])
    m = m_ref[...]
    out_ref[...] = _ffn_half_step(x_ref[...] + m[5:6] * mix, m, g2_ref, wg_ref, wu_ref, wd_ref, 2)


def _even_out(tok, m, o_lat, o_ctx, yc, w_glu, b_glu, w_out, g2, wg2, wu2, wd2, *, tm, mod_row):
    t, d = tok.shape
    tiles_lat = o_lat.shape[1] // tm
    pairs, _, wid = yc.shape
    sc = pairs * 2 * S5_GROUP
    wo = w_out[:MLA_HEADS * V_DIM].reshape(MLA_HEADS, V_DIM, d)
    woo = jnp.concatenate([wo, jnp.zeros((MLA_HEADS, HEAD_LANES - V_DIM, d), F32)], axis=1)
    woo = woo.reshape(MLA_HEADS * HEAD_LANES, d).astype(BF16)
    wos = w_out[MLA_HEADS * V_DIM:].astype(BF16)
    wg = w_glu.astype(BF16)
    bg = b_glu.reshape(1, sc).astype(F32)
    gg2 = g2.reshape(1, d)
    full = lambda a: pl.BlockSpec(a.shape, lambda i: (0,) * a.ndim, pipeline_mode=pl.Buffered(1))
    return pl.pallas_call(
        functools.partial(_even_out_kernel, tiles_lat=tiles_lat),
        grid=(t // tm,),
        in_specs=[
            pl.BlockSpec((tm, d), lambda i: (i, 0)),
            pl.BlockSpec((None, N_MOD, d), lambda i: (mod_row(i), 0, 0)),
            pl.BlockSpec((MLA_HEADS, tm, HEAD_LANES), lambda i: (0, jnp.minimum(i, tiles_lat - 1), 0)),
            pl.BlockSpec((MLA_HEADS, tm, HEAD_LANES), lambda i: (0, jnp.maximum(i - tiles_lat, 0), 0)),
            pl.BlockSpec((pairs, tm // S5_CHUNK, wid), lambda i: (0, i, 0)),
            full(wg), full(bg), full(woo), full(wos), full(gg2), full(wg2), full(wu2), full(wd2),
        ],
        out_specs=pl.BlockSpec((tm, d), lambda i: (i, 0)),
        out_shape=jax.ShapeDtypeStruct((t, d), F32),
        scratch_shapes=[pltpu.VMEM((sc // LANES, tm, LANES), F32)],
        compiler_params=_cparams(("parallel",)),
        name="even_out_ffn",
    )(tok, m, o_lat, o_ctx, yc, wg, bg, woo, wos, gg2, wg2, wu2, wd2)


def _odd_in_kernel(x_ref, m_ref, g1_ref, wg_ref, wu_ref, wd_ref, g_ref, w_ref,
                   tok_out, hy_out, lx_out, lg_out, lxc_out, *, tiles_lat):
    m = m_ref[...]
    x = _ffn_half_step(x_ref[...], m, g1_ref, wg_ref, wu_ref, wd_ref, 0)
    hb = (_rms(x, g_ref[...]) * (1.0 + m[4:5]) + m[3:4]).astype(BF16)
    n_hy, n_lx = hy_out.shape[1], lx_out.shape[1]
    is_lat = pl.program_id(0) < tiles_lat

    @pl.when(is_lat)
    def _():
        tok_out[...] = x
        p = _dot(hb, w_ref[...])
        hy_out[...] = p[:, :n_hy]
        lx_out[...] = p[:, n_hy:n_hy + n_lx]
        lg_out[...] = p[:, n_hy + n_lx:]

    @pl.when(jnp.logical_not(is_lat))
    def _():
        lxc_out[...] = _dot(hb, w_ref[:, n_hy:n_hy + n_lx])


def _odd_in(tok, m, ffn1, g, w_in, *, hy_w, lru_w, tm, mod_row, n_lat_rows):
    t, d = tok.shape
    g1, wg1, wu1, wd1 = ffn1
    gg1 = g1.reshape(1, d)
    w = w_in.astype(BF16)
    gg = g.reshape(1, d)
    tiles_lat = n_lat_rows // tm
    full = lambda a: pl.BlockSpec(a.shape, lambda i: (0,) * a.ndim, pipeline_mode=pl.Buffered(1))
    lat_row = lambda c: pl.BlockSpec((tm, c), lambda i: (jnp.minimum(i, tiles_lat - 1), 0))
    ctx_row = lambda c: pl.BlockSpec((tm, c), lambda i: (jnp.maximum(i - tiles_lat, 0), 0))
    return pl.pallas_call(
        functools.partial(_odd_in_kernel, tiles_lat=tiles_lat),
        grid=(t // tm,),
        in_specs=[pl.BlockSpec((tm, d), lambda i: (i, 0)),
                  pl.BlockSpec((None, N_MOD, d), lambda i: (mod_row(i), 0, 0)),
                  full(gg1), full(wg1), full(wu1), full(wd1), full(gg), full(w)],
        out_specs=[lat_row(d), lat_row(hy_w), lat_row(lru_w), lat_row(lru_w), ctx_row(lru_w)],
        out_shape=[jax.ShapeDtypeStruct((n_lat_rows, c), F32) for c in (d, hy_w, lru_w, lru_w)]
        + [jax.ShapeDtypeStruct((t - n_lat_rows, lru_w), F32)],
        compiler_params=_cparams(("arbitrary",)),
        name="ffn_odd_in",
    )(tok, m, gg1, wg1, wu1, wd1, gg, w)


def _shift_down(cur, prev8, k):
    r = pltpu.roll(cur, k, 0)
    row = lax.broadcasted_iota(jnp.int32, prev8.shape, 0)
    head = jnp.where(row < k, pltpu.roll(prev8, k, 0), r[:SUBLANES])
    return jnp.concatenate([head, r[SUBLANES:]], axis=0)


def _shift_up(cur, next8, k):
    rows = cur.shape[0]
    r = pltpu.roll(cur, rows - k, 0)
    row = lax.broadcasted_iota(jnp.int32, next8.shape, 0)
    tail = jnp.where(row >= SUBLANES - k, pltpu.roll(next8, SUBLANES - k, 0), r[rows - SUBLANES:])
    return jnp.concatenate([r[:rows - SUBLANES], tail], axis=0)


def _halo_specs(tt, width, n_tiles, tile_of):
    per = tt // SUBLANES
    last = n_tiles * per - 1
    cur = pl.BlockSpec((None, tt, width), lambda b, i: (b, tile_of(i), 0))
    prev = pl.BlockSpec((None, SUBLANES, width), lambda b, i: (b, jnp.maximum(tile_of(i) * per - 1, 0), 0))
    nxt = pl.BlockSpec((None, SUBLANES, width), lambda b, i: (b, jnp.minimum((tile_of(i) + 1) * per, last), 0))
    return cur, prev, nxt


def _hyena_prep_kernel(cur_ref, prev_ref, next_ref, w_ref, b_ref, z_out, x0_out):
    i = pl.program_id(1)
    cur = cur_ref[...]
    prev8 = prev_ref[...] * (i > 0).astype(F32)
    next8 = next_ref[...] * (i < pl.num_programs(1) - 1).astype(F32)
    w = w_ref[...]
    u = w[0:1] * _shift_down(cur, prev8, 1) + w[1:2] * cur + w[2:3] * _shift_up(cur, next8, 1) + b_ref[...]
    c = z_out.shape[1]
    x0_out[...] = u[:, :c]
    z_out[...] = u[:, 2 * c:] * u[:, c:2 * c]


def _hyena_prep(hy, conv_w, conv_b, *, tt):
    b, n, c3 = hy.shape
    c = c3 // 3
    cur, prev, nxt = _halo_specs(tt, c3, n // tt, lambda i: i)
    w8 = jnp.concatenate([conv_w.astype(F32), jnp.zeros((SUBLANES - conv_w.shape[0], c3), F32)], axis=0)
    out_spec = pl.BlockSpec((None, tt, c), lambda bi, i: (bi, i, 0))
    return pl.pallas_call(
        _hyena_prep_kernel,
        grid=(b, n // tt),
        in_specs=[cur, prev, nxt, pl.BlockSpec((SUBLANES, c3), lambda bi, i: (0, 0)),
                  pl.BlockSpec((1, c3), lambda bi, i: (0, 0))],
        out_specs=[out_spec, out_spec],
        out_shape=[jax.ShapeDtypeStruct((b, n, c), F32)] * 2,
        compiler_params=_cparams(("parallel", "parallel")),
        name="hyena_prep",
    )(hy, hy, hy, w8, conv_b.reshape(1, c3).astype(F32))


def _gather_pitch(rows):
    p = -(-rows // SUBLANES)
    return (p + 1 - p % 2) * SUBLANES


class _Dft:
    def __init__(self, n):
        self.n = n
        self.N = 2 * n
        self.N2 = FFT_N2
        self.N1 = self.N // self.N2
        self.A = n // self.N2
        self.K1 = self.N1 // 2 + 1
        self.K1p = -(-self.K1 // SUBLANES) * SUBLANES
        self.y_pitch = _gather_pitch(2 * self.K1p)
        self.x_pitch = _gather_pitch(2 * self.N2)
        N, N1, N2, A, K1, K1p = self.N, self.N1, self.N2, self.A, self.K1, self.K1p
        b = np.arange(N2)[:, None, None]
        k1 = np.arange(K1p)[None, :, None]
        a = np.arange(A)[None, None, :]
        phi = 2.0 * np.pi * ((a * k1 % N1) / N1 + (b * k1 % N) / N)
        live = (k1 < K1).astype(np.float64)
        f1 = np.concatenate([np.cos(phi) * live, -np.sin(phi) * live], axis=1)
        f1 = np.concatenate([f1[0::2], f1[1::2]], axis=2)
        wgt = np.where((k1 == 0) | (k1 == N1 // 2), 1.0, 2.0) * live / N
        g1 = np.concatenate([np.cos(phi) * wgt, -np.sin(phi) * wgt], axis=1)
        g1 = np.transpose(g1, (0, 2, 1))
        ang = 2.0 * np.pi * (np.arange(N2)[:, None] * np.arange(N2)[None, :] % N2) / N2
        c, s = np.cos(ang), np.sin(ang)
        f2 = np.block([[c, s], [-s, c]])
        g2 = np.block([[c, -s], [s, c]])
        self.f1, self.f2, self.g2, self.g1 = (jnp.asarray(x, F32).astype(BF16) for x in (f1, f2, g2, g1))


def _dft_forward(src_ref, f1_ref, f2_ref, y_scr, put, dft):
    n2, a_n, k1p2, yp = dft.N2, dft.A, 2 * dft.K1p, dft.y_pitch

    zero = jnp.zeros((a_n, LANES), BF16)

    def stage1(i, _):
        b = 2 * i
        x0 = src_ref[pl.ds(b, a_n, stride=n2), :].astype(BF16)
        x1 = src_ref[pl.ds(b + 1, a_n, stride=n2), :].astype(BF16)
        rhs = jnp.concatenate([jnp.concatenate([x0, zero], axis=1), jnp.concatenate([zero, x1], axis=1)], axis=0)
        y = _dot(f1_ref[i], rhs)
        for j in range(2):
            r0 = pl.multiple_of((b + j) * yp, SUBLANES)
            y_scr[pl.ds(r0, k1p2), :] = y[:, j * LANES:(j + 1) * LANES]
        return 0

    lax.fori_loop(0, n2 // 2, stage1, 0, unroll=DFT_UNROLL // 2)

    def plane(k1):
        yr = y_scr[pl.ds(k1, n2, stride=yp), :]
        yi = y_scr[pl.ds(dft.K1p + k1, n2, stride=yp), :]
        return jnp.concatenate([yr, yi], axis=0).astype(BF16)

    def stage2_pair(i, _):
        k1 = 2 * i
        put(k1, _dot(f2_ref[...], jnp.concatenate([plane(k1), plane(k1 + 1)], axis=1)), 2)
        return 0

    pairs = (dft.K1 // 2) - (dft.K1 // 2) % (DFT_UNROLL // 2)
    lax.fori_loop(0, pairs, stage2_pair, 0, unroll=DFT_UNROLL // 2)
    for k1 in range(2 * pairs, dft.K1):
        put(k1, _dot(f2_ref[...], plane(k1)), 1)


def _hyena_filter_kernel(feat_ref, w1_ref, b1_ref, w2_ref, b2_ref, w3f_ref, w3b_ref, freq_ref, delta_ref,
                         f1_ref, f2_ref, h_out, filt_scr, y_scr, hid_scr, *, dft):
    n = dft.n
    rb = _row_tile(n)
    blocks = n // rb
    rows_of = lambda r: pl.ds(pl.multiple_of(r * rb, rb), rb)

    @pl.when(pl.program_id(0) == 0)
    def _():
        freq = freq_ref[...]
        w1_hi, w1_lo = _split(w1_ref[...])
        w2_hi, w2_lo = _split(w2_ref[...])

        def hidden(r, _):
            f_hi, f_lo = _split(feat_ref[rows_of(r), :])
            h = jnp.sin(freq[0:1] * (_dot3(f_hi, f_lo, w1_hi, w1_lo) + b1_ref[...]))
            h_hi, h_lo = _split(h)
            hid_scr[rows_of(r), :] = jnp.sin(freq[1:2] * (_dot3(h_hi, h_lo, w2_hi, w2_lo) + b2_ref[...]))
            return 0

        lax.fori_loop(0, blocks, hidden, 0)

    def fill_filter(w3_ref):
        w_hi, w_lo = _split(w3_ref[...])

        def blk(r, _):
            h_hi, h_lo = _split(hid_scr[rows_of(r), :])
            decay = jnp.exp(-feat_ref[rows_of(r), 0:1] * delta_ref[...])
            filt_scr[rows_of(r), :] = _dot3(h_hi, h_lo, w_hi, w_lo) * decay
            return 0

        lax.fori_loop(0, blocks, blk, 0)

    n2 = dft.N2
    fill_filter(w3f_ref)

    def put_fwd(k1, x, w):
        for j in range(w):
            h_out[k1 + j] = x[:, j * LANES:(j + 1) * LANES]

    _dft_forward(filt_scr, f1_ref, f2_ref, y_scr, put_fwd, dft)

    fill_filter(w3b_ref)
    first = lax.broadcasted_iota(jnp.int32, (SUBLANES, LANES), 0) == 0
    filt_scr[:SUBLANES, :] = jnp.where(first, 0.0, filt_scr[:SUBLANES, :])
    sign = jnp.where(lax.broadcasted_iota(jnp.int32, (2 * n2, LANES), 0) < n2, 1.0, -1.0)

    def put_bwd(k1, x, w):
        for j in range(w):
            h_out[k1 + j] = h_out[k1 + j] + sign * x[:, j * LANES:(j + 1) * LANES]

    _dft_forward(filt_scr, f1_ref, f2_ref, y_scr, put_bwd, dft)


def _hyena_features(n):
    t = np.linspace(0.0, 1.0, n, dtype=np.float32).astype(np.float64)[:, None]
    w = (2.0 * np.pi * np.arange(n, dtype=np.float64)[:, None] / n).astype(np.float32).astype(np.float64)
    bands = np.linspace(1e-4, HY_BANDS - 1, HY_BANDS, dtype=np.float32).astype(np.float64)[None, :]
    bw = (bands * w).astype(np.float32).astype(np.float64)
    z = np.concatenate([t, np.cos(bw), -np.sin(bw)], axis=-1)
    z = np.concatenate([z, np.zeros((n, LANES - z.shape[1]))], axis=-1)
    return jnp.asarray(z, F32)


def _hyena_filter_spectrum(n, w1, b1, w2, b2, w3, sin_freq, dft):
    ch = w3.shape[1] // 2
    hid = w1.shape[1]
    feat = _hyena_features(n)
    w1p = jnp.concatenate([w1.astype(F32), jnp.zeros((LANES - w1.shape[0], hid), F32)], axis=0)
    deltas = jnp.asarray(np.abs(np.linspace(HY_MIN_DECAY, HY_MAX_DECAY, ch, dtype=np.float32)), F32).reshape(1, ch)
    freq8 = jnp.concatenate([sin_freq.astype(F32), jnp.zeros((SUBLANES - 2, hid), F32)], axis=0)
    full = lambda a: pl.BlockSpec(a.shape, lambda c: (0,) * a.ndim, pipeline_mode=pl.Buffered(1))
    tiles = ch // LANES
    args = (feat, w1p, b1.reshape(1, hid).astype(F32), w2.astype(F32), b2.reshape(1, hid).astype(F32))
    return pl.pallas_call(
        functools.partial(_hyena_filter_kernel, dft=dft),
        grid=(tiles,),
        in_specs=[full(a) for a in args] + [
            pl.BlockSpec((hid, LANES), lambda c: (0, c)),
            pl.BlockSpec((hid, LANES), lambda c: (0, tiles + c)),
            full(freq8),
            pl.BlockSpec((1, LANES), lambda c: (0, c)),
            full(dft.f1), full(dft.f2),
        ],
        out_specs=pl.BlockSpec((dft.K1, 2 * dft.N2, LANES), lambda c: (0, 0, c)),
        out_shape=jax.ShapeDtypeStruct((dft.K1, 2 * dft.N2, ch), F32),
        scratch_shapes=[pltpu.VMEM((n, LANES), F32), pltpu.VMEM((dft.N2 * dft.y_pitch, LANES), F32),
                        pltpu.VMEM((n, hid), F32)],
        compiler_params=_cparams(("arbitrary",)),
        name="hyena_filter",
    )(*args, w3.astype(F32), w3.astype(F32), freq8, deltas, dft.f1, dft.f2)


def _hyena_conv_kernel(z_ref, h_ref, f1_ref, f2_ref, g2_ref, g1_ref, y_out, y_scr, x_scr, *, dft):
    n2, a_n, k1p, xp = dft.N2, dft.A, dft.K1p, dft.x_pitch

    def put(k1, x, w):
        hk = jnp.concatenate([h_ref[k1 + j] for j in range(w)], axis=1)
        xr, xi, hr, hi = x[:n2], x[n2:], hk[:n2], hk[n2:]
        prod = jnp.concatenate([xr * hr - xi * hi, xr * hi + xi * hr], axis=0).astype(BF16)
        inv = _dot(g2_ref[...], prod)
        for j in range(w):
            r0 = pl.multiple_of((k1 + j) * xp, SUBLANES)
            x_scr[pl.ds(r0, 2 * n2), :] = inv[:, j * LANES:(j + 1) * LANES]

    if k1p > dft.K1:
        x_scr[dft.K1 * xp:, :] = jnp.zeros(((k1p - dft.K1) * xp, LANES), F32)
    _dft_forward(z_ref, f1_ref, f2_ref, y_scr, put, dft)

    def last(b, _):
        yr = x_scr[pl.ds(b, k1p, stride=xp), :]
        yi = x_scr[pl.ds(n2 + b, k1p, stride=xp), :]
        y_out[pl.ds(b, a_n, stride=n2), :] = _dot(g1_ref[b], jnp.concatenate([yr, yi], axis=0).astype(BF16))
        return 0

    lax.fori_loop(0, n2, last, 0, unroll=DFT_UNROLL)


def _hyena_conv(z, spec, dft):
    b, n, ch = z.shape
    once = pl.Buffered(1)
    full = lambda a: pl.BlockSpec(a.shape, lambda c, bi: (0,) * a.ndim, pipeline_mode=once)
    io_spec = pl.BlockSpec((None, n, LANES), lambda c, bi: (bi, 0, c))
    return pl.pallas_call(
        functools.partial(_hyena_conv_kernel, dft=dft),
        grid=(ch // LANES, b),
        in_specs=[io_spec, pl.BlockSpec((dft.K1, 2 * dft.N2, LANES), lambda c, bi: (0, 0, c), pipeline_mode=once),
                  full(dft.f1), full(dft.f2), full(dft.g2), full(dft.g1)],
        out_specs=io_spec,
        out_shape=jax.ShapeDtypeStruct((b, n, ch), F32),
        scratch_shapes=[pltpu.VMEM((dft.N2 * dft.y_pitch, LANES), F32), pltpu.VMEM((dft.K1p * dft.x_pitch, LANES), F32)],
        compiler_params=_cparams(("parallel", "arbitrary")),
        name="hyena_conv",
    )(z, spec, dft.f1, dft.f2, dft.g2, dft.g1)


def _lru_tile_terms(cur, prev8, next8, conv_w, conv_b, wg, bg, sp, reverse):
    rows, c = cur.shape
    xc = (conv_w[0:1] * _shift_down(cur, prev8, 2) + conv_w[1:2] * _shift_down(cur, prev8, 1)
          + conv_w[2:3] * cur + conv_w[3:4] * _shift_up(cur, next8, 1) + conv_b)
    g = _dot(xc.astype(BF16), wg) + bg
    gate = 0.5 + 0.5 * jnp.tanh(0.5 * g)
    r, ig = gate[:, :c], gate[:, c:]
    log_a = -LRU_C * r * sp
    a = jnp.exp(log_a)
    th = jnp.tanh(log_a)
    bb = jnp.sqrt(-2.0 * th / (1.0 - th)) * (ig * xc)
    groups = rows // SUBLANES
    a, bb = a.reshape(groups, SUBLANES, c), bb.reshape(groups, SUBLANES, c)
    sub = lax.broadcasted_iota(jnp.int32, (groups, SUBLANES, c), 1)
    for k in (1, 2, 4):
        shift = SUBLANES - k if reverse else k
        ok = sub < SUBLANES - k if reverse else sub >= k
        a_s, b_s = pltpu.roll(a, shift, 1), pltpu.roll(bb, shift, 1)
        bb = jnp.where(ok, bb + a * b_s, bb)
        a = jnp.where(ok, a * a_s, a)
    return a.reshape(rows, c), bb.reshape(rows, c)


def _lru_kernel(ctx_ref, curf_ref, prevf_ref, nextf_ref, curb_ref, prevb_ref, nextb_ref,
                cw_ref, cb_ref, wgf_ref, wgb_ref, bgf_ref, bgb_ref, sp_ref,
                hf_out, hb_out, a_scr, b_scr, carry_scr, *, batch, tt):
    i = pl.program_id(0)
    nt = pl.num_programs(0)
    cw, cb = cw_ref[...], cb_ref[...]
    sp = sp_ref[...]
    c = cw.shape[1]
    params = ((wgf_ref, bgf_ref, sp[0:1]), (wgb_ref, bgb_ref, sp[1:2]))

    def carry_scan(rows, write):
        groups = rows // SUBLANES

        def body(j, carries):
            new = []
            for d in range(2):
                jj = j if d == 0 else groups - 1 - j
                r0 = pl.multiple_of(jj * SUBLANES, SUBLANES)
                for bi in range(batch):
                    h8 = b_scr[d, bi, pl.ds(r0, SUBLANES), :] + a_scr[d, bi, pl.ds(r0, SUBLANES), :] * carries[d * batch + bi]
                    write(d, bi, r0, h8)
                    edge = h8[SUBLANES - 1:SUBLANES] if d == 0 else h8[0:1]
                    new.append(jnp.broadcast_to(edge, (SUBLANES, c)))
            return tuple(new)

        init = tuple(carry_scr[d, bi] for d in range(2) for bi in range(batch))
        out = lax.fori_loop(0, groups, body, init)
        for d in range(2):
            for bi in range(batch):
                carry_scr[d, bi] = out[d * batch + bi]

    @pl.when(i == 0)
    def _():
        carry_scr[...] = jnp.zeros(carry_scr.shape, F32)
        nc = ctx_ref.shape[1]
        zero8 = jnp.zeros((SUBLANES, c), F32)
        for d in range(2):
            wg, bg, spd = params[d]
            for bi in range(batch):
                a, bb = _lru_tile_terms(ctx_ref[bi], zero8, zero8, cw, cb, wg[...], bg[...], spd, d == 1)
                a_scr[d, bi, :nc, :] = a
                b_scr[d, bi, :nc, :] = bb
        carry_scan(nc, lambda d, bi, r0, h8: None)

    tiles = ((curf_ref, prevf_ref, nextf_ref, i), (curb_ref, prevb_ref, nextb_ref, nt - 1 - i))
    for d in range(2):
        cur_ref, prev_ref, next_ref, ti = tiles[d]
        wg, bg, spd = params[d]
        has_prev = (ti > 0).astype(F32)
        has_next = (ti < nt - 1).astype(F32)
        for bi in range(batch):
            a, bb = _lru_tile_terms(cur_ref[bi], prev_ref[bi] * has_prev, next_ref[bi] * has_next,
                                    cw, cb, wg[...], bg[...], spd, d == 1)
            a_scr[d, bi, :tt, :] = a
            b_scr[d, bi, :tt, :] = bb

    def write(d, bi, r0, h8):
        if d == 0:
            hf_out[bi, pl.ds(r0, SUBLANES), :] = h8
        else:
            hb_out[bi, pl.ds(r0, SUBLANES), :] = h8

    carry_scan(tt, write)


def _lru_gate_weights(w_a, w_x):
    def dense(w):
        nb, bs = w.shape[0], w.shape[1]
        eye = jnp.eye(nb, dtype=F32)
        return jnp.einsum('hij,hk->hikj', w.astype(F32), eye).reshape(nb * bs, nb * bs)
    return [jnp.concatenate([dense(w_a[d]), dense(w_x[d])], axis=1).astype(BF16) for d in range(2)]


def _lru(lx_lat, lx_ctx, conv_w, conv_b, w_a, b_a, w_x, b_x, lam, *, tt):
    b, n, c = lx_lat.shape
    nc = lx_ctx.shape[1]
    nt = n // tt
    wgf, wgb = _lru_gate_weights(w_a, w_x)
    bgf = jnp.concatenate([b_a[0], b_x[0]]).reshape(1, 2 * c).astype(F32)
    bgb = jnp.concatenate([b_a[1], b_x[1]]).reshape(1, 2 * c).astype(F32)
    sp = jax.nn.softplus(-lam.astype(F32))
    sp8 = jnp.concatenate([sp, jnp.zeros((SUBLANES - 2, c), F32)], axis=0)
    cw8 = jnp.concatenate([conv_w.astype(F32), jnp.zeros((SUBLANES - conv_w.shape[0], c), F32)], axis=0)
    cb = conv_b.reshape(1, c).astype(F32)
    per = tt // SUBLANES
    last = n // SUBLANES - 1

    def specs(tile_of):
        cur = pl.BlockSpec((b, tt, c), lambda i: (0, tile_of(i), 0))
        prev = pl.BlockSpec((b, SUBLANES, c), lambda i: (0, jnp.maximum(tile_of(i) * per - 1, 0), 0))
        nxt = pl.BlockSpec((b, SUBLANES, c), lambda i: (0, jnp.minimum((tile_of(i) + 1) * per, last), 0))
        return [cur, prev, nxt]

    fwd_tile = lambda i: i
    bwd_tile = lambda i: nt - 1 - i
    full = lambda a: pl.BlockSpec(a.shape, lambda i: (0,) * a.ndim)
    rows_scr = max(tt, nc)
    return pl.pallas_call(
        functools.partial(_lru_kernel, batch=b, tt=tt),
        grid=(nt,),
        in_specs=[full(lx_ctx)] + specs(fwd_tile) + specs(bwd_tile)
        + [full(cw8), full(cb), full(wgf), full(wgb), full(bgf), full(bgb), full(sp8)],
        out_specs=[pl.BlockSpec((b, tt, c), lambda i: (0, i, 0)), pl.BlockSpec((b, tt, c), lambda i: (0, nt - 1 - i, 0))],
        out_shape=[jax.ShapeDtypeStruct((b, n, c), F32)] * 2,
        scratch_shapes=[pltpu.VMEM((2, b, rows_scr, c), F32), pltpu.VMEM((2, b, rows_scr, c), F32),
                        pltpu.VMEM((2, b, SUBLANES, c), F32)],
        compiler_params=_cparams(("arbitrary",)),
        name="rglru",
    )(lx_ctx, lx_lat, lx_lat, lx_lat, lx_lat, lx_lat, lx_lat, cw8, cb, wgf, wgb, bgf, bgb, sp8)


def _odd_out_kernel(x_ref, m_ref, yc_ref, z_ref, x0_ref, hb_ref, hf_ref, hbk_ref, lg_ref, wh_ref, wr_ref,
                    g2_ref, wg_ref, wu_ref, wd_ref, fn_ref, out_ref):
    z = z_ref[...]
    hy = x0_ref[...] * (yc_ref[...] + z * hb_ref[...])
    r = (hf_ref[...] + hbk_ref[...]) * _gelu_tanh(lg_ref[...])
    mix = _dot(hy.astype(BF16), wh_ref[...]) + _dot(r.astype(BF16), wr_ref[...])
    m = m_ref[...]
    y = _ffn_half_step(x_ref[...] + m[5:6] * mix, m, g2_ref, wg_ref, wu_ref, wd_ref, 2)
    out_ref[...] = _rms(y, fn_ref[...])


def _odd_out(tok, m, yc, z, x0, hy_bias, hf, hb, lg, w_out, g2, wg2, wu2, wd2, final_g, *, n_rows, tm, mod_row):
    d = tok.shape[1]
    c = yc.shape[1]
    wh, wr = w_out[:c].astype(BF16), w_out[c:].astype(BF16)
    hbias = hy_bias.reshape(1, c).astype(F32)
    gg2, fg = g2.reshape(1, d), final_g.reshape(1, d)
    full = lambda a: pl.BlockSpec(a.shape, lambda i: (0,) * a.ndim, pipeline_mode=pl.Buffered(1))
    row = lambda w: pl.BlockSpec((tm, w), lambda i: (i, 0))
    return pl.pallas_call(
        _odd_out_kernel,
        grid=(n_rows // tm,),
        in_specs=[row(d), pl.BlockSpec((None, N_MOD, d), lambda i: (mod_row(i), 0, 0)),
                  row(c), row(c), row(c), full(hbias), row(c), row(c), row(c), full(wh), full(wr),
                  full(gg2), full(wg2), full(wu2), full(wd2), full(fg)],
        out_specs=row(d),
        out_shape=jax.ShapeDtypeStruct((n_rows, d), F32),
        compiler_params=_cparams(("parallel",)),
        name="odd_out_ffn",
    )(tok, m, yc, z, x0, hbias, hf, hb, lg, wh, wr, gg2, wg2, wu2, wd2, fg)


def kernel(x, c, ctx, c_ctx, mod_w, mod_b, norm_ffn1, norm_mix, norm_ffn2, ffn1_w_gate, ffn1_w_up, ffn1_w_down, ffn2_w_gate, ffn2_w_up, ffn2_w_down, ev_w_in, mla_q_norm, mla_w_uq, mla_kv_norm, mla_w_ukv, s5_lambda_re, s5_lambda_im, s5_log_step, s5_b_re, s5_b_im, s5_c_re, s5_c_im, s5_d, s5_w_glu, s5_b_glu, ev_w_out, od_w_in, hy_conv_w, hy_conv_b, hy_filt_w1, hy_filt_b1, hy_filt_w2, hy_filt_b2, hy_filt_w3, hy_sin_freq, hy_bias, lru_conv_w, lru_conv_b, lru_w_a, lru_b_a, lru_w_x, lru_b_x, lru_lambda, od_w_out, final_norm):
    b, n, d = x.shape
    nc = ctx.shape[1]
    depth = mod_w.shape[0]
    assert depth == 2, "the trunk is laid out for one even and one odd layer"
    n_lat_rows = b * n
    tm = _row_tile(n, b * nc)
    tpb = n // tm
    mod_row = lambda i: jnp.minimum(i // tpb, b)

    cond8 = jnp.concatenate([c, c_ctx[None, :], jnp.zeros((SUBLANES - b - 1, d), F32)], axis=0)
    m = _adaln(cond8, mod_w, mod_b)
    ffn_weights = lambda ws, layer: tuple(_bf16_layer(w, layer) for w in ws)
    ffn1_w = (ffn1_w_gate, ffn1_w_up, ffn1_w_down)
    ffn2_w = (ffn2_w_gate, ffn2_w_up, ffn2_w_down)

    ev_w = _even_in_weights(ev_w_in[0], mla_w_uq[0], mla_w_ukv[0])
    ffn1 = (norm_ffn1[0],) + ffn_weights(ffn1_w, 0)
    tok, qt, k, vt, s_chunks = _even_in(x.reshape(n_lat_rows, d), ctx.reshape(b * nc, d), m[0], ffn1, norm_mix[0],
                                        ev_w, mla_q_norm[0], mla_kv_norm[0], n=n, tm=tm, mod_row=mod_row)
    o_lat, o_ctx = _attention(qt, k, vt, b=b, n=n, nc=nc)
    s5_ops = _s5_operators(s5_lambda_re[0], s5_lambda_im[0], s5_log_step[0], s5_b_re[0], s5_b_im[0],
                           s5_c_re[0], s5_c_im[0], s5_d[0])
    ys = _s5(s_chunks, s5_ops, b=b, n=n, nc=nc)
    tok = _even_out(tok, m[0], o_lat, o_ctx, ys, s5_w_glu[0], s5_b_glu[0], ev_w_out[0], norm_ffn2[0],
                    *ffn_weights(ffn2_w, 0), tm=tm, mod_row=mod_row)

    hy_w = hy_conv_w.shape[2]
    lru_w = lru_conv_w.shape[2]
    ffn1 = (norm_ffn1[1],) + ffn_weights(ffn1_w, 1)
    tok, hy, lx, lg, lx_ctx = _odd_in(tok, m[1], ffn1, norm_mix[1], od_w_in[0], hy_w=hy_w, lru_w=lru_w, tm=tm,
                                      mod_row=mod_row, n_lat_rows=n_lat_rows)
    tt = _row_tile(n)
    z, x0 = _hyena_prep(hy.reshape(b, n, hy_w), hy_conv_w[0], hy_conv_b[0], tt=tt)
    dft = _Dft(n)
    spec = _hyena_filter_spectrum(n, hy_filt_w1[0], hy_filt_b1[0], hy_filt_w2[0], hy_filt_b2[0], hy_filt_w3[0],
                                  hy_sin_freq[0], dft)
    yc = _hyena_conv(z, spec, dft)
    hf, hb = _lru(lx.reshape(b, n, lru_w), lx_ctx.reshape(b, nc, lru_w), lru_conv_w[0],
                  lru_conv_b[0], lru_w_a[0], lru_b_a[0], lru_w_x[0], lru_b_x[0], lru_lambda[0], tt=min(tt, 256))
    flat = lambda a: a.reshape(n_lat_rows, a.shape[2])
    lat = _odd_out(tok, m[1], flat(yc), flat(z), flat(x0), hy_bias[0], flat(hf), flat(hb), lg, od_w_out[0],
                   norm_ffn2[1], *ffn_weights(ffn2_w, 1), final_norm,
                   n_rows=n_lat_rows, tm=tm, mod_row=mod_row)
    return lat.reshape(b, n, d)
```
